```python
import jax, jax.numpy as jnp
from jax import lax
import numpy as np

D_MODEL = 1024
BATCH = 8
SEQ = 8192
DEPTH = 4

CHUNK = 64
Q_BLOCK = 128

MLA_HEADS = 8
MLA_NOPE = 64
MLA_ROPE = 32
MLA_V = 64
Q_LORA = 384
KV_LORA = 256
ROPE_THETA = 10000.0

FOX_HEADS = 8
FOX_HEAD_DIM = 64

CONV_CHANNELS = 512
CONV_WIDTH = 31

FFN_HIDDEN = -(-8 * D_MODEL // (3 * 256)) * 256

N_BRANCHES = 3
RMS_EPS = 1e-6
LN_EPS = 1e-5

IN_SIZES = (
    Q_LORA,
    KV_LORA,
    MLA_ROPE,
    FOX_HEADS * FOX_HEAD_DIM,
    FOX_HEADS * FOX_HEAD_DIM,
    FOX_HEADS * FOX_HEAD_DIM,
    FOX_HEADS,
    2 * CONV_CHANNELS,
    N_BRANCHES * D_MODEL,
)
IN_WIDTH = int(sum(IN_SIZES))
IN_SPLITS = [int(v) for v in np.cumsum(IN_SIZES)[:-1]]

kernel_name = "hybrid_mla_fox_conformer_gated"


def rmsnorm(x, g):
    xf = x.astype(jnp.float32)
    y = xf * lax.rsqrt(jnp.mean(xf * xf, axis=-1, keepdims=True) + RMS_EPS)
    return (y * g.astype(jnp.float32)).astype(x.dtype)


def layernorm(x, g, b):
    xf = x.astype(jnp.float32)
    mu = jnp.mean(xf, axis=-1, keepdims=True)
    var = jnp.mean(jnp.square(xf - mu), axis=-1, keepdims=True)
    y = (xf - mu) * lax.rsqrt(var + LN_EPS)
    return (y * g.astype(jnp.float32) + b.astype(jnp.float32)).astype(x.dtype)


def apply_rope(x, cos, sin):
    half = x.shape[-1] // 2
    x1 = x[..., :half].astype(jnp.float32)
    x2 = x[..., half:].astype(jnp.float32)
    return jnp.concatenate([x1 * cos - x2 * sin, x2 * cos + x1 * sin], axis=-1).astype(x.dtype)


def block_attention(q, k, v, scale, diag_mask, cum=None):
    S = q.shape[2]
    outs = []
    for q0 in range(0, S, Q_BLOCK):
        q1 = q0 + Q_BLOCK
        s = jnp.einsum('bhqd,bhkd->bhqk', q[:, :, q0:q1], k[:, :, :q1]).astype(jnp.float32) * scale
        if cum is not None:
            s = s + cum[:, :, q0:q1, None] - cum[:, :, None, :q1]
        mask = jnp.concatenate([jnp.ones((Q_BLOCK, q0), dtype=bool), diag_mask], axis=1)
        s = jnp.where(mask, s, -jnp.inf)
        p = jax.nn.softmax(s, axis=-1).astype(v.dtype)
        outs.append(jnp.einsum('bhqk,bhkd->bhqd', p, v[:, :, :q1]))
    return jnp.concatenate(outs, axis=2)


def causal_depthwise_conv(u, w, b):
    y = lax.conv_general_dilated(
        u, w[:, None, :], window_strides=(1,), padding=[(CONV_WIDTH - 1, 0)],
        dimension_numbers=('NWC', 'WIO', 'NWC'), feature_group_count=u.shape[-1])
    return y + b


def _fwd_setup_inputs(seed: int = 0) -> dict:
    key = jax.random.key(seed)
    ks = jax.random.split(key, 24)
    f32 = jnp.float32
    L = DEPTH

    def nrm(k, shape, fan_in, scale=1.0):
        return jax.random.normal(k, shape, f32) * (scale * fan_in ** -0.5)

    def gain(k, shape):
        return 1.0 + 0.05 * jax.random.normal(k, shape, f32)

    out_scale = (2.0 * DEPTH) ** -0.5
    x = jax.random.normal(ks[0], (BATCH, SEQ, D_MODEL), f32)
    positions = (jax.random.randint(ks[1], (BATCH, 1), 0, 4096, dtype=jnp.int32)
                 + jnp.arange(SEQ, dtype=jnp.int32)[None, :])
    b_forget = (jnp.linspace(1.0, 6.0, FOX_HEADS, dtype=f32)[None, :]
                + 0.1 * jax.random.normal(ks[10], (L, FOX_HEADS), f32))
    return {
        "x": x,
        "positions": positions,
        "norm_mix_g": gain(ks[2], (L, D_MODEL)),
        "w_in": nrm(ks[3], (L, D_MODEL, IN_WIDTH), D_MODEL),
        "b_gate": 0.01 * jax.random.normal(ks[4], (L, N_BRANCHES * D_MODEL), f32),
        "q_norm_g": gain(ks[5], (L, Q_LORA)),
        "w_uq": nrm(ks[6], (L, Q_LORA, MLA_HEADS * (MLA_NOPE + MLA_ROPE)), Q_LORA),
        "kv_norm_g": gain(ks[7], (L, KV_LORA)),
        "w_ukv": nrm(ks[8], (L, KV_LORA, MLA_HEADS * (MLA_NOPE + MLA_V)), KV_LORA),
        "b_forget": b_forget,
        "dw_kernel": nrm(ks[11], (L, CONV_WIDTH, CONV_CHANNELS), CONV_WIDTH),
        "dw_bias": 0.01 * jax.random.normal(ks[12], (L, CONV_CHANNELS), f32),
        "conv_ln_g": gain(ks[13], (L, CONV_CHANNELS)),
        "conv_ln_b": 0.01 * jax.random.normal(ks[14], (L, CONV_CHANNELS), f32),
        "w_bo_a": nrm(ks[15], (L, MLA_HEADS * MLA_V, D_MODEL), MLA_HEADS * MLA_V),
        "w_bo_b": nrm(ks[16], (L, FOX_HEADS * FOX_HEAD_DIM, D_MODEL), FOX_HEADS * FOX_HEAD_DIM),
        "w_bo_c": nrm(ks[17], (L, CONV_CHANNELS, D_MODEL), CONV_CHANNELS),
        "w_out": nrm(ks[18], (L, D_MODEL, D_MODEL), D_MODEL, out_scale),
        "norm_ffn_g": gain(ks[19], (L, D_MODEL)),
        "w_ffn_gate": nrm(ks[20], (L, D_MODEL, FFN_HIDDEN), D_MODEL),
        "w_ffn_up": nrm(ks[21], (L, D_MODEL, FFN_HIDDEN), D_MODEL),
        "w_ffn_down": nrm(ks[22], (L, FFN_HIDDEN, D_MODEL), FFN_HIDDEN, out_scale),
        "final_norm_g": gain(ks[23], (D_MODEL,)),
    }


def _fwd_reference(x, positions, norm_mix_g, w_in, b_gate, q_norm_g, w_uq, kv_norm_g, w_ukv,
              b_forget, dw_kernel, dw_bias, conv_ln_g, conv_ln_b, w_bo_a, w_bo_b, w_bo_c,
              w_out, norm_ffn_g, w_ffn_gate, w_ffn_up, w_ffn_down, final_norm_g):
    B, S, _ = x.shape

    inv_freq = 1.0 / (ROPE_THETA ** (jnp.arange(0, MLA_ROPE, 2, dtype=jnp.float32) / MLA_ROPE))
    ang = positions.astype(jnp.float32)[..., None] * inv_freq
    cos, sin = jnp.cos(ang), jnp.sin(ang)

    idx = jnp.arange(Q_BLOCK)
    chunk_mask = (idx[:, None] // CHUNK) >= (idx[None, :] // CHUNK)
    frame_mask = idx[:, None] >= idx[None, :]

    mla_scale = (MLA_NOPE + MLA_ROPE) ** -0.5
    fox_scale = FOX_HEAD_DIM ** -0.5

    for l in range(DEPTH):
        h = rmsnorm(x, norm_mix_g[l])
        proj = jnp.einsum('bsd,dn->bsn', h, w_in[l])
        (c_q, c_kv, k_r, q_b, k_b, v_b, f_logit, conv_in, gate_logit) = jnp.split(proj, IN_SPLITS, axis=-1)

        q_a = (rmsnorm(c_q, q_norm_g[l]) @ w_uq[l]).reshape(B, S, MLA_HEADS, MLA_NOPE + MLA_ROPE)
        q_rope = apply_rope(q_a[..., MLA_NOPE:], cos[:, :, None, :], sin[:, :, None, :])
        kv = (rmsnorm(c_kv, kv_norm_g[l]) @ w_ukv[l]).reshape(B, S, MLA_HEADS, MLA_NOPE + MLA_V)
        k_rope = apply_rope(k_r, cos, sin)
        k_rope = jnp.broadcast_to(k_rope[:, :, None, :], (B, S, MLA_HEADS, MLA_ROPE))
        qa = jnp.concatenate([q_a[..., :MLA_NOPE], q_rope], axis=-1).transpose(0, 2, 1, 3)
        ka = jnp.concatenate([kv[..., :MLA_NOPE], k_rope], axis=-1).transpose(0, 2, 1, 3)
        va = kv[..., MLA_NOPE:].transpose(0, 2, 1, 3)
        o_a = block_attention(qa, ka, va, mla_scale, chunk_mask)
        o_a = o_a.transpose(0, 2, 1, 3).reshape(B, S, MLA_HEADS * MLA_V) @ w_bo_a[l]

        log_f = jax.nn.log_sigmoid(f_logit.astype(jnp.float32) + b_forget[l].astype(jnp.float32))
        cum = jnp.cumsum(log_f, axis=1).transpose(0, 2, 1)
        qf = q_b.reshape(B, S, FOX_HEADS, FOX_HEAD_DIM).transpose(0, 2, 1, 3)
        kf = k_b.reshape(B, S, FOX_HEADS, FOX_HEAD_DIM).transpose(0, 2, 1, 3)
        vf = v_b.reshape(B, S, FOX_HEADS, FOX_HEAD_DIM).transpose(0, 2, 1, 3)
        o_b = block_attention(qf, kf, vf, fox_scale, frame_mask, cum)
        o_b = o_b.transpose(0, 2, 1, 3).reshape(B, S, FOX_HEADS * FOX_HEAD_DIM) @ w_bo_b[l]

        u = conv_in[..., :CONV_CHANNELS] * jax.nn.sigmoid(conv_in[..., CONV_CHANNELS:])
        u = causal_depthwise_conv(u, dw_kernel[l], dw_bias[l])
        u = jax.nn.silu(layernorm(u, conv_ln_g[l], conv_ln_b[l]))
        o_c = u @ w_bo_c[l]

        g = jax.nn.sigmoid(gate_logit + b_gate[l])
        g_a, g_b, g_c = jnp.split(g, N_BRANCHES, axis=-1)
        y = g_a * o_a + g_b * o_b + g_c * o_c
        x = x + y @ w_out[l]

        h = rmsnorm(x, norm_ffn_g[l])
        ff = jax.nn.silu(h @ w_ffn_gate[l]) * (h @ w_ffn_up[l])
        x = x + ff @ w_ffn_down[l]

    return rmsnorm(x, final_norm_g)


import jax as _jax
import jax.numpy as _jnp

TWIN_FORMAT = 'train_step'
FWD_PARAMS = ['x', 'positions', 'norm_mix_g', 'w_in', 'b_gate', 'q_norm_g', 'w_uq', 'kv_norm_g', 'w_ukv', 'b_forget', 'dw_kernel', 'dw_bias', 'conv_ln_g', 'conv_ln_b', 'w_bo_a', 'w_bo_b', 'w_bo_c', 'w_out', 'norm_ffn_g', 'w_ffn_gate', 'w_ffn_up', 'w_ffn_down', 'final_norm_g']
TWIN_WEIGHTS = ['norm_mix_g', 'w_in', 'b_gate', 'q_norm_g', 'w_uq', 'kv_norm_g', 'w_ukv', 'b_forget', 'dw_kernel', 'dw_bias', 'conv_ln_g', 'conv_ln_b', 'w_bo_a', 'w_bo_b', 'w_bo_c', 'w_out', 'norm_ffn_g', 'w_ffn_gate', 'w_ffn_up', 'w_ffn_down', 'final_norm_g']
TWIN_DIFF_INPUT = 'x'
TWIN_INPUTS = ['x', 'positions', 'norm_mix_g', 'w_in', 'b_gate', 'q_norm_g', 'w_uq', 'kv_norm_g', 'w_ukv', 'b_forget', 'dw_kernel', 'dw_bias', 'conv_ln_g', 'conv_ln_b', 'w_bo_a', 'w_bo_b', 'w_bo_c', 'w_out', 'norm_ffn_g', 'w_ffn_gate', 'w_ffn_up', 'w_ffn_down', 'final_norm_g', 'loss_target', 'm_norm_mix_g', 'm_w_in', 'm_b_gate', 'm_q_norm_g', 'm_w_uq', 'm_kv_norm_g', 'm_w_ukv', 'm_b_forget', 'm_dw_kernel', 'm_dw_bias', 'm_conv_ln_g', 'm_conv_ln_b', 'm_w_bo_a', 'm_w_bo_b', 'm_w_bo_c', 'm_w_out', 'm_norm_ffn_g', 'm_w_ffn_gate', 'm_w_ffn_up', 'm_w_ffn_down', 'm_final_norm_g', 'v_norm_mix_g', 'v_w_in', 'v_b_gate', 'v_q_norm_g', 'v_w_uq', 'v_kv_norm_g', 'v_w_ukv', 'v_b_forget', 'v_dw_kernel', 'v_dw_bias', 'v_conv_ln_g', 'v_conv_ln_b', 'v_w_bo_a', 'v_w_bo_b', 'v_w_bo_c', 'v_w_out', 'v_norm_ffn_g', 'v_w_ffn_gate', 'v_w_ffn_up', 'v_w_ffn_down', 'v_final_norm_g']
TWIN_OUTPUTS = ['loss', 'grad_x', 'grad_norm_mix_g', 'grad_w_in', 'grad_b_gate', 'grad_q_norm_g', 'grad_w_uq', 'grad_kv_norm_g', 'grad_w_ukv', 'grad_b_forget', 'grad_dw_kernel', 'grad_dw_bias', 'grad_conv_ln_g', 'grad_conv_ln_b', 'grad_w_bo_a', 'grad_w_bo_b', 'grad_w_bo_c', 'grad_w_out', 'grad_norm_ffn_g', 'grad_w_ffn_gate', 'grad_w_ffn_up', 'grad_w_ffn_down', 'grad_final_norm_g', 'delta_norm_mix_g', 'delta_w_in', 'delta_b_gate', 'delta_q_norm_g', 'delta_w_uq', 'delta_kv_norm_g', 'delta_w_ukv', 'delta_b_forget', 'delta_dw_kernel', 'delta_dw_bias', 'delta_conv_ln_g', 'delta_conv_ln_b', 'delta_w_bo_a', 'delta_w_bo_b', 'delta_w_bo_c', 'delta_w_out', 'delta_norm_ffn_g', 'delta_w_ffn_gate', 'delta_w_ffn_up', 'delta_w_ffn_down', 'delta_final_norm_g', 'new_m_norm_mix_g', 'new_m_w_in', 'new_m_b_gate', 'new_m_q_norm_g', 'new_m_w_uq', 'new_m_kv_norm_g', 'new_m_w_ukv', 'new_m_b_forget', 'new_m_dw_kernel', 'new_m_dw_bias', 'new_m_conv_ln_g', 'new_m_conv_ln_b', 'new_m_w_bo_a', 'new_m_w_bo_b', 'new_m_w_bo_c', 'new_m_w_out', 'new_m_norm_ffn_g', 'new_m_w_ffn_gate', 'new_m_w_ffn_up', 'new_m_w_ffn_down', 'new_m_final_norm_g', 'new_v_norm_mix_g', 'new_v_w_in', 'new_v_b_gate', 'new_v_q_norm_g', 'new_v_w_uq', 'new_v_kv_norm_g', 'new_v_w_ukv', 'new_v_b_forget', 'new_v_dw_kernel', 'new_v_dw_bias', 'new_v_conv_ln_g', 'new_v_conv_ln_b', 'new_v_w_bo_a', 'new_v_w_bo_b', 'new_v_w_bo_c', 'new_v_w_out', 'new_v_norm_ffn_g', 'new_v_w_ffn_gate', 'new_v_w_ffn_up', 'new_v_w_ffn_down', 'new_v_final_norm_g']
TWIN_LEAF_KINDS = {'loss': 'loss', 'grad_x': 'grad_x', 'grad_norm_mix_g': 'grad_w', 'grad_w_in': 'grad_w', 'grad_b_gate': 'grad_w', 'grad_q_norm_g': 'grad_w', 'grad_w_uq': 'grad_w', 'grad_kv_norm_g': 'grad_w', 'grad_w_ukv': 'grad_w', 'grad_b_forget': 'grad_w', 'grad_dw_kernel': 'grad_w', 'grad_dw_bias': 'grad_w', 'grad_conv_ln_g': 'grad_w', 'grad_conv_ln_b': 'grad_w', 'grad_w_bo_a': 'grad_w', 'grad_w_bo_b': 'grad_w', 'grad_w_bo_c': 'grad_w', 'grad_w_out': 'grad_w', 'grad_norm_ffn_g': 'grad_w', 'grad_w_ffn_gate': 'grad_w', 'grad_w_ffn_up': 'grad_w', 'grad_w_ffn_down': 'grad_w', 'grad_final_norm_g': 'grad_w', 'delta_norm_mix_g': 'delta_w', 'delta_w_in': 'delta_w', 'delta_b_gate': 'delta_w', 'delta_q_norm_g': 'delta_w', 'delta_w_uq': 'delta_w', 'delta_kv_norm_g': 'delta_w', 'delta_w_ukv': 'delta_w', 'delta_b_forget': 'delta_w', 'delta_dw_kernel': 'delta_w', 'delta_dw_bias': 'delta_w', 'delta_conv_ln_g': 'delta_w', 'delta_conv_ln_b': 'delta_w', 'delta_w_bo_a': 'delta_w', 'delta_w_bo_b': 'delta_w', 'delta_w_bo_c': 'delta_w', 'delta_w_out': 'delta_w', 'delta_norm_ffn_g': 'delta_w', 'delta_w_ffn_gate': 'delta_w', 'delta_w_ffn_up': 'delta_w', 'delta_w_ffn_down': 'delta_w', 'delta_final_norm_g': 'delta_w', 'new_m_norm_mix_g': 'new_m', 'new_m_w_in': 'new_m', 'new_m_b_gate': 'new_m', 'new_m_q_norm_g': 'new_m', 'new_m_w_uq': 'new_m', 'new_m_kv_norm_g': 'new_m', 'new_m_w_ukv': 'new_m', 'new_m_b_forget': 'new_m', 'new_m_dw_kernel': 'new_m', 'new_m_dw_bias': 'new_m', 'new_m_conv_ln_g': 'new_m', 'new_m_conv_ln_b': 'new_m', 'new_m_w_bo_a': 'new_m', 'new_m_w_bo_b': 'new_m', 'new_m_w_bo_c': 'new_m', 'new_m_w_out': 'new_m', 'new_m_norm_ffn_g': 'new_m', 'new_m_w_ffn_gate': 'new_m', 'new_m_w_ffn_up': 'new_m', 'new_m_w_ffn_down': 'new_m', 'new_m_final_norm_g': 'new_m', 'new_v_norm_mix_g': 'new_v', 'new_v_w_in': 'new_v', 'new_v_b_gate': 'new_v', 'new_v_q_norm_g': 'new_v', 'new_v_w_uq': 'new_v', 'new_v_kv_norm_g': 'new_v', 'new_v_w_ukv': 'new_v', 'new_v_b_forget': 'new_v', 'new_v_dw_kernel': 'new_v', 'new_v_dw_bias': 'new_v', 'new_v_conv_ln_g': 'new_v', 'new_v_conv_ln_b': 'new_v', 'new_v_w_bo_a': 'new_v', 'new_v_w_bo_b': 'new_v', 'new_v_w_bo_c': 'new_v', 'new_v_w_out': 'new_v', 'new_v_norm_ffn_g': 'new_v', 'new_v_w_ffn_gate': 'new_v', 'new_v_w_ffn_up': 'new_v', 'new_v_w_ffn_down': 'new_v', 'new_v_final_norm_g': 'new_v'}


def _forward(args):
    return _fwd_reference(*[args[k] for k in FWD_PARAMS])


def _output_shape():
    def fwd():
        inp = _fwd_setup_inputs(0)
        return _fwd_reference(*[inp[k] for k in FWD_PARAMS])
    out = _jax.eval_shape(fwd)
    return out.shape, out.dtype

N_MICROBATCH = 1
ADAM_LR = 0.001
ADAM_B1 = 0.9
ADAM_B2 = 0.999
ADAM_EPS = 1e-08
ADAM_WD = 0.01
ADAM_STEP = 10
PER_EXAMPLE_BATCH_AXIS = {'x': 0, 'positions': 0, 'loss_target': 0}
SHARED_INPUTS = []
_WEIGHT_DTYPES = {'norm_mix_g': _jnp.float32, 'w_in': _jnp.float32, 'b_gate': _jnp.float32, 'q_norm_g': _jnp.float32, 'w_uq': _jnp.float32, 'kv_norm_g': _jnp.float32, 'w_ukv': _jnp.float32, 'b_forget': _jnp.float32, 'dw_kernel': _jnp.float32, 'dw_bias': _jnp.float32, 'conv_ln_g': _jnp.float32, 'conv_ln_b': _jnp.float32, 'w_bo_a': _jnp.float32, 'w_bo_b': _jnp.float32, 'w_bo_c': _jnp.float32, 'w_out': _jnp.float32, 'norm_ffn_g': _jnp.float32, 'w_ffn_gate': _jnp.float32, 'w_ffn_up': _jnp.float32, 'w_ffn_down': _jnp.float32, 'final_norm_g': _jnp.float32}
MOMENT_SCALE = {'norm_mix_g': 4.551507e-02, 'w_in': 1.806243e-02, 'b_gate': 7.954964e-03, 'q_norm_g': 1.146639e-02, 'w_uq': 8.339163e-03, 'kv_norm_g': 2.133989e-02, 'w_ukv': 1.013933e-02, 'b_forget': 9.436775e-02, 'dw_kernel': 4.307872e-02, 'dw_bias': 9.677700e-02, 'conv_ln_g': 5.369483e-02, 'conv_ln_b': 5.308757e-02, 'w_bo_a': 8.123759e-03, 'w_bo_b': 1.708523e-02, 'w_bo_c': 3.103692e-02, 'w_out': 9.920977e-02, 'norm_ffn_g': 7.591145e-02, 'w_ffn_gate': 3.215034e-02, 'w_ffn_up': 3.142538e-02, 'w_ffn_down': 1.477140e-01, 'final_norm_g': 6.397008e+01}


def _to_microbatches(a, axis):
    t = _jnp.moveaxis(a, axis, 0)
    t = t.reshape((N_MICROBATCH, t.shape[0] // N_MICROBATCH) + t.shape[1:])
    return _jnp.moveaxis(t, 1, axis + 1)


def setup_inputs(seed: int = 0) -> dict:
    inp = _fwd_setup_inputs(seed)
    key = _jax.random.fold_in(_jax.random.key(seed), 7919)
    shape, _ = _output_shape()
    out = dict(inp)
    out["loss_target"] = _jax.random.normal(_jax.random.fold_in(key, 0), shape, _jnp.float32)
    for i, name in enumerate(TWIN_WEIGHTS):
        w = inp[name].astype(_jnp.float32)
        if MOMENT_SCALE is None:
            s = _jnp.sqrt(_jnp.mean(_jnp.square(w)) + 1e-30)
        else:
            s = MOMENT_SCALE[name]
        km, kv = _jax.random.split(_jax.random.fold_in(key, i + 1))
        out[name] = w
        out["m_" + name] = s * _jax.random.normal(km, w.shape, _jnp.float32)
        out["v_" + name] = (s * s) * _jax.random.uniform(kv, w.shape, _jnp.float32, 0.5, 1.5)
    if N_MICROBATCH > 1:
        for name, axis in PER_EXAMPLE_BATCH_AXIS.items():
            out[name] = _to_microbatches(out[name], axis)
    return {'x': out['x'], 'positions': out['positions'], 'norm_mix_g': out['norm_mix_g'], 'w_in': out['w_in'], 'b_gate': out['b_gate'], 'q_norm_g': out['q_norm_g'], 'w_uq': out['w_uq'], 'kv_norm_g': out['kv_norm_g'], 'w_ukv': out['w_ukv'], 'b_forget': out['b_forget'], 'dw_kernel': out['dw_kernel'], 'dw_bias': out['dw_bias'], 'conv_ln_g': out['conv_ln_g'], 'conv_ln_b': out['conv_ln_b'], 'w_bo_a': out['w_bo_a'], 'w_bo_b': out['w_bo_b'], 'w_bo_c': out['w_bo_c'], 'w_out': out['w_out'], 'norm_ffn_g': out['norm_ffn_g'], 'w_ffn_gate': out['w_ffn_gate'], 'w_ffn_up': out['w_ffn_up'], 'w_ffn_down': out['w_ffn_down'], 'final_norm_g': out['final_norm_g'], 'loss_target': out['loss_target'], 'm_norm_mix_g': out['m_norm_mix_g'], 'm_w_in': out['m_w_in'], 'm_b_gate': out['m_b_gate'], 'm_q_norm_g': out['m_q_norm_g'], 'm_w_uq': out['m_w_uq'], 'm_kv_norm_g': out['m_kv_norm_g'], 'm_w_ukv': out['m_w_ukv'], 'm_b_forget': out['m_b_forget'], 'm_dw_kernel': out['m_dw_kernel'], 'm_dw_bias': out['m_dw_bias'], 'm_conv_ln_g': out['m_conv_ln_g'], 'm_conv_ln_b': out['m_conv_ln_b'], 'm_w_bo_a': out['m_w_bo_a'], 'm_w_bo_b': out['m_w_bo_b'], 'm_w_bo_c': out['m_w_bo_c'], 'm_w_out': out['m_w_out'], 'm_norm_ffn_g': out['m_norm_ffn_g'], 'm_w_ffn_gate': out['m_w_ffn_gate'], 'm_w_ffn_up': out['m_w_ffn_up'], 'm_w_ffn_down': out['m_w_ffn_down'], 'm_final_norm_g': out['m_final_norm_g'], 'v_norm_mix_g': out['v_norm_mix_g'], 'v_w_in': out['v_w_in'], 'v_b_gate': out['v_b_gate'], 'v_q_norm_g': out['v_q_norm_g'], 'v_w_uq': out['v_w_uq'], 'v_kv_norm_g': out['v_kv_norm_g'], 'v_w_ukv': out['v_w_ukv'], 'v_b_forget': out['v_b_forget'], 'v_dw_kernel': out['v_dw_kernel'], 'v_dw_bias': out['v_dw_bias'], 'v_conv_ln_g': out['v_conv_ln_g'], 'v_conv_ln_b': out['v_conv_ln_b'], 'v_w_bo_a': out['v_w_bo_a'], 'v_w_bo_b': out['v_w_bo_b'], 'v_w_bo_c': out['v_w_bo_c'], 'v_w_out': out['v_w_out'], 'v_norm_ffn_g': out['v_norm_ffn_g'], 'v_w_ffn_gate': out['v_w_ffn_gate'], 'v_w_ffn_up': out['v_w_ffn_up'], 'v_w_ffn_down': out['v_w_ffn_down'], 'v_final_norm_g': out['v_final_norm_g']}


def _loss(weights, diff, rest, loss_target):
    with _jax.named_scope("forward"):
        args = {**rest, TWIN_DIFF_INPUT: diff, **{k: w.astype(_WEIGHT_DTYPES[k]) for k, w in weights.items()}}
        y = _forward(args)
    with _jax.named_scope("loss_head"):
        err = _jnp.square(y.astype(_jnp.float32) - loss_target)
        return 0.5 * _jnp.sum(_jnp.mean(err, axis=-1)) if err.ndim else 0.5 * err


def _adamw(w, g, m, v):
    m = ADAM_B1 * m + (1.0 - ADAM_B1) * g
    v = ADAM_B2 * v + (1.0 - ADAM_B2) * _jnp.square(g)
    m_hat = m / (1.0 - ADAM_B1 ** ADAM_STEP)
    v_hat = v / (1.0 - ADAM_B2 ** ADAM_STEP)
    delta = -ADAM_LR * (m_hat / (_jnp.sqrt(v_hat) + ADAM_EPS) + ADAM_WD * w)
    return delta, m, v


def reference(x, positions, norm_mix_g, w_in, b_gate, q_norm_g, w_uq, kv_norm_g, w_ukv, b_forget, dw_kernel, dw_bias, conv_ln_g, conv_ln_b, w_bo_a, w_bo_b, w_bo_c, w_out, norm_ffn_g, w_ffn_gate, w_ffn_up, w_ffn_down, final_norm_g, loss_target, m_norm_mix_g, m_w_in, m_b_gate, m_q_norm_g, m_w_uq, m_kv_norm_g, m_w_ukv, m_b_forget, m_dw_kernel, m_dw_bias, m_conv_ln_g, m_conv_ln_b, m_w_bo_a, m_w_bo_b, m_w_bo_c, m_w_out, m_norm_ffn_g, m_w_ffn_gate, m_w_ffn_up, m_w_ffn_down, m_final_norm_g, v_norm_mix_g, v_w_in, v_b_gate, v_q_norm_g, v_w_uq, v_kv_norm_g, v_w_ukv, v_b_forget, v_dw_kernel, v_dw_bias, v_conv_ln_g, v_conv_ln_b, v_w_bo_a, v_w_bo_b, v_w_bo_c, v_w_out, v_norm_ffn_g, v_w_ffn_gate, v_w_ffn_up, v_w_ffn_down, v_final_norm_g):
    given = dict(x=x, positions=positions, norm_mix_g=norm_mix_g, w_in=w_in, b_gate=b_gate, q_norm_g=q_norm_g, w_uq=w_uq, kv_norm_g=kv_norm_g, w_ukv=w_ukv, b_forget=b_forget, dw_kernel=dw_kernel, dw_bias=dw_bias, conv_ln_g=conv_ln_g, conv_ln_b=conv_ln_b, w_bo_a=w_bo_a, w_bo_b=w_bo_b, w_bo_c=w_bo_c, w_out=w_out, norm_ffn_g=norm_ffn_g, w_ffn_gate=w_ffn_gate, w_ffn_up=w_ffn_up, w_ffn_down=w_ffn_down, final_norm_g=final_norm_g, loss_target=loss_target, m_norm_mix_g=m_norm_mix_g, m_w_in=m_w_in, m_b_gate=m_b_gate, m_q_norm_g=m_q_norm_g, m_w_uq=m_w_uq, m_kv_norm_g=m_kv_norm_g, m_w_ukv=m_w_ukv, m_b_forget=m_b_forget, m_dw_kernel=m_dw_kernel, m_dw_bias=m_dw_bias, m_conv_ln_g=m_conv_ln_g, m_conv_ln_b=m_conv_ln_b, m_w_bo_a=m_w_bo_a, m_w_bo_b=m_w_bo_b, m_w_bo_c=m_w_bo_c, m_w_out=m_w_out, m_norm_ffn_g=m_norm_ffn_g, m_w_ffn_gate=m_w_ffn_gate, m_w_ffn_up=m_w_ffn_up, m_w_ffn_down=m_w_ffn_down, m_final_norm_g=m_final_norm_g, v_norm_mix_g=v_norm_mix_g, v_w_in=v_w_in, v_b_gate=v_b_gate, v_q_norm_g=v_q_norm_g, v_w_uq=v_w_uq, v_kv_norm_g=v_kv_norm_g, v_w_ukv=v_w_ukv, v_b_forget=v_b_forget, v_dw_kernel=v_dw_kernel, v_dw_bias=v_dw_bias, v_conv_ln_g=v_conv_ln_g, v_conv_ln_b=v_conv_ln_b, v_w_bo_a=v_w_bo_a, v_w_bo_b=v_w_bo_b, v_w_bo_c=v_w_bo_c, v_w_out=v_w_out, v_norm_ffn_g=v_norm_ffn_g, v_w_ffn_gate=v_w_ffn_gate, v_w_ffn_up=v_w_ffn_up, v_w_ffn_down=v_w_ffn_down, v_final_norm_g=v_final_norm_g)
    weights = {n: given[n] for n in TWIN_WEIGHTS}
    shared = {n: given[n] for n in SHARED_INPUTS}
    per_example = {n: given[n] for n in ['x', 'positions']}
    grad_fn = _jax.value_and_grad(_loss, argnums=(0, 1))

    def one_microbatch(ex, loss_target):
        ex = dict(ex)
        diff = ex.pop(TWIN_DIFF_INPUT)
        return grad_fn(weights, diff, {**shared, **ex}, loss_target)

    if N_MICROBATCH == 1:
        loss, (grad_w, grad_x) = one_microbatch(per_example, given["loss_target"])
    else:
        def body(carry, xs):
            loss_sum, grad_sum = carry
            l_k, (gw_k, gx_k) = one_microbatch(xs[0], xs[1])
            with _jax.named_scope("update"):
                return (loss_sum + l_k, _jax.tree.map(_jnp.add, grad_sum, gw_k)), gx_k

        init = (_jnp.zeros((), _jnp.float32), _jax.tree.map(_jnp.zeros_like, weights))
        (loss, grad_w), grad_x = _jax.lax.scan(body, init, (per_example, given["loss_target"]))
    with _jax.named_scope("update"):
        delta_w, new_m, new_v = {}, {}, {}
        for n in TWIN_WEIGHTS:
            delta_w[n], new_m[n], new_v[n] = _adamw(weights[n], grad_w[n], given["m_" + n], given["v_" + n])
    return (loss, grad_x, *[grad_w[n] for n in TWIN_WEIGHTS], *[delta_w[n] for n in TWIN_WEIGHTS],
            *[new_m[n] for n in TWIN_WEIGHTS], *[new_v[n] for n in TWIN_WEIGHTS])
```

```python
import functools

import numpy as np
import jax
import jax.numpy as jnp
from jax import lax
from jax.experimental import pallas as pl
from jax.experimental.pallas import tpu as pltpu

F32 = jnp.float32
BF16 = jnp.bfloat16

D_MODEL = 1024
DEPTH = 4
CHUNK = 64
MLA_HEADS, MLA_NOPE, MLA_ROPE, MLA_V = 8, 64, 32, 64
Q_LORA, KV_LORA = 384, 256
ROPE_THETA = 10000.0
FOX_HEADS, FOX_HEAD_DIM = 8, 64
CONV_CHANNELS, CONV_WIDTH = 512, 31
FFN_HIDDEN = 2816
RMS_EPS = 1e-6
LN_EPS = 1e-5
ADAM_LR, ADAM_B1, ADAM_B2, ADAM_EPS, ADAM_WD, ADAM_STEP = 0.001, 0.9, 0.999, 1e-08, 0.01, 10

N_DEV = 8
LANES = 128
VMEM_LIMIT_BYTES = 56 * 1024 * 1024
NEG_BIG = -1e30
ROPE_HALF = MLA_ROPE // 2
CONV_HALO = 32

COL_GATE = 0
COL_CONV = 3072
COL_QB = 4096
COL_KB = 4608
COL_VB = 5120
COL_CQ = 5632
COL_CKV = 6144
COL_SMALL = 6400
IN_COLS = 6656
CQ_PAD = 512
O_CQ, O_CKV, O_KR, O_QB, O_KB, O_VB, O_F, O_CONV, O_GATE, O_END = 0, 384, 640, 672, 1184, 1696, 2208, 2216, 3240, 6312

WEIGHT_NAMES = ['norm_mix_g', 'w_in', 'b_gate', 'q_norm_g', 'w_uq', 'kv_norm_g', 'w_ukv', 'b_forget', 'dw_kernel',
                'dw_bias', 'conv_ln_g', 'conv_ln_b', 'w_bo_a', 'w_bo_b', 'w_bo_c', 'w_out', 'norm_ffn_g',
                'w_ffn_gate', 'w_ffn_up', 'w_ffn_down', 'final_norm_g']
SHARDED = [('w_in', 'col'), ('w_uq', 'col'), ('w_ukv', 'col'), ('dw_kernel', 'col'), ('w_bo_a', 'col'),
           ('w_bo_b', 'col'), ('w_bo_c', 'col'), ('w_out', 'row'), ('w_ffn_gate', 'col'), ('w_ffn_up', 'col'),
           ('w_ffn_down', 'row')]
REPLICATED = ['norm_mix_g', 'b_gate', 'q_norm_g', 'kv_norm_g', 'b_forget', 'dw_bias', 'conv_ln_g', 'conv_ln_b',
              'norm_ffn_g', 'final_norm_g']
FLAT_ROW_MULTIPLE = 512


def _pick(n, cands):
    for c in cands:
        if n % c == 0:
            return c
    raise ValueError(f"no tile for {n}")


def _params(sem):
    return pltpu.CompilerParams(dimension_semantics=sem, vmem_limit_bytes=VMEM_LIMIT_BYTES)


def _mm(a, b, *, mode="nn", out_dtype=F32, add=None, name):
    if mode == "nn":
        (M, K), N = a.shape, b.shape[1]
    elif mode == "nt":
        (M, K), N = a.shape, b.shape[0]
    else:
        (K, M), N = a.shape, b.shape[1]
    tm = _pick(M, (1024, 512, 256, 128))
    tn = _pick(N, (512, 384, 256, 128))
    tk = K if K <= 1024 else _pick(K, (512, 384, 256, 128))
    nk = K // tk
    dims = {"nn": (((1,), (0,)), ((), ())), "nt": (((1,), (1,)), ((), ())), "tn": (((0,), (0,)), ((), ()))}[mode]
    has_add = add is not None

    def body(*refs):
        if has_add:
            a_ref, b_ref, add_ref, o_ref, acc_ref = refs
        else:
            a_ref, b_ref, o_ref, acc_ref = refs
        k = pl.program_id(2)

        @pl.when(k == 0)
        def _():
            acc_ref[...] = jnp.zeros_like(acc_ref)

        acc_ref[...] += lax.dot_general(a_ref[...].astype(BF16), b_ref[...].astype(BF16), dims,
                                        preferred_element_type=F32)

        @pl.when(k == nk - 1)
        def _():
            r = acc_ref[...]
            if has_add:
                r = r + add_ref[...]
            o_ref[...] = r.astype(o_ref.dtype)

    if mode == "nn":
        a_spec = pl.BlockSpec((tm, tk), lambda i, j, k: (i, k))
        b_spec = pl.BlockSpec((tk, tn), lambda i, j, k: (k, j))
    elif mode == "nt":
        a_spec = pl.BlockSpec((tm, tk), lambda i, j, k: (i, k))
        b_spec = pl.BlockSpec((tn, tk), lambda i, j, k: (j, k))
    else:
        a_spec = pl.BlockSpec((tk, tm), lambda i, j, k: (k, i))
        b_spec = pl.BlockSpec((tk, tn), lambda i, j, k: (k, j))
    o_spec = pl.BlockSpec((tm, tn), lambda i, j, k: (i, j))
    in_specs = [a_spec, b_spec] + ([o_spec] if has_add else [])
    args = (a, b) + ((add,) if has_add else ())
    return pl.pallas_call(
        body, name=name, grid=(M // tm, N // tn, nk), in_specs=in_specs, out_specs=o_spec,
        out_shape=jax.ShapeDtypeStruct((M, N), out_dtype),
        scratch_shapes=[pltpu.VMEM((tm, tn), F32)],
        compiler_params=_params(("parallel", "parallel", "arbitrary")),
    )(*args)


def _rowwise(name, fn, rows, row_ins, full_ins, outs, reds=(), tr=256):
    tr = min(tr, rows)
    assert rows % tr == 0
    n_r, n_f, n_o, n_d = len(row_ins), len(full_ins), len(outs), len(reds)

    def body(*refs):
        ins = [r[...] for r in refs[:n_r + n_f]]
        o_refs = refs[n_r + n_f:n_r + n_f + n_o]
        d_refs = refs[n_r + n_f + n_o:]
        res = fn(*ins)
        for o, v in zip(o_refs, res[:n_o]):
            o[...] = v.astype(o.dtype)
        if n_d:
            @pl.when(pl.program_id(0) == 0)
            def _():
                for d in d_refs:
                    d[...] = jnp.zeros_like(d)

            for d, v in zip(d_refs, res[n_o:]):
                d[...] += v

    in_specs = []
    for (arr, w, cidx, roff) in row_ins:
        in_specs.append(pl.BlockSpec((tr, w), functools.partial(lambda i, c, r: (i + r, c), c=cidx, r=roff)))
    for f in full_ins:
        in_specs.append(pl.BlockSpec(f.shape, lambda i: (0, 0)))
    out_specs = [pl.BlockSpec((tr, w), lambda i: (i, 0)) for (w, _) in outs]
    out_specs += [pl.BlockSpec((r, w), lambda i: (0, 0)) for (r, w) in reds]
    out_shape = [jax.ShapeDtypeStruct((rows, w), dt) for (w, dt) in outs]
    out_shape += [jax.ShapeDtypeStruct((r, w), F32) for (r, w) in reds]
    res = pl.pallas_call(
        body, name=name, grid=(rows // tr,), in_specs=in_specs, out_specs=out_specs, out_shape=out_shape,
        compiler_params=_params(("arbitrary",) if n_d else ("parallel",)),
    )(*[a for (a, _, _, _) in row_ins], *full_ins)
    return res


def _whole(arr, width=None, cidx=0, roff=0):
    return (arr, arr.shape[1] if width is None else width, cidx, roff)


def _colsum(v):
    return jnp.sum(v, axis=0, keepdims=True)


def _sigmoid(z):
    return 1.0 / (1.0 + jnp.exp(-z))


def _rms_fwd(x_in, g, n_true, name):
    def fn(x, g):
        x = x.astype(F32)
        r = lax.rsqrt(jnp.sum(x * x, axis=1, keepdims=True) * (1.0 / n_true) + RMS_EPS)
        return (x * r * g,)

    rows = x_in[0].shape[0]
    return _rowwise(name, fn, rows, [x_in], [g], [(x_in[1], BF16)])[0]


def _rms_bwd(x_in, dh, g, n_true, res, name):
    has_res = res is not None

    def fn(*a):
        if has_res:
            x, dh, rs, g = a
        else:
            x, dh, g = a
        x = x.astype(F32)
        dh = dh.astype(F32)
        r = lax.rsqrt(jnp.sum(x * x, axis=1, keepdims=True) * (1.0 / n_true) + RMS_EPS)
        xh = x * r
        dxh = dh * g
        dx = r * (dxh - xh * (jnp.sum(dxh * xh, axis=1, keepdims=True) * (1.0 / n_true)))
        if has_res:
            dx = dx + rs
        return dx, _colsum(dh * xh)

    rows, w = x_in[0].shape[0], x_in[1]
    ins = [x_in, _whole(dh)] + ([_whole(res)] if has_res else [])
    return _rowwise(name, fn, rows, ins, [g], [(w, F32)], [(1, w)])


def _att_tile(S):
    return min(512, S // 2)


def _visible(q_idx, k_idx, group):
    if group == 1:
        return q_idx >= k_idx
    return (q_idx // group) >= (k_idx // group)


def _attn_fwd(q, k, v, scale, group, cum, name):
    H, S, dk = q.shape
    dv = v.shape[-1]
    t = _att_tile(S)
    n = S // t
    bias = cum is not None

    def body(*refs):
        if bias:
            q_ref, k_ref, v_ref, cq_ref, ck_ref, o_ref, lse_ref = refs
        else:
            q_ref, k_ref, v_ref, o_ref, lse_ref = refs
        i = pl.program_id(1)
        qv = q_ref[0]

        def scores(j):
            start = pl.multiple_of(j * t, t)
            kj = k_ref[0, pl.ds(start, t), :]
            s = lax.dot_general(qv, kj, (((1,), (1,)), ((), ())), preferred_element_type=F32) * scale
            if bias:
                s = s + (cq_ref[0] - ck_ref[0, pl.ds(j, 1), :])
            return s, start

        def update(s, start, carry):
            m, l, acc = carry
            m_new = jnp.maximum(m, jnp.max(s, axis=1, keepdims=True))
            p = jnp.exp(s - m_new)
            a = jnp.exp(m - m_new)
            l = a * l + jnp.sum(p, axis=1, keepdims=True)
            vj = v_ref[0, pl.ds(start, t), :]
            acc = a * acc + jnp.dot(p.astype(BF16), vj, preferred_element_type=F32)
            return m_new, l, acc

        def step(j, carry):
            s, start = scores(j)
            return update(s, start, carry)

        carry = (jnp.full((t, 1), NEG_BIG, F32), jnp.zeros((t, 1), F32), jnp.zeros((t, dv), F32))
        carry = lax.fori_loop(0, i, step, carry)
        s, start = scores(i)
        r = lax.broadcasted_iota(jnp.int32, (t, t), 0)
        c = lax.broadcasted_iota(jnp.int32, (t, t), 1)
        s = jnp.where(_visible(r, c, group), s, NEG_BIG)
        m, l, acc = update(s, start, carry)
        o_ref[0] = (acc / l).astype(o_ref.dtype)
        lse_ref[0] = m + jnp.log(l)

    in_specs = [pl.BlockSpec((1, t, dk), lambda h, i: (h, i, 0)),
                pl.BlockSpec((1, S, dk), lambda h, i: (h, 0, 0)),
                pl.BlockSpec((1, S, dv), lambda h, i: (h, 0, 0))]
    args = [q, k, v]
    if bias:
        in_specs += [pl.BlockSpec((1, t, 1), lambda h, i: (h, i, 0)),
                     pl.BlockSpec((1, n, t), lambda h, i: (h, 0, 0))]
        args += [cum.reshape(H, S, 1), cum.reshape(H, n, t)]
    return pl.pallas_call(
        body, name=name, grid=(H, n), in_specs=in_specs,
        out_specs=[pl.BlockSpec((1, t, dv), lambda h, i: (h, i, 0)), pl.BlockSpec((1, t, 1), lambda h, i: (h, i, 0))],
        out_shape=[jax.ShapeDtypeStruct((H, S, dv), BF16), jax.ShapeDtypeStruct((H, S, 1), F32)],
        compiler_params=_params(("parallel", "arbitrary")),
    )(*args)


def _attn_bwd(q, k, v, do, lse, delta, scale, group, cum, name):
    H, S, dk = q.shape
    dv = v.shape[-1]
    t = _att_tile(S)
    n = S // t
    bias = cum is not None

    def body(*refs):
        if bias:
            q_ref, k_ref, v_ref, do_ref, lse_ref, dl_ref, cq_ref, ck_ref, dq_ref, dk_ref, dv_ref, dc_ref, dcq_ref = refs
        else:
            q_ref, k_ref, v_ref, do_ref, lse_ref, dl_ref, dq_ref, dk_ref, dv_ref = refs
        j = pl.program_id(1)

        @pl.when(j == 0)
        def _():
            dq_ref[...] = jnp.zeros_like(dq_ref)
            if bias:
                dcq_ref[...] = jnp.zeros_like(dcq_ref)

        kj = k_ref[0]
        vj = v_ref[0]

        def step(i, carry, masked):
            dk_acc, dv_acc, dc_acc = carry
            start = pl.multiple_of(i * t, t)
            qi = q_ref[0, pl.ds(start, t), :]
            doi = do_ref[0, pl.ds(start, t), :]
            s_t = lax.dot_general(kj, qi, (((1,), (1,)), ((), ())), preferred_element_type=F32) * scale
            if bias:
                s_t = s_t + (cq_ref[0, pl.ds(i, 1), :] - ck_ref[0])
            if masked:
                kr = lax.broadcasted_iota(jnp.int32, (t, t), 0)
                qc = lax.broadcasted_iota(jnp.int32, (t, t), 1)
                s_t = jnp.where(_visible(qc, kr, group), s_t, NEG_BIG)
            p_t = jnp.exp(s_t - lse_ref[0, pl.ds(i, 1), :])
            dp_t = lax.dot_general(vj, doi, (((1,), (1,)), ((), ())), preferred_element_type=F32)
            ds_t = p_t * (dp_t - dl_ref[0, pl.ds(i, 1), :])
            ds_b = ds_t.astype(BF16)
            dv_acc = dv_acc + jnp.dot(p_t.astype(BF16), doi, preferred_element_type=F32)
            dk_acc = dk_acc + jnp.dot(ds_b, qi, preferred_element_type=F32)
            if bias:
                dc_acc = dc_acc - jnp.sum(ds_t, axis=1, keepdims=True)
                dcq_ref[0, pl.ds(i, 1), :] += jnp.sum(ds_t, axis=0, keepdims=True)
            dq_ref[0, pl.ds(start, t), :] += lax.dot_general(
                ds_b, kj, (((0,), (0,)), ((), ())), preferred_element_type=F32) * scale
            return dk_acc, dv_acc, dc_acc

        carry = (jnp.zeros((t, dk), F32), jnp.zeros((t, dv), F32), jnp.zeros((t, 1), F32))
        carry = step(j, carry, True)
        carry = lax.fori_loop(j + 1, n, lambda i, cr: step(i, cr, False), carry)
        dk_ref[0] = carry[0] * scale
        dv_ref[0] = carry[1]
        if bias:
            dc_ref[0] = carry[2]

    in_specs = [pl.BlockSpec((1, S, dk), lambda h, j: (h, 0, 0)),
                pl.BlockSpec((1, t, dk), lambda h, j: (h, j, 0)),
                pl.BlockSpec((1, t, dv), lambda h, j: (h, j, 0)),
                pl.BlockSpec((1, S, dv), lambda h, j: (h, 0, 0)),
                pl.BlockSpec((1, n, t), lambda h, j: (h, 0, 0)),
                pl.BlockSpec((1, n, t), lambda h, j: (h, 0, 0))]
    args = [q, k, v, do, lse.reshape(H, n, t), delta.reshape(H, n, t)]
    out_specs = [pl.BlockSpec((1, S, dk), lambda h, j: (h, 0, 0)),
                 pl.BlockSpec((1, t, dk), lambda h, j: (h, j, 0)),
                 pl.BlockSpec((1, t, dv), lambda h, j: (h, j, 0))]
    out_shape = [jax.ShapeDtypeStruct((H, S, dk), F32), jax.ShapeDtypeStruct((H, S, dk), F32),
                 jax.ShapeDtypeStruct((H, S, dv), F32)]
    if bias:
        in_specs += [pl.BlockSpec((1, n, t), lambda h, j: (h, 0, 0)),
                     pl.BlockSpec((1, t, 1), lambda h, j: (h, j, 0))]
        args += [cum.reshape(H, n, t), cum.reshape(H, S, 1)]
        out_specs += [pl.BlockSpec((1, t, 1), lambda h, j: (h, j, 0)), pl.BlockSpec((1, n, t), lambda h, j: (h, 0, 0))]
        out_shape += [jax.ShapeDtypeStruct((H, S, 1), F32), jax.ShapeDtypeStruct((H, n, t), F32)]
    return pl.pallas_call(
        body, name=name, grid=(H, n), in_specs=in_specs, out_specs=out_specs, out_shape=out_shape,
        compiler_params=_params(("parallel", "arbitrary")),
    )(*args)


def _attn_delta(o, do, name):
    H, S, dv = o.shape

    def fn(o, do):
        return (jnp.sum(o.astype(F32) * do.astype(F32), axis=1, keepdims=True),)

    d = _rowwise(name, fn, H * S, [_whole(o.reshape(H * S, dv)), _whole(do.reshape(H * S, dv))], [], [(1, F32)],
                 tr=1024)[0]
    return d.reshape(H, S)


def _fox_gate_fwd(f_t, b, name):
    Hh, S = f_t.shape
    nb = S // LANES

    def body(f_ref, b_ref, cum_ref):
        r = lax.broadcasted_iota(jnp.int32, (LANES, LANES), 0)
        c = lax.broadcasted_iota(jnp.int32, (LANES, LANES), 1)
        upper = (r <= c).astype(F32)
        carry = jnp.zeros((Hh, 1), F32)
        for blk in range(nb):
            z = f_ref[:, blk * LANES:(blk + 1) * LANES] + b_ref[...]
            logf = jnp.minimum(z, 0.0) - jnp.log(1.0 + jnp.exp(-jnp.abs(z)))
            cs = jnp.dot(logf, upper, preferred_element_type=F32, precision=lax.Precision.HIGHEST) + carry
            cum_ref[:, blk * LANES:(blk + 1) * LANES] = cs
            carry = cs[:, LANES - 1:LANES]

    return pl.pallas_call(body, name=name, out_shape=jax.ShapeDtypeStruct((Hh, S), F32),
                          compiler_params=pltpu.CompilerParams(vmem_limit_bytes=VMEM_LIMIT_BYTES))(f_t, b)


def _fox_gate_bwd(f_t, b, dcum_k, dcum_q, name):
    Hh, S = f_t.shape
    nb = S // LANES

    def body(f_ref, b_ref, dck_ref, dcq_ref, dz_ref, db_ref):
        r = lax.broadcasted_iota(jnp.int32, (LANES, LANES), 0)
        c = lax.broadcasted_iota(jnp.int32, (LANES, LANES), 1)
        lower = (r >= c).astype(F32)
        carry = jnp.zeros((Hh, 1), F32)
        db = jnp.zeros((Hh, 1), F32)
        for blk in range(nb - 1, -1, -1):
            sl = slice(blk * LANES, (blk + 1) * LANES)
            rc = jnp.dot(dck_ref[:, sl] + dcq_ref[:, sl], lower, preferred_element_type=F32,
                         precision=lax.Precision.HIGHEST) + carry
            carry = rc[:, 0:1]
            z = f_ref[:, sl] + b_ref[...]
            dz = rc * (1.0 - _sigmoid(z))
            dz_ref[:, sl] = dz
            db = db + jnp.sum(dz, axis=1, keepdims=True)
        db_ref[...] = db

    return pl.pallas_call(
        body, name=name,
        out_shape=[jax.ShapeDtypeStruct((Hh, S), F32), jax.ShapeDtypeStruct((Hh, 1), F32)],
        compiler_params=pltpu.CompilerParams(vmem_limit_bytes=VMEM_LIMIT_BYTES))(f_t, b, dcum_k, dcum_q)


def _conv_tile(S):
    return min(512, S // 2)


def _glu(cin):
    a = cin[:, :CONV_CHANNELS].astype(F32)
    b = cin[:, CONV_CHANNELS:].astype(F32)
    return a * _sigmoid(b)


def _conv_taps(ext_ref, w_ref, ts, first):
    acc = jnp.zeros((ts, CONV_CHANNELS), F32)
    for j in range(CONV_WIDTH):
        acc = acc + w_ref[j:j + 1, :] * ext_ref[first + j:first + j + ts, :]
    return acc


def _fill_u0_ext(ext_ref, cin_ref, halo_ref, i):
    ext_ref[0:CONV_HALO, :] = jnp.where(i > 0, _glu(halo_ref[...]), 0.0)
    ext_ref[CONV_HALO:, :] = _glu(cin_ref[...])


def _conv_specs(ts, cidx):
    per = ts // CONV_HALO
    wide = 2 * CONV_CHANNELS
    return [pl.BlockSpec((ts, wide), lambda i: (i, cidx)),
            pl.BlockSpec((CONV_HALO, wide), lambda i: (jnp.maximum(i * per - 1, 0), cidx))]


def _conv_fwd(proj, w, bias, ln_g, ln_b, name):
    S = proj.shape[0]
    ts = _conv_tile(S)
    C = CONV_CHANNELS

    def body(cin_ref, halo_ref, w_ref, b_ref, g_ref, bb_ref, o_ref, ext_ref):
        _fill_u0_ext(ext_ref, cin_ref, halo_ref, pl.program_id(0))
        u1 = _conv_taps(ext_ref, w_ref, ts, CONV_HALO - (CONV_WIDTH - 1)) + b_ref[...]
        mu = jnp.mean(u1, axis=1, keepdims=True)
        xc = u1 - mu
        rstd = lax.rsqrt(jnp.mean(xc * xc, axis=1, keepdims=True) + LN_EPS)
        u2 = xc * rstd * g_ref[...] + bb_ref[...]
        o_ref[...] = (u2 * _sigmoid(u2)).astype(o_ref.dtype)

    vec = pl.BlockSpec((1, C), lambda i: (0, 0))
    return pl.pallas_call(
        body, name=name, grid=(S // ts,),
        in_specs=_conv_specs(ts, COL_CONV // (2 * C)) + [pl.BlockSpec((32, C), lambda i: (0, 0)), vec, vec, vec],
        out_specs=pl.BlockSpec((ts, C), lambda i: (i, 0)),
        out_shape=jax.ShapeDtypeStruct((S, C), BF16),
        scratch_shapes=[pltpu.VMEM((ts + CONV_HALO, C), F32)],
        compiler_params=_params(("parallel",)),
    )(proj, proj, w, bias, ln_g, ln_b)


def _conv_bwd_a(proj, du3, w, bias, ln_g, ln_b, name):
    S = proj.shape[0]
    ts = _conv_tile(S)
    C = CONV_CHANNELS
    first = CONV_HALO - (CONV_WIDTH - 1)

    def body(cin_ref, halo_ref, du3_ref, w_ref, b_ref, g_ref, bb_ref, du1_ref, dw_ref, dbias_ref, dg_ref, dbb_ref,
             ext_ref):
        i = pl.program_id(0)

        @pl.when(i == 0)
        def _():
            dw_ref[...] = jnp.zeros_like(dw_ref)
            dbias_ref[...] = jnp.zeros_like(dbias_ref)
            dg_ref[...] = jnp.zeros_like(dg_ref)
            dbb_ref[...] = jnp.zeros_like(dbb_ref)

        _fill_u0_ext(ext_ref, cin_ref, halo_ref, i)
        u1 = _conv_taps(ext_ref, w_ref, ts, first) + b_ref[...]
        mu = jnp.mean(u1, axis=1, keepdims=True)
        xc = u1 - mu
        rstd = lax.rsqrt(jnp.mean(xc * xc, axis=1, keepdims=True) + LN_EPS)
        xh = xc * rstd
        u2 = xh * g_ref[...] + bb_ref[...]
        sg = _sigmoid(u2)
        du2 = du3_ref[...].astype(F32) * (sg * (1.0 + u2 * (1.0 - sg)))
        dg_ref[...] += _colsum(du2 * xh)
        dbb_ref[...] += _colsum(du2)
        dxh = du2 * g_ref[...]
        du1 = rstd * (dxh - jnp.mean(dxh, axis=1, keepdims=True) - xh * jnp.mean(dxh * xh, axis=1, keepdims=True))
        du1_ref[...] = du1
        dbias_ref[...] += _colsum(du1)
        for j in range(CONV_WIDTH):
            dw_ref[j:j + 1, :] += _colsum(du1 * ext_ref[first + j:first + j + ts, :])

    vec = pl.BlockSpec((1, C), lambda i: (0, 0))
    taps = pl.BlockSpec((32, C), lambda i: (0, 0))
    return pl.pallas_call(
        body, name=name, grid=(S // ts,),
        in_specs=_conv_specs(ts, COL_CONV // (2 * C)) + [pl.BlockSpec((ts, C), lambda i: (i, 0)), taps, vec, vec, vec],
        out_specs=[pl.BlockSpec((ts, C), lambda i: (i, 0)), taps, vec, vec, vec],
        out_shape=[jax.ShapeDtypeStruct((S, C), F32), jax.ShapeDtypeStruct((32, C), F32)]
        + [jax.ShapeDtypeStruct((1, C), F32)] * 3,
        scratch_shapes=[pltpu.VMEM((ts + CONV_HALO, C), F32)],
        compiler_params=_params(("arbitrary",)),
    )(proj, proj, du3, w, bias, ln_g, ln_b)


def _conv_bwd_b(proj, du1, w, name):
    S = proj.shape[0]
    ts = _conv_tile(S)
    C = CONV_CHANNELS
    per = ts // CONV_HALO
    nblk = S // ts
    last_halo = S // CONV_HALO - 1

    def body(cin_ref, du1_ref, nxt_ref, w_ref, o_ref, ext_ref):
        i = pl.program_id(0)
        ext_ref[0:ts, :] = du1_ref[...]
        ext_ref[ts:, :] = jnp.where(i < nblk - 1, nxt_ref[...], 0.0)
        du0 = jnp.zeros((ts, C), F32)
        for j in range(CONV_WIDTH):
            off = CONV_WIDTH - 1 - j
            du0 = du0 + w_ref[j:j + 1, :] * ext_ref[off:off + ts, :]
        a = cin_ref[:, :C].astype(F32)
        sg = _sigmoid(cin_ref[:, C:].astype(F32))
        o_ref[:, :C] = (du0 * sg).astype(o_ref.dtype)
        o_ref[:, C:] = (du0 * a * sg * (1.0 - sg)).astype(o_ref.dtype)

    return pl.pallas_call(
        body, name=name, grid=(nblk,),
        in_specs=[pl.BlockSpec((ts, 2 * C), lambda i: (i, COL_CONV // (2 * C))),
                  pl.BlockSpec((ts, C), lambda i: (i, 0)),
                  pl.BlockSpec((CONV_HALO, C), lambda i: (jnp.minimum((i + 1) * per, last_halo), 0)),
                  pl.BlockSpec((32, C), lambda i: (0, 0))],
        out_specs=pl.BlockSpec((ts, 2 * C), lambda i: (i, 0)),
        out_shape=jax.ShapeDtypeStruct((S, 2 * C), BF16),
        scratch_shapes=[pltpu.VMEM((ts + CONV_HALO, C), F32)],
        compiler_params=_params(("parallel",)),
    )(proj, du1, du1, w)


def _mesh_pos():
    return lax.axis_index("x"), lax.axis_index("y"), lax.axis_index("c")


def _exchange(x, gather, name):
    R = x.shape[-2]

    def body(x_ref, out_ref, send_sems, recv_sems, local_sem):
        mx, my, mc = _mesh_pos()
        me = 4 * mx + 2 * my + mc

        def src(dst_dev):
            return x_ref if gather else x_ref.at[dst_dev]

        local = pltpu.make_async_copy(src(me), out_ref.at[me], local_sem)
        local.start()
        copies = []
        for k in range(1, N_DEV):
            px, py, pc = mx ^ (k >> 2), my ^ ((k >> 1) & 1), mc ^ (k & 1)
            peer = 4 * px + 2 * py + pc
            cp = pltpu.make_async_remote_copy(
                src_ref=src(peer), dst_ref=out_ref.at[me], send_sem=send_sems.at[k - 1], recv_sem=recv_sems.at[k - 1],
                device_id=(px, py, pc), device_id_type=pl.DeviceIdType.MESH)
            cp.start()
            copies.append(cp)
        for cp in copies:
            cp.wait_recv()
        for cp in copies:
            cp.wait_send()
        local.wait()

    return pl.pallas_call(
        body, name=name,
        in_specs=[pl.BlockSpec(memory_space=pl.ANY)], out_specs=pl.BlockSpec(memory_space=pl.ANY),
        out_shape=jax.ShapeDtypeStruct((N_DEV, R, LANES), x.dtype),
        scratch_shapes=[pltpu.SemaphoreType.DMA((N_DEV - 1,)), pltpu.SemaphoreType.DMA((N_DEV - 1,)),
                        pltpu.SemaphoreType.DMA(())],
    )(x)


def _flat_rows(n_elems, multiple):
    rows = -(-n_elems // LANES)
    return -(-rows // multiple) * multiple


def _pack_flat(arrs, multiple):
    flat = jnp.concatenate([a.reshape(-1) for a in arrs])
    rows = _flat_rows(flat.shape[0], multiple)
    flat = jnp.pad(flat, (0, rows * LANES - flat.shape[0]))
    return flat.reshape(rows, LANES)


def _unpack_flat(flat, shapes):
    v = flat.reshape(-1)
    out, off = [], 0
    for s in shapes:
        n = int(np.prod(s))
        out.append(v[off:off + n].reshape(s))
        off += n
    return out


def _adamw_sum(name, g_slabs, w, m, v, tr):
    R = w.shape[0]
    c1 = 1.0 - ADAM_B1 ** ADAM_STEP
    c2 = 1.0 - ADAM_B2 ** ADAM_STEP

    def fn(*a):
        g = a[0]
        for d in range(1, N_DEV):
            g = g + a[d]
        w, m, v = a[N_DEV:]
        m_new = ADAM_B1 * m + (1.0 - ADAM_B1) * g
        v_new = ADAM_B2 * v + (1.0 - ADAM_B2) * (g * g)
        delta = -ADAM_LR * ((m_new / c1) / (jnp.sqrt(v_new / c2) + ADAM_EPS) + ADAM_WD * w)
        return g, delta, m_new, v_new

    tr = min(tr, R)
    ins = [(g_slabs, LANES, 0, d * (R // tr)) for d in range(N_DEV)] + [_whole(w), _whole(m), _whole(v)]
    return _rowwise(name, fn, R, ins, [], [(LANES, F32)] * 4, tr=tr)


def _full_weights(gathered, local_shapes):
    g2 = gathered.reshape(N_DEV, -1)
    out, off = {}, 0
    for (name, kind), shp in zip(SHARDED, local_shapes):
        n = int(np.prod(shp))
        seg = g2[:, off:off + n].reshape((N_DEV,) + tuple(shp))
        off += n
        L, a, b = shp
        if kind == 'col':
            out[name] = seg.transpose(1, 2, 0, 3).reshape(L, a, N_DEV * b)
        else:
            out[name] = seg.transpose(1, 0, 2, 3).reshape(L, N_DEV * a, b)
    return out


def _pack_grads(grads, local_shapes):
    parts = []
    for (name, kind), shp in zip(SHARDED, local_shapes):
        L, a, b = shp
        g = grads[name]
        if kind == 'col':
            parts.append(g.reshape(L, a, N_DEV, b).transpose(2, 0, 1, 3).reshape(N_DEV, -1))
        else:
            parts.append(g.reshape(L, N_DEV, a, b).transpose(1, 0, 2, 3).reshape(N_DEV, -1))
    flat = jnp.concatenate(parts, axis=1)
    rows = _flat_rows(flat.shape[1], FLAT_ROW_MULTIPLE)
    flat = jnp.pad(flat, ((0, 0), (0, rows * LANES - flat.shape[1])))
    return flat.reshape(N_DEV, rows, LANES)


def _rearrange_w_in(w):
    z = lambda n: jnp.zeros((w.shape[0], n), w.dtype)
    return jnp.concatenate([
        w[:, O_GATE:O_END], w[:, O_CONV:O_GATE], w[:, O_QB:O_KB], w[:, O_KB:O_VB], w[:, O_VB:O_F],
        w[:, O_CQ:O_CKV], z(CQ_PAD - Q_LORA), w[:, O_CKV:O_KR], w[:, O_KR:O_QB], w[:, O_F:O_CONV],
        z(LANES - MLA_ROPE - FOX_HEADS), z(IN_COLS - COL_SMALL - LANES)], axis=1)


def _restore_w_in(g):
    return jnp.concatenate([
        g[:, COL_CQ:COL_CQ + Q_LORA], g[:, COL_CKV:COL_CKV + KV_LORA], g[:, COL_SMALL:COL_SMALL + MLA_ROPE],
        g[:, COL_QB:COL_KB], g[:, COL_KB:COL_VB], g[:, COL_VB:COL_CQ],
        g[:, COL_SMALL + MLA_ROPE:COL_SMALL + MLA_ROPE + FOX_HEADS], g[:, COL_CONV:COL_QB], g[:, COL_GATE:COL_CONV]],
        axis=1)


def _rearrange_w_uq(w):
    w3 = w.reshape(Q_LORA, MLA_HEADS, MLA_NOPE + MLA_ROPE)
    cols = jnp.concatenate([w3[:, :, :MLA_NOPE].reshape(Q_LORA, -1),
                            w3[:, :, MLA_NOPE:MLA_NOPE + ROPE_HALF].reshape(Q_LORA, -1),
                            w3[:, :, MLA_NOPE + ROPE_HALF:].reshape(Q_LORA, -1)], axis=1)
    return jnp.pad(cols, ((0, CQ_PAD - Q_LORA), (0, 0)))


def _restore_w_uq(g):
    g = g[:Q_LORA]
    n = MLA_HEADS * MLA_NOPE
    h = MLA_HEADS * ROPE_HALF
    parts = [g[:, :n].reshape(Q_LORA, MLA_HEADS, MLA_NOPE), g[:, n:n + h].reshape(Q_LORA, MLA_HEADS, ROPE_HALF),
             g[:, n + h:].reshape(Q_LORA, MLA_HEADS, ROPE_HALF)]
    return jnp.concatenate(parts, axis=2).reshape(Q_LORA, -1)


def _rearrange_w_ukv(w):
    w3 = w.reshape(KV_LORA, MLA_HEADS, MLA_NOPE + MLA_V)
    return jnp.concatenate([w3[:, :, :MLA_NOPE].reshape(KV_LORA, -1), w3[:, :, MLA_NOPE:].reshape(KV_LORA, -1)], axis=1)


def _restore_w_ukv(g):
    n = MLA_HEADS * MLA_NOPE
    parts = [g[:, :n].reshape(KV_LORA, MLA_HEADS, MLA_NOPE), g[:, n:].reshape(KV_LORA, MLA_HEADS, MLA_V)]
    return jnp.concatenate(parts, axis=2).reshape(KV_LORA, -1)


def _heads(a, H):
    S = a.shape[0]
    return a.reshape(S, H, -1).transpose(1, 0, 2)


def _unheads(a):
    H, S, d = a.shape
    return a.transpose(1, 0, 2).reshape(S, H * d)


def _rope_q(x_src, cos, sin, name):
    def fn(x1, x2, c, s):
        return x1 * c - x2 * s, x2 * c + x1 * s

    S = x_src.shape[0]
    return _rowwise(name, fn, S, [(x_src, LANES, 4, 0), (x_src, LANES, 5, 0), _whole(cos), _whole(sin)], [],
                    [(LANES, F32), (LANES, F32)], tr=512)


def _rope_k(x_in, cos_k, sin_k, fold_heads, name):
    def fn(x, c, s):
        if fold_heads:
            x = x[:, :LANES] + x[:, LANES:]
            x = x + pltpu.roll(x, 64, 1)
            x = x + pltpu.roll(x, 32, 1)
        lane = lax.broadcasted_iota(jnp.int32, x.shape, 1)
        partner = jnp.where(lane < ROPE_HALF, pltpu.roll(x, LANES - ROPE_HALF, 1), pltpu.roll(x, ROPE_HALF, 1))
        return (x * c + partner * s,)

    S = x_in[0].shape[0]
    return _rowwise(name, fn, S, [x_in, _whole(cos_k), _whole(sin_k)], [], [(LANES, F32)], tr=512)[0]


def _layer_forward(x, W, T, l):
    S = x.shape[0]
    nm = lambda s: f"{s}_l{l}"
    h1 = _rms_fwd(_whole(x), W['norm_mix_g'], D_MODEL, nm("rms_mix"))
    proj = _mm(h1, W['w_in'], name=nm("mm_in"))
    small = proj[:, COL_SMALL:COL_SMALL + LANES]

    cqn = _rms_fwd((proj, CQ_PAD, COL_CQ // CQ_PAD, 0), W['q_norm_g'], Q_LORA, nm("rms_q"))
    ckvn = _rms_fwd((proj, KV_LORA, COL_CKV // KV_LORA, 0), W['kv_norm_g'], KV_LORA, nm("rms_kv"))
    qa = _mm(cqn, W['w_uq'], name=nm("mm_uq"))
    kv = _mm(ckvn, W['w_ukv'], out_dtype=BF16, name=nm("mm_ukv"))
    q_r1, q_r2 = _rope_q(qa, T['cos_q'], T['sin_q'], nm("rope_q"))
    k_rope = _rope_k(_whole(small), T['cos_k'], T['sin_k'], False, nm("rope_k"))[:, :MLA_ROPE]
    n_nope = MLA_HEADS * MLA_NOPE
    q_mla = jnp.concatenate([qa[:, :n_nope].reshape(S, MLA_HEADS, MLA_NOPE), q_r1.reshape(S, MLA_HEADS, ROPE_HALF),
                             q_r2.reshape(S, MLA_HEADS, ROPE_HALF)], axis=2).astype(BF16).transpose(1, 0, 2)
    k_mla = jnp.concatenate([kv[:, :n_nope].reshape(S, MLA_HEADS, MLA_NOPE),
                             jnp.broadcast_to(k_rope.astype(BF16)[:, None, :], (S, MLA_HEADS, MLA_ROPE))],
                            axis=2).transpose(1, 0, 2)
    v_mla = _heads(kv[:, n_nope:], MLA_HEADS)
    mla_scale = (MLA_NOPE + MLA_ROPE) ** -0.5
    o_mla, lse_a = _attn_fwd(q_mla, k_mla, v_mla, mla_scale, CHUNK, None, nm("mla_fwd"))
    oa_cat = _unheads(o_mla)
    o_a = _mm(oa_cat, W['w_bo_a'], name=nm("mm_bo_a"))

    f_t = small[:, MLA_ROPE:MLA_ROPE + FOX_HEADS].T
    cum = _fox_gate_fwd(f_t, W['b_forget'], nm("fox_gate"))
    q_fox = _heads(proj[:, COL_QB:COL_KB].astype(BF16), FOX_HEADS)
    k_fox = _heads(proj[:, COL_KB:COL_VB].astype(BF16), FOX_HEADS)
    v_fox = _heads(proj[:, COL_VB:COL_CQ].astype(BF16), FOX_HEADS)
    fox_scale = FOX_HEAD_DIM ** -0.5
    o_fox, lse_b = _attn_fwd(q_fox, k_fox, v_fox, fox_scale, 1, cum, nm("fox_fwd"))
    ob_cat = _unheads(o_fox)
    o_b = _mm(ob_cat, W['w_bo_b'], name=nm("mm_bo_b"))

    u3 = _conv_fwd(proj, W['dw_kernel'], W['dw_bias'], W['conv_ln_g'], W['conv_ln_b'], nm("conv_fwd"))
    o_c = _mm(u3, W['w_bo_c'], name=nm("mm_bo_c"))

    def gate_fn(la, lb, lc, oa, ob, oc, bg):
        ga = _sigmoid(la + bg[:, :D_MODEL])
        gb = _sigmoid(lb + bg[:, D_MODEL:2 * D_MODEL])
        gc = _sigmoid(lc + bg[:, 2 * D_MODEL:])
        return (ga * oa + gb * ob + gc * oc,)

    logit_ins = [(proj, D_MODEL, COL_GATE // D_MODEL + b, 0) for b in range(3)]
    y = _rowwise(nm("gate_fwd"), gate_fn, S, logit_ins + [_whole(o_a), _whole(o_b), _whole(o_c)], [W['b_gate']],
                 [(D_MODEL, BF16)])[0]
    x2 = _mm(y, W['w_out'], add=x, name=nm("mm_out"))

    h2 = _rms_fwd(_whole(x2), W['norm_ffn_g'], D_MODEL, nm("rms_ffn"))
    gu = _mm(h2, W['w_gu'], name=nm("mm_gu"))

    def swiglu_fn(gt, up):
        return (gt * _sigmoid(gt) * up,)

    ff = _rowwise(nm("swiglu_fwd"), swiglu_fn, S, [(gu, FFN_HIDDEN, 0, 0), (gu, FFN_HIDDEN, 1, 0)], [],
                  [(FFN_HIDDEN, BF16)])[0]
    x3 = _mm(ff, W['w_ffn_down'], add=x2, name=nm("mm_down"))

    saved = dict(x=x, h1=h1, proj=proj, small=small, cqn=cqn, ckvn=ckvn, q_mla=q_mla, k_mla=k_mla, v_mla=v_mla,
                 o_mla=o_mla, lse_a=lse_a, oa_cat=oa_cat, o_a=o_a, f_t=f_t, cum=cum, q_fox=q_fox, k_fox=k_fox,
                 v_fox=v_fox, o_fox=o_fox, lse_b=lse_b, ob_cat=ob_cat, o_b=o_b, u3=u3, o_c=o_c, y=y, x2=x2, h2=h2,
                 gu=gu, ff=ff)
    return x3, saved


def _layer_backward(dx3, sv, W, T, l):
    S = dx3.shape[0]
    nm = lambda s: f"{s}_l{l}"
    G = {}

    G['w_ffn_down'] = _mm(sv['ff'], dx3, mode="tn", name=nm("mm_down_dw"))
    dff = _mm(dx3, W['w_ffn_down'], mode="nt", name=nm("mm_down_dx"))

    def swiglu_bwd_fn(gt, up, d):
        sg = _sigmoid(gt)
        return (jnp.concatenate([d * up * (sg * (1.0 + gt * (1.0 - sg))), d * (gt * sg)], axis=1),)

    dgu = _rowwise(nm("swiglu_bwd"), swiglu_bwd_fn, S,
                   [(sv['gu'], FFN_HIDDEN, 0, 0), (sv['gu'], FFN_HIDDEN, 1, 0), _whole(dff)], [],
                   [(2 * FFN_HIDDEN, BF16)])[0]
    G['w_gu'] = _mm(sv['h2'], dgu, mode="tn", name=nm("mm_gu_dw"))
    dh2 = _mm(dgu, W['w_gu'], mode="nt", name=nm("mm_gu_dx"))
    dx2, G['norm_ffn_g'] = _rms_bwd(_whole(sv['x2']), dh2, W['norm_ffn_g'], D_MODEL, dx3, nm("rms_ffn_bwd"))

    G['w_out'] = _mm(sv['y'], dx2, mode="tn", name=nm("mm_out_dw"))
    dy = _mm(dx2, W['w_out'], mode="nt", name=nm("mm_out_dx"))

    def gate_bwd_fn(la, lb, lc, oa, ob, oc, dy, bg):
        outs, dls = [], []
        for k, (lg, o) in enumerate(((la, oa), (lb, ob), (lc, oc))):
            g = _sigmoid(lg + bg[:, k * D_MODEL:(k + 1) * D_MODEL])
            outs.append(dy * g)
            dls.append(dy * o * g * (1.0 - g))
        dl = jnp.concatenate(dls, axis=1)
        return (*outs, dl, _colsum(dl))

    proj = sv['proj']
    logit_ins = [(proj, D_MODEL, COL_GATE // D_MODEL + b, 0) for b in range(3)]
    do_a, do_b, do_c, dlogit, G['b_gate'] = _rowwise(
        nm("gate_bwd"), gate_bwd_fn, S, logit_ins + [_whole(sv['o_a']), _whole(sv['o_b']), _whole(sv['o_c']), _whole(dy)],
        [W['b_gate']], [(D_MODEL, BF16)] * 3 + [(3 * D_MODEL, BF16)], [(1, 3 * D_MODEL)], tr=128)

    G['w_bo_c'] = _mm(sv['u3'], do_c, mode="tn", name=nm("mm_bo_c_dw"))
    du3 = _mm(do_c, W['w_bo_c'], mode="nt", name=nm("mm_bo_c_dx"))
    du1, G['dw_kernel'], G['dw_bias'], G['conv_ln_g'], G['conv_ln_b'] = _conv_bwd_a(
        proj, du3, W['dw_kernel'], W['dw_bias'], W['conv_ln_g'], W['conv_ln_b'], nm("conv_bwd_a"))
    dconv = _conv_bwd_b(proj, du1, W['dw_kernel'], nm("conv_bwd_b"))

    G['w_bo_b'] = _mm(sv['ob_cat'], do_b, mode="tn", name=nm("mm_bo_b_dw"))
    dob = _heads(_mm(do_b, W['w_bo_b'], mode="nt", out_dtype=BF16, name=nm("mm_bo_b_dx")), FOX_HEADS)
    delta_b = _attn_delta(sv['o_fox'], dob, nm("fox_delta"))
    dq_f, dk_f, dv_f, dcum_k, dcum_q = _attn_bwd(
        sv['q_fox'], sv['k_fox'], sv['v_fox'], dob, sv['lse_b'].reshape(FOX_HEADS, S), delta_b, FOX_HEAD_DIM ** -0.5, 1,
        sv['cum'], nm("fox_bwd"))
    dz, G['b_forget'] = _fox_gate_bwd(sv['f_t'], W['b_forget'], dcum_k.reshape(FOX_HEADS, S),
                                      dcum_q.reshape(FOX_HEADS, S), nm("fox_gate_bwd"))

    G['w_bo_a'] = _mm(sv['oa_cat'], do_a, mode="tn", name=nm("mm_bo_a_dw"))
    doa = _heads(_mm(do_a, W['w_bo_a'], mode="nt", out_dtype=BF16, name=nm("mm_bo_a_dx")), MLA_HEADS)
    delta_a = _attn_delta(sv['o_mla'], doa, nm("mla_delta"))
    dq_m, dk_m, dv_m = _attn_bwd(sv['q_mla'], sv['k_mla'], sv['v_mla'], doa, sv['lse_a'].reshape(MLA_HEADS, S),
                                 delta_a, (MLA_NOPE + MLA_ROPE) ** -0.5, CHUNK, None, nm("mla_bwd"))
    dq_s = dq_m.transpose(1, 0, 2)
    dqr = jnp.concatenate([dq_s[:, :, MLA_NOPE:MLA_NOPE + ROPE_HALF].reshape(S, -1),
                           dq_s[:, :, MLA_NOPE + ROPE_HALF:].reshape(S, -1)], axis=1)

    def rope_q_bwd_fn(d1, d2, c, s):
        return d1 * c + d2 * s, d2 * c - d1 * s

    dq_r1, dq_r2 = _rowwise(nm("rope_q_bwd"), rope_q_bwd_fn, S,
                            [(dqr, LANES, 0, 0), (dqr, LANES, 1, 0), _whole(T['cos_q']), _whole(T['sin_q'])], [],
                            [(LANES, BF16), (LANES, BF16)], tr=512)
    dqa = jnp.concatenate([dq_s[:, :, :MLA_NOPE].reshape(S, -1).astype(BF16), dq_r1, dq_r2], axis=1)
    G['w_uq'] = _mm(sv['cqn'], dqa, mode="tn", name=nm("mm_uq_dw"))
    dcqn = _mm(dqa, W['w_uq'], mode="nt", name=nm("mm_uq_dx"))
    dcq, G['q_norm_g'] = _rms_bwd((proj, CQ_PAD, COL_CQ // CQ_PAD, 0), dcqn, W['q_norm_g'], Q_LORA, None,
                                  nm("rms_q_bwd"))
    dk_s = dk_m.transpose(1, 0, 2)
    dkv = jnp.concatenate([dk_s[:, :, :MLA_NOPE].reshape(S, -1), _unheads(dv_m)], axis=1).astype(BF16)
    G['w_ukv'] = _mm(sv['ckvn'], dkv, mode="tn", name=nm("mm_ukv_dw"))
    dckvn = _mm(dkv, W['w_ukv'], mode="nt", name=nm("mm_ukv_dx"))
    dckv, G['kv_norm_g'] = _rms_bwd((proj, KV_LORA, COL_CKV // KV_LORA, 0), dckvn, W['kv_norm_g'], KV_LORA, None,
                                    nm("rms_kv_bwd"))
    dk_rope_heads = dk_s[:, :, MLA_NOPE:].reshape(S, MLA_HEADS * MLA_ROPE)
    dkr = _rope_k(_whole(dk_rope_heads), T['cos_k'], T['sin_k_neg'], True, nm("rope_k_bwd"))

    dsmall = jnp.concatenate([dkr[:, :MLA_ROPE], dz.T, jnp.zeros((S, LANES - MLA_ROPE - FOX_HEADS), F32)], axis=1)
    dproj = jnp.concatenate([
        dlogit, dconv, _unheads(dq_f).astype(BF16), _unheads(dk_f).astype(BF16), _unheads(dv_f).astype(BF16),
        dcq.astype(BF16), dckv.astype(BF16), dsmall.astype(BF16),
        jnp.zeros((S, IN_COLS - COL_SMALL - LANES), BF16)], axis=1)
    G['w_in'] = _mm(sv['h1'], dproj, mode="tn", name=nm("mm_in_dw"))
    dh1 = _mm(dproj, W['w_in'], mode="nt", name=nm("mm_in_dx"))
    dx, G['norm_mix_g'] = _rms_bwd(_whole(sv['x']), dh1, W['norm_mix_g'], D_MODEL, dx2, nm("rms_mix_bwd"))
    return dx, G


def _loss_head(x, target, g, name):
    def fn(x, t, g):
        r = lax.rsqrt(jnp.mean(x * x, axis=1, keepdims=True) + RMS_EPS)
        xh = x * r
        e = xh * g - t
        part = 0.5 * jnp.sum(jnp.mean(e * e, axis=1, keepdims=True), axis=0, keepdims=True)
        dy = e * (1.0 / D_MODEL)
        dxh = dy * g
        dx = r * (dxh - xh * jnp.mean(dxh * xh, axis=1, keepdims=True))
        return dx, jnp.broadcast_to(part, (1, LANES)), _colsum(dy * xh)

    S = x.shape[0]
    dx, part, dg = _rowwise(name, fn, S, [_whole(x), _whole(target)], [g], [(D_MODEL, F32)], [(1, LANES), (1, D_MODEL)])
    return part[0, 0], dx, dg


def kernel(x, positions, norm_mix_g, w_in, b_gate, q_norm_g, w_uq, kv_norm_g, w_ukv, b_forget, dw_kernel, dw_bias, conv_ln_g, conv_ln_b, w_bo_a, w_bo_b, w_bo_c, w_out, norm_ffn_g, w_ffn_gate, w_ffn_up, w_ffn_down, final_norm_g, loss_target, m_norm_mix_g, m_w_in, m_b_gate, m_q_norm_g, m_w_uq, m_kv_norm_g, m_w_ukv, m_b_forget, m_dw_kernel, m_dw_bias, m_conv_ln_g, m_conv_ln_b, m_w_bo_a, m_w_bo_b, m_w_bo_c, m_w_out, m_norm_ffn_g, m_w_ffn_gate, m_w_ffn_up, m_w_ffn_down, m_final_norm_g, v_norm_mix_g, v_w_in, v_b_gate, v_q_norm_g, v_w_uq, v_kv_norm_g, v_w_ukv, v_b_forget, v_dw_kernel, v_dw_bias, v_conv_ln_g, v_conv_ln_b, v_w_bo_a, v_w_bo_b, v_w_bo_c, v_w_out, v_norm_ffn_g, v_w_ffn_gate, v_w_ffn_up, v_w_ffn_down, v_final_norm_g):
    local = dict(norm_mix_g=norm_mix_g, w_in=w_in, b_gate=b_gate, q_norm_g=q_norm_g, w_uq=w_uq, kv_norm_g=kv_norm_g,
                 w_ukv=w_ukv, b_forget=b_forget, dw_kernel=dw_kernel, dw_bias=dw_bias, conv_ln_g=conv_ln_g,
                 conv_ln_b=conv_ln_b, w_bo_a=w_bo_a, w_bo_b=w_bo_b, w_bo_c=w_bo_c, w_out=w_out, norm_ffn_g=norm_ffn_g,
                 w_ffn_gate=w_ffn_gate, w_ffn_up=w_ffn_up, w_ffn_down=w_ffn_down, final_norm_g=final_norm_g)
    mom_m = dict(norm_mix_g=m_norm_mix_g, w_in=m_w_in, b_gate=m_b_gate, q_norm_g=m_q_norm_g, w_uq=m_w_uq,
                 kv_norm_g=m_kv_norm_g, w_ukv=m_w_ukv, b_forget=m_b_forget, dw_kernel=m_dw_kernel, dw_bias=m_dw_bias,
                 conv_ln_g=m_conv_ln_g, conv_ln_b=m_conv_ln_b, w_bo_a=m_w_bo_a, w_bo_b=m_w_bo_b, w_bo_c=m_w_bo_c,
                 w_out=m_w_out, norm_ffn_g=m_norm_ffn_g, w_ffn_gate=m_w_ffn_gate, w_ffn_up=m_w_ffn_up,
                 w_ffn_down=m_w_ffn_down, final_norm_g=m_final_norm_g)
    mom_v = dict(norm_mix_g=v_norm_mix_g, w_in=v_w_in, b_gate=v_b_gate, q_norm_g=v_q_norm_g, w_uq=v_w_uq,
                 kv_norm_g=v_kv_norm_g, w_ukv=v_w_ukv, b_forget=v_b_forget, dw_kernel=v_dw_kernel, dw_bias=v_dw_bias,
                 conv_ln_g=v_conv_ln_g, conv_ln_b=v_conv_ln_b, w_bo_a=v_w_bo_a, w_bo_b=v_w_bo_b, w_bo_c=v_w_bo_c,
                 w_out=v_w_out, norm_ffn_g=v_norm_ffn_g, w_ffn_gate=v_w_ffn_gate, w_ffn_up=v_w_ffn_up,
                 w_ffn_down=v_w_ffn_down, final_norm_g=v_final_norm_g)
    S = x.shape[1]
    xs = x[0]
    sh_names = [n for n, _ in SHARDED]
    sh_shapes = [local[n].shape for n in sh_names]
    rep_shapes = [local[n].shape for n in REPLICATED]

    w_flat = _pack_flat([local[n] for n in sh_names], FLAT_ROW_MULTIPLE)
    gathered = _exchange(w_flat.astype(BF16), True, "gather_weights")
    full = _full_weights(gathered, sh_shapes)

    def layer_weights(l):
        W = {n: full[n][l] for n in ('w_bo_a', 'w_bo_b', 'w_bo_c', 'w_out', 'w_ffn_down')}
        W['w_in'] = _rearrange_w_in(full['w_in'][l])
        W['w_uq'] = _rearrange_w_uq(full['w_uq'][l])
        W['w_ukv'] = _rearrange_w_ukv(full['w_ukv'][l])
        W['w_gu'] = jnp.concatenate([full['w_ffn_gate'][l], full['w_ffn_up'][l]], axis=1)
        W['dw_kernel'] = jnp.pad(full['dw_kernel'][l].astype(F32), ((0, 32 - CONV_WIDTH), (0, 0)))
        for n in ('norm_mix_g', 'b_gate', 'kv_norm_g', 'dw_bias', 'conv_ln_g', 'conv_ln_b', 'norm_ffn_g'):
            W[n] = local[n][l][None, :]
        W['q_norm_g'] = jnp.pad(local['q_norm_g'][l], (0, CQ_PAD - Q_LORA))[None, :]
        W['b_forget'] = local['b_forget'][l][:, None]
        return W

    inv_freq = 1.0 / (ROPE_THETA ** (jnp.arange(0, MLA_ROPE, 2, dtype=F32) / MLA_ROPE))
    ang = positions[0].astype(F32)[:, None] * inv_freq
    cos, sin = jnp.cos(ang), jnp.sin(ang)
    zpad = jnp.zeros((S, LANES - MLA_ROPE), F32)
    T = dict(cos_q=jnp.tile(cos, (1, MLA_HEADS)), sin_q=jnp.tile(sin, (1, MLA_HEADS)),
             cos_k=jnp.concatenate([cos, cos, zpad], axis=1), sin_k=jnp.concatenate([-sin, sin, zpad], axis=1),
             sin_k_neg=jnp.concatenate([sin, -sin, zpad], axis=1))

    Ws, saved = [], []
    h = xs
    for l in range(DEPTH):
        W = layer_weights(l)
        h, sv = _layer_forward(h, W, T, l)
        Ws.append(W)
        saved.append(sv)
    loss_part, dh, dg_final = _loss_head(h, loss_target[0], local['final_norm_g'][None, :], "loss_head")
    loss = lax.psum(loss_part, ("x", "y", "c"))
    layer_grads = [None] * DEPTH
    for l in range(DEPTH - 1, -1, -1):
        dh, layer_grads[l] = _layer_backward(dh, saved[l], Ws[l], T, l)
    grad_x = dh[None]

    grads_full = {}
    grads_full['w_in'] = jnp.stack([_restore_w_in(g['w_in']) for g in layer_grads])
    grads_full['w_uq'] = jnp.stack([_restore_w_uq(g['w_uq']) for g in layer_grads])
    grads_full['w_ukv'] = jnp.stack([_restore_w_ukv(g['w_ukv']) for g in layer_grads])
    grads_full['dw_kernel'] = jnp.stack([g['dw_kernel'][:CONV_WIDTH] for g in layer_grads])
    for n in ('w_bo_a', 'w_bo_b', 'w_bo_c', 'w_out', 'w_ffn_down'):
        grads_full[n] = jnp.stack([g[n] for g in layer_grads])
    grads_full['w_ffn_gate'] = jnp.stack([g['w_gu'][:, :FFN_HIDDEN] for g in layer_grads])
    grads_full['w_ffn_up'] = jnp.stack([g['w_gu'][:, FFN_HIDDEN:] for g in layer_grads])
    packed = _pack_grads(grads_full, sh_shapes)
    received = _exchange(packed, False, "scatter_grads")
    R = w_flat.shape[0]
    m_flat = _pack_flat([mom_m[n] for n in sh_names], FLAT_ROW_MULTIPLE)
    v_flat = _pack_flat([mom_v[n] for n in sh_names], FLAT_ROW_MULTIPLE)
    g_sh, d_sh, nm_sh, nv_sh = _adamw_sum("adamw_sharded", received.reshape(N_DEV * R, LANES), w_flat, m_flat, v_flat,
                                          512)

    rep_grads = {
        'norm_mix_g': jnp.concatenate([g['norm_mix_g'] for g in layer_grads]),
        'b_gate': jnp.concatenate([g['b_gate'] for g in layer_grads]),
        'q_norm_g': jnp.concatenate([g['q_norm_g'][:, :Q_LORA] for g in layer_grads]),
        'kv_norm_g': jnp.concatenate([g['kv_norm_g'] for g in layer_grads]),
        'b_forget': jnp.concatenate([g['b_forget'].T for g in layer_grads]),
        'dw_bias': jnp.concatenate([g['dw_bias'] for g in layer_grads]),
        'conv_ln_g': jnp.concatenate([g['conv_ln_g'] for g in layer_grads]),
        'conv_ln_b': jnp.concatenate([g['conv_ln_b'] for g in layer_grads]),
        'norm_ffn_g': jnp.concatenate([g['norm_ffn_g'] for g in layer_grads]),
        'final_norm_g': dg_final[0],
    }
    REP_ROWS = 256
    rg_flat = _pack_flat([rep_grads[n] for n in REPLICATED], REP_ROWS)
    rg_all = _exchange(rg_flat, True, "gather_replicated_grads")
    Rr = rg_flat.shape[0]
    rw = _pack_flat([local[n] for n in REPLICATED], REP_ROWS)
    rm = _pack_flat([mom_m[n] for n in REPLICATED], REP_ROWS)
    rv = _pack_flat([mom_v[n] for n in REPLICATED], REP_ROWS)
    g_rp, d_rp, nm_rp, nv_rp = _adamw_sum("adamw_replicated", rg_all.reshape(N_DEV * Rr, LANES), rw, rm, rv, Rr)

    def by_name(flat_sh, flat_rp):
        d = dict(zip(sh_names, _unpack_flat(flat_sh, sh_shapes)))
        d.update(zip(REPLICATED, _unpack_flat(flat_rp, rep_shapes)))
        return [d[n] for n in WEIGHT_NAMES]

    return (loss, grad_x, *by_name(g_sh, g_rp), *by_name(d_sh, d_rp), *by_name(nm_sh, nm_rp), *by_name(nv_sh, nv_rp))
```

```python
import functools

import numpy as np
import jax
import jax.numpy as jnp
from jax import lax
from jax.experimental import pallas as pl
from jax.experimental.pallas import tpu as pltpu

F32 = jnp.float32
BF16 = jnp.bfloat16

D_MODEL = 1024
DEPTH = 4
CHUNK = 64
MLA_HEADS, MLA_NOPE, MLA_ROPE, MLA_V = 8, 64, 32, 64
Q_LORA, KV_LORA = 384, 256
ROPE_THETA = 10000.0
FOX_HEADS, FOX_HEAD_DIM = 8, 64
CONV_CHANNELS, CONV_WIDTH = 512, 31
FFN_HIDDEN = 2816
RMS_EPS = 1e-6
LN_EPS = 1e-5
ADAM_LR, ADAM_B1, ADAM_B2, ADAM_EPS, ADAM_WD, ADAM_STEP = 0.001, 0.9, 0.999, 1e-08, 0.01, 10

N_DEV = 8
LANES = 128
VMEM_LIMIT_BYTES = 56 * 1024 * 1024
NEG_BIG = -1e30
ROPE_HALF = MLA_ROPE // 2
CONV_HALO = 32

COL_GATE = 0
COL_CONV = 3072
COL_QB = 4096
COL_KB = 4608
COL_VB = 5120
COL_CQ = 5632
COL_CKV = 6144
COL_SMALL = 6400
IN_COLS = 6656
CQ_PAD = 512
O_CQ, O_CKV, O_KR, O_QB, O_KB, O_VB, O_F, O_CONV, O_GATE, O_END = 0, 384, 640, 672, 1184, 1696, 2208, 2216, 3240, 6312

WEIGHT_NAMES = ['norm_mix_g', 'w_in', 'b_gate', 'q_norm_g', 'w_uq', 'kv_norm_g', 'w_ukv', 'b_forget', 'dw_kernel',
                'dw_bias', 'conv_ln_g', 'conv_ln_b', 'w_bo_a', 'w_bo_b', 'w_bo_c', 'w_out', 'norm_ffn_g',
                'w_ffn_gate', 'w_ffn_up', 'w_ffn_down', 'final_norm_g']
SHARDED = [('w_in', 'col'), ('w_uq', 'col'), ('w_ukv', 'col'), ('dw_kernel', 'col'), ('w_bo_a', 'col'),
           ('w_bo_b', 'col'), ('w_bo_c', 'col'), ('w_out', 'row'), ('w_ffn_gate', 'col'), ('w_ffn_up', 'col'),
           ('w_ffn_down', 'row')]
REPLICATED = ['norm_mix_g', 'b_gate', 'q_norm_g', 'kv_norm_g', 'b_forget', 'dw_bias', 'conv_ln_g', 'conv_ln_b',
              'norm_ffn_g', 'final_norm_g']
FLAT_ROW_MULTIPLE = 512


def _pick(n, cands):
    for c in cands:
        if n % c == 0:
            return c
    raise ValueError(f"no tile for {n}")


def _params(sem):
    return pltpu.CompilerParams(dimension_semantics=sem, vmem_limit_bytes=VMEM_LIMIT_BYTES)


MM_ACC_BYTES = 8 * 1024 * 1024
MM_FULL_K = 2816
_LANE_TILES = (2048, 1664, 1536, 1408, 1024, 768, 512, 384, 256, 128)


def _mm_tiles(mode, M, N, K):
    if mode == "tn":
        tm = _pick(M, tuple(c for c in _LANE_TILES if c <= 1408))
        tn = _pick(N, tuple(c for c in _LANE_TILES if tm * c * 4 <= MM_ACC_BYTES))
        tk = _pick(K, (1024, 512, 256, 128))
    else:
        tm = _pick(M, (1024, 512, 256, 128))
        tn = _pick(N, (512, 384, 256, 128))
        tk = K if K <= MM_FULL_K else _pick(K, _LANE_TILES)
    return tm, tn, tk


def _mm(a, b, *, mode="nn", out_dtype=F32, add=None, name):
    if mode == "nn":
        (M, K), N = a.shape, b.shape[1]
    elif mode == "nt":
        (M, K), N = a.shape, b.shape[0]
    else:
        (K, M), N = a.shape, b.shape[1]
    tm, tn, tk = _mm_tiles(mode, M, N, K)
    nk = K // tk
    dims = {"nn": (((1,), (0,)), ((), ())), "nt": (((1,), (1,)), ((), ())), "tn": (((0,), (0,)), ((), ()))}[mode]
    has_add = add is not None

    def body(*refs):
        if has_add:
            a_ref, b_ref, add_ref, o_ref, acc_ref = refs
        else:
            a_ref, b_ref, o_ref, acc_ref = refs
        k = pl.program_id(2)

        @pl.when(k == 0)
        def _():
            acc_ref[...] = jnp.zeros_like(acc_ref)

        acc_ref[...] += lax.dot_general(a_ref[...].astype(BF16), b_ref[...].astype(BF16), dims,
                                        preferred_element_type=F32)

        @pl.when(k == nk - 1)
        def _():
            r = acc_ref[...]
            if has_add:
                r = r + add_ref[...]
            o_ref[...] = r.astype(o_ref.dtype)

    if mode == "nn":
        a_spec = pl.BlockSpec((tm, tk), lambda i, j, k: (i, k))
        b_spec = pl.BlockSpec((tk, tn), lambda i, j, k: (k, j))
    elif mode == "nt":
        a_spec = pl.BlockSpec((tm, tk), lambda i, j, k: (i, k))
        b_spec = pl.BlockSpec((tn, tk), lambda i, j, k: (j, k))
    else:
        a_spec = pl.BlockSpec((tk, tm), lambda i, j, k: (k, i))
        b_spec = pl.BlockSpec((tk, tn), lambda i, j, k: (k, j))
    o_spec = pl.BlockSpec((tm, tn), lambda i, j, k: (i, j))
    in_specs = [a_spec, b_spec] + ([o_spec] if has_add else [])
    args = (a, b) + ((add,) if has_add else ())
    return pl.pallas_call(
        body, name=name, grid=(M // tm, N // tn, nk), in_specs=in_specs, out_specs=o_spec,
        out_shape=jax.ShapeDtypeStruct((M, N), out_dtype),
        scratch_shapes=[pltpu.VMEM((tm, tn), F32)],
        compiler_params=_params(("parallel", "parallel", "arbitrary")),
    )(*args)


def _rowwise(name, fn, rows, row_ins, full_ins, outs, reds=(), tr=256):
    tr = min(tr, rows)
    assert rows % tr == 0
    n_r, n_f, n_o, n_d = len(row_ins), len(full_ins), len(outs), len(reds)

    def body(*refs):
        ins = [r[...] for r in refs[:n_r + n_f]]
        o_refs = refs[n_r + n_f:n_r + n_f + n_o]
        d_refs = refs[n_r + n_f + n_o:]
        res = fn(*ins)
        for o, v in zip(o_refs, res[:n_o]):
            o[...] = v.astype(o.dtype)
        if n_d:
            @pl.when(pl.program_id(0) == 0)
            def _():
                for d in d_refs:
                    d[...] = jnp.zeros_like(d)

            for d, v in zip(d_refs, res[n_o:]):
                d[...] += v

    in_specs = []
    for (arr, w, cidx, roff) in row_ins:
        in_specs.append(pl.BlockSpec((tr, w), functools.partial(lambda i, c, r: (i + r, c), c=cidx, r=roff)))
    for f in full_ins:
        in_specs.append(pl.BlockSpec(f.shape, lambda i: (0, 0)))
    out_specs = [pl.BlockSpec((tr, w), lambda i: (i, 0)) for (w, _) in outs]
    out_specs += [pl.BlockSpec((r, w), lambda i: (0, 0)) for (r, w) in reds]
    out_shape = [jax.ShapeDtypeStruct((rows, w), dt) for (w, dt) in outs]
    out_shape += [jax.ShapeDtypeStruct((r, w), F32) for (r, w) in reds]
    res = pl.pallas_call(
        body, name=name, grid=(rows // tr,), in_specs=in_specs, out_specs=out_specs, out_shape=out_shape,
        compiler_params=_params(("arbitrary",) if n_d else ("parallel",)),
    )(*[a for (a, _, _, _) in row_ins], *full_ins)
    return res


def _whole(arr, width=None, cidx=0, roff=0):
    return (arr, arr.shape[1] if width is None else width, cidx, roff)


def _colsum(v):
    return jnp.sum(v, axis=0, keepdims=True)


def _sigmoid(z):
    return 1.0 / (1.0 + jnp.exp(-z))


def _rms_fwd(x_in, g, n_true, name):
    def fn(x, g):
        x = x.astype(F32)
        r = lax.rsqrt(jnp.sum(x * x, axis=1, keepdims=True) * (1.0 / n_true) + RMS_EPS)
        return (x * r * g,)

    rows = x_in[0].shape[0]
    return _rowwise(name, fn, rows, [x_in], [g], [(x_in[1], BF16)])[0]


def _rms_bwd(x_in, dh, g, n_true, res, name):
    has_res = res is not None

    def fn(*a):
        if has_res:
            x, dh, rs, g = a
        else:
            x, dh, g = a
        x = x.astype(F32)
        dh = dh.astype(F32)
        r = lax.rsqrt(jnp.sum(x * x, axis=1, keepdims=True) * (1.0 / n_true) + RMS_EPS)
        xh = x * r
        dxh = dh * g
        dx = r * (dxh - xh * (jnp.sum(dxh * xh, axis=1, keepdims=True) * (1.0 / n_true)))
        if has_res:
            dx = dx + rs
            return dx, dx, _colsum(dh * xh)
        return dx, _colsum(dh * xh)

    rows, w = x_in[0].shape[0], x_in[1]
    ins = [x_in, _whole(dh)] + ([_whole(res)] if has_res else [])
    outs = [(w, F32), (w, BF16)] if has_res else [(w, F32)]
    return _rowwise(name, fn, rows, ins, [g], outs, [(1, w)])


ATT_SUB = 2


def _att_tile(S):
    return min(512, S // 2)


def _visible(q_idx, k_idx, group):
    if group == 1:
        return q_idx >= k_idx
    return (q_idx // group) >= (k_idx // group)


def _attn_fwd(q, k, v, group, cum, name):
    H, S, dk = q.shape
    dv = v.shape[-1]
    t = _att_tile(S)
    ts = t // ATT_SUB
    n = S // t
    bias = cum is not None

    def body(*refs):
        if bias:
            q_ref, k_ref, v_ref, ck_ref, o_ref, lse_ref = refs
        else:
            q_ref, k_ref, v_ref, o_ref, lse_ref = refs
        i = pl.program_id(1)
        qv = q_ref[0]

        def step(j, carry, masked):
            m, l, acc = carry
            start = pl.multiple_of(j * t, t)
            subs = []
            for h in range(ATT_SUB):
                kh = k_ref[0, pl.ds(pl.multiple_of(start + h * ts, ts), ts), :]
                s = lax.dot_general(qv, kh, (((1,), (1,)), ((), ())), preferred_element_type=F32)
                if bias:
                    s = s - ck_ref[0, pl.ds(j, 1), h * ts:(h + 1) * ts]
                if masked:
                    r = lax.broadcasted_iota(jnp.int32, (t, ts), 0)
                    c = lax.broadcasted_iota(jnp.int32, (t, ts), 1) + h * ts
                    s = jnp.where(_visible(r, c, group), s, NEG_BIG)
                subs.append(s)
            for h, s in enumerate(subs):
                m_new = jnp.maximum(m, jnp.max(s, axis=1, keepdims=True))
                p = jnp.exp(s - m_new)
                a = jnp.exp(m - m_new)
                l = a * l + jnp.sum(p, axis=1, keepdims=True)
                vh = v_ref[0, pl.ds(pl.multiple_of(start + h * ts, ts), ts), :]
                acc = a * acc + jnp.dot(p.astype(BF16), vh, preferred_element_type=F32)
                m = m_new
            return m, l, acc

        carry = (jnp.full((t, 1), NEG_BIG, F32), jnp.zeros((t, 1), F32), jnp.zeros((t, dv), F32))
        carry = lax.fori_loop(0, i, lambda j, cr: step(j, cr, False), carry)
        m, l, acc = step(i, carry, True)
        o_ref[0] = (acc / l).astype(o_ref.dtype)
        lse_ref[0] = m + jnp.log(l)

    in_specs = [pl.BlockSpec((1, t, dk), lambda h, i: (h, i, 0)),
                pl.BlockSpec((1, S, dk), lambda h, i: (h, 0, 0)),
                pl.BlockSpec((1, S, dv), lambda h, i: (h, 0, 0))]
    args = [q, k, v]
    if bias:
        in_specs.append(pl.BlockSpec((1, n, t), lambda h, i: (h, 0, 0)))
        args.append(cum.reshape(H, n, t))
    return pl.pallas_call(
        body, name=name, grid=(H, n), in_specs=in_specs,
        out_specs=[pl.BlockSpec((1, t, dv), lambda h, i: (h, i, 0)), pl.BlockSpec((1, t, 1), lambda h, i: (h, i, 0))],
        out_shape=[jax.ShapeDtypeStruct((H, S, dv), BF16), jax.ShapeDtypeStruct((H, S, 1), F32)],
        compiler_params=_params(("parallel", "arbitrary")),
    )(*args)


def _attn_bwd(q, k, v, do, lse, delta, scale, group, cum, name):
    H, S, dk = q.shape
    dv = v.shape[-1]
    t = _att_tile(S)
    ts = t // ATT_SUB
    n = S // t
    bias = cum is not None

    def body(*refs):
        if bias:
            q_ref, k_ref, v_ref, do_ref, lse_ref, dl_ref, ck_ref, dq_ref, dk_ref, dv_ref, dc_ref, dcq_ref = refs
        else:
            q_ref, k_ref, v_ref, do_ref, lse_ref, dl_ref, dq_ref, dk_ref, dv_ref = refs
        j = pl.program_id(1)

        @pl.when(j == 0)
        def _():
            dq_ref[...] = jnp.zeros_like(dq_ref)
            if bias:
                dcq_ref[...] = jnp.zeros_like(dcq_ref)

        kj = k_ref[0]
        vj = v_ref[0]

        def step(i, carry, masked):
            dk_acc, dv_acc, dc_acc = carry
            start = pl.multiple_of(i * t, t)
            subs = []
            for h in range(ATT_SUB):
                rs = pl.multiple_of(start + h * ts, ts)
                qi = q_ref[0, pl.ds(rs, ts), :]
                doi = do_ref[0, pl.ds(rs, ts), :]
                s_t = lax.dot_general(kj, qi, (((1,), (1,)), ((), ())), preferred_element_type=F32)
                if bias:
                    s_t = s_t - ck_ref[0]
                if masked:
                    kr = lax.broadcasted_iota(jnp.int32, (t, ts), 0)
                    qc = lax.broadcasted_iota(jnp.int32, (t, ts), 1) + h * ts
                    s_t = jnp.where(_visible(qc, kr, group), s_t, NEG_BIG)
                dp_t = lax.dot_general(vj, doi, (((1,), (1,)), ((), ())), preferred_element_type=F32)
                subs.append((rs, qi, doi, s_t, dp_t))
            for h, (rs, qi, doi, s_t, dp_t) in enumerate(subs):
                lanes = slice(h * ts, (h + 1) * ts)
                p_t = jnp.exp(s_t - lse_ref[0, pl.ds(i, 1), lanes])
                ds_t = p_t * (dp_t - dl_ref[0, pl.ds(i, 1), lanes])
                ds_b = ds_t.astype(BF16)
                dv_acc = dv_acc + jnp.dot(p_t.astype(BF16), doi, preferred_element_type=F32)
                dk_acc = dk_acc + jnp.dot(ds_b, qi, preferred_element_type=F32)
                if bias:
                    dc_acc = dc_acc - jnp.sum(ds_t, axis=1, keepdims=True)
                    dcq_ref[0, pl.ds(i, 1), lanes] += jnp.sum(ds_t, axis=0, keepdims=True)
                dq_ref[0, pl.ds(rs, ts), :] += lax.dot_general(
                    ds_b, kj, (((0,), (0,)), ((), ())), preferred_element_type=F32) * scale
            return dk_acc, dv_acc, dc_acc

        carry = (jnp.zeros((t, dk), F32), jnp.zeros((t, dv), F32), jnp.zeros((t, 1), F32))
        carry = step(j, carry, True)
        carry = lax.fori_loop(j + 1, n, lambda i, cr: step(i, cr, False), carry)
        dk_ref[0] = carry[0]
        dv_ref[0] = carry[1]
        if bias:
            dc_ref[0] = carry[2]

    in_specs = [pl.BlockSpec((1, S, dk), lambda h, j: (h, 0, 0)),
                pl.BlockSpec((1, t, dk), lambda h, j: (h, j, 0)),
                pl.BlockSpec((1, t, dv), lambda h, j: (h, j, 0)),
                pl.BlockSpec((1, S, dv), lambda h, j: (h, 0, 0)),
                pl.BlockSpec((1, n, t), lambda h, j: (h, 0, 0)),
                pl.BlockSpec((1, n, t), lambda h, j: (h, 0, 0))]
    args = [q, k, v, do, lse.reshape(H, n, t), delta.reshape(H, n, t)]
    out_specs = [pl.BlockSpec((1, S, dk), lambda h, j: (h, 0, 0)),
                 pl.BlockSpec((1, t, dk), lambda h, j: (h, j, 0)),
                 pl.BlockSpec((1, t, dv), lambda h, j: (h, j, 0))]
    out_shape = [jax.ShapeDtypeStruct((H, S, dk), F32), jax.ShapeDtypeStruct((H, S, dk), F32),
                 jax.ShapeDtypeStruct((H, S, dv), F32)]
    if bias:
        in_specs.append(pl.BlockSpec((1, t, 1), lambda h, j: (h, j, 0)))
        args.append(cum.reshape(H, S, 1))
        out_specs += [pl.BlockSpec((1, t, 1), lambda h, j: (h, j, 0)), pl.BlockSpec((1, n, t), lambda h, j: (h, 0, 0))]
        out_shape += [jax.ShapeDtypeStruct((H, S, 1), F32), jax.ShapeDtypeStruct((H, n, t), F32)]
    return pl.pallas_call(
        body, name=name, grid=(H, n), in_specs=in_specs, out_specs=out_specs, out_shape=out_shape,
        compiler_params=_params(("parallel", "arbitrary")),
    )(*args)


def _attn_delta(o, do, name):
    H, S, dv = o.shape

    def fn(o, do):
        return (jnp.sum(o.astype(F32) * do.astype(F32), axis=1, keepdims=True),)

    d = _rowwise(name, fn, H * S, [_whole(o.reshape(H * S, dv)), _whole(do.reshape(H * S, dv))], [], [(1, F32)],
                 tr=1024)[0]
    return d.reshape(H, S)


def _fox_gate_fwd(f_t, b, name):
    Hh, S = f_t.shape
    nb = S // LANES

    def body(f_ref, b_ref, cum_ref):
        r = lax.broadcasted_iota(jnp.int32, (LANES, LANES), 0)
        c = lax.broadcasted_iota(jnp.int32, (LANES, LANES), 1)
        upper = (r <= c).astype(F32)
        carry = jnp.zeros((Hh, 1), F32)
        for blk in range(nb):
            z = f_ref[:, blk * LANES:(blk + 1) * LANES] + b_ref[...]
            logf = jnp.minimum(z, 0.0) - jnp.log(1.0 + jnp.exp(-jnp.abs(z)))
            cs = jnp.dot(logf, upper, preferred_element_type=F32, precision=lax.Precision.HIGHEST) + carry
            cum_ref[:, blk * LANES:(blk + 1) * LANES] = cs
            carry = cs[:, LANES - 1:LANES]

    return pl.pallas_call(body, name=name, out_shape=jax.ShapeDtypeStruct((Hh, S), F32),
                          compiler_params=pltpu.CompilerParams(vmem_limit_bytes=VMEM_LIMIT_BYTES))(f_t, b)


def _fox_gate_bwd(f_t, b, dcum_k, dcum_q, name):
    Hh, S = f_t.shape
    nb = S // LANES

    def body(f_ref, b_ref, dck_ref, dcq_ref, dz_ref, db_ref):
        r = lax.broadcasted_iota(jnp.int32, (LANES, LANES), 0)
        c = lax.broadcasted_iota(jnp.int32, (LANES, LANES), 1)
        lower = (r >= c).astype(F32)
        carry = jnp.zeros((Hh, 1), F32)
        db = jnp.zeros((Hh, 1), F32)
        for blk in range(nb - 1, -1, -1):
            sl = slice(blk * LANES, (blk + 1) * LANES)
            rc = jnp.dot(dck_ref[:, sl] + dcq_ref[:, sl], lower, preferred_element_type=F32,
                         precision=lax.Precision.HIGHEST) + carry
            carry = rc[:, 0:1]
            z = f_ref[:, sl] + b_ref[...]
            dz = rc * (1.0 - _sigmoid(z))
            dz_ref[:, sl] = dz
            db = db + jnp.sum(dz, axis=1, keepdims=True)
        db_ref[...] = db

    return pl.pallas_call(
        body, name=name,
        out_shape=[jax.ShapeDtypeStruct((Hh, S), F32), jax.ShapeDtypeStruct((Hh, 1), F32)],
        compiler_params=pltpu.CompilerParams(vmem_limit_bytes=VMEM_LIMIT_BYTES))(f_t, b, dcum_k, dcum_q)


def _conv_tile(S):
    return min(512, S // 2)


def _glu(cin):
    a = cin[:, :CONV_CHANNELS].astype(F32)
    b = cin[:, CONV_CHANNELS:].astype(F32)
    return a * _sigmoid(b)


def _conv_taps(ext_ref, w_ref, ts, first):
    acc = jnp.zeros((ts, CONV_CHANNELS), F32)
    for j in range(CONV_WIDTH):
        acc = acc + w_ref[j:j + 1, :] * ext_ref[first + j:first + j + ts, :]
    return acc


def _fill_u0_ext(ext_ref, cin_ref, halo_ref, i):
    ext_ref[0:CONV_HALO, :] = jnp.where(i > 0, _glu(halo_ref[...]), 0.0)
    ext_ref[CONV_HALO:, :] = _glu(cin_ref[...])


def _conv_specs(ts, cidx):
    per = ts // CONV_HALO
    wide = 2 * CONV_CHANNELS
    return [pl.BlockSpec((ts, wide), lambda i: (i, cidx)),
            pl.BlockSpec((CONV_HALO, wide), lambda i: (jnp.maximum(i * per - 1, 0), cidx))]


def _conv_fwd(proj, w, bias, ln_g, ln_b, name):
    S = proj.shape[0]
    ts = _conv_tile(S)
    C = CONV_CHANNELS

    def body(cin_ref, halo_ref, w_ref, b_ref, g_ref, bb_ref, o_ref, ext_ref):
        _fill_u0_ext(ext_ref, cin_ref, halo_ref, pl.program_id(0))
        u1 = _conv_taps(ext_ref, w_ref, ts, CONV_HALO - (CONV_WIDTH - 1)) + b_ref[...]
        mu = jnp.mean(u1, axis=1, keepdims=True)
        xc = u1 - mu
        rstd = lax.rsqrt(jnp.mean(xc * xc, axis=1, keepdims=True) + LN_EPS)
        u2 = xc * rstd * g_ref[...] + bb_ref[...]
        o_ref[...] = (u2 * _sigmoid(u2)).astype(o_ref.dtype)

    vec = pl.BlockSpec((1, C), lambda i: (0, 0))
    return pl.pallas_call(
        body, name=name, grid=(S // ts,),
        in_specs=_conv_specs(ts, COL_CONV // (2 * C)) + [pl.BlockSpec((32, C), lambda i: (0, 0)), vec, vec, vec],
        out_specs=pl.BlockSpec((ts, C), lambda i: (i, 0)),
        out_shape=jax.ShapeDtypeStruct((S, C), BF16),
        scratch_shapes=[pltpu.VMEM((ts + CONV_HALO, C), F32)],
        compiler_params=_params(("parallel",)),
    )(proj, proj, w, bias, ln_g, ln_b)


def _conv_bwd_a(proj, du3, w, bias, ln_g, ln_b, name):
    S = proj.shape[0]
    ts = _conv_tile(S)
    C = CONV_CHANNELS
    first = CONV_HALO - (CONV_WIDTH - 1)

    def body(cin_ref, halo_ref, du3_ref, w_ref, b_ref, g_ref, bb_ref, du1_ref, dw_ref, dbias_ref, dg_ref, dbb_ref,
             ext_ref):
        i = pl.program_id(0)

        @pl.when(i == 0)
        def _():
            dw_ref[...] = jnp.zeros_like(dw_ref)
            dbias_ref[...] = jnp.zeros_like(dbias_ref)
            dg_ref[...] = jnp.zeros_like(dg_ref)
            dbb_ref[...] = jnp.zeros_like(dbb_ref)

        _fill_u0_ext(ext_ref, cin_ref, halo_ref, i)
        u1 = _conv_taps(ext_ref, w_ref, ts, first) + b_ref[...]
        mu = jnp.mean(u1, axis=1, keepdims=True)
        xc = u1 - mu
        rstd = lax.rsqrt(jnp.mean(xc * xc, axis=1, keepdims=True) + LN_EPS)
        xh = xc * rstd
        u2 = xh * g_ref[...] + bb_ref[...]
        sg = _sigmoid(u2)
        du2 = du3_ref[...].astype(F32) * (sg * (1.0 + u2 * (1.0 - sg)))
        dg_ref[...] += _colsum(du2 * xh)
        dbb_ref[...] += _colsum(du2)
        dxh = du2 * g_ref[...]
        du1 = rstd * (dxh - jnp.mean(dxh, axis=1, keepdims=True) - xh * jnp.mean(dxh * xh, axis=1, keepdims=True))
        du1_ref[...] = du1
        dbias_ref[...] += _colsum(du1)
        for j in range(CONV_WIDTH):
            dw_ref[j:j + 1, :] += _colsum(du1 * ext_ref[first + j:first + j + ts, :])

    vec = pl.BlockSpec((1, C), lambda i: (0, 0))
    taps = pl.BlockSpec((32, C), lambda i: (0, 0))
    return pl.pallas_call(
        body, name=name, grid=(S // ts,),
        in_specs=_conv_specs(ts, COL_CONV // (2 * C)) + [pl.BlockSpec((ts, C), lambda i: (i, 0)), taps, vec, vec, vec],
        out_specs=[pl.BlockSpec((ts, C), lambda i: (i, 0)), taps, vec, vec, vec],
        out_shape=[jax.ShapeDtypeStruct((S, C), F32), jax.ShapeDtypeStruct((32, C), F32)]
        + [jax.ShapeDtypeStruct((1, C), F32)] * 3,
        scratch_shapes=[pltpu.VMEM((ts + CONV_HALO, C), F32)],
        compiler_params=_params(("arbitrary",)),
    )(proj, proj, du3, w, bias, ln_g, ln_b)


def _conv_bwd_b(proj, du1, w, name):
    S = proj.shape[0]
    ts = _conv_tile(S)
    C = CONV_CHANNELS
    per = ts // CONV_HALO
    nblk = S // ts
    last_halo = S // CONV_HALO - 1

    def body(cin_ref, du1_ref, nxt_ref, w_ref, o_ref, ext_ref):
        i = pl.program_id(0)
        ext_ref[0:ts, :] = du1_ref[...]
        ext_ref[ts:, :] = jnp.where(i < nblk - 1, nxt_ref[...], 0.0)
        du0 = jnp.zeros((ts, C), F32)
        for j in range(CONV_WIDTH):
            off = CONV_WIDTH - 1 - j
            du0 = du0 + w_ref[j:j + 1, :] * ext_ref[off:off + ts, :]
        a = cin_ref[:, :C].astype(F32)
        sg = _sigmoid(cin_ref[:, C:].astype(F32))
        o_ref[:, :C] = (du0 * sg).astype(o_ref.dtype)
        o_ref[:, C:] = (du0 * a * sg * (1.0 - sg)).astype(o_ref.dtype)

    return pl.pallas_call(
        body, name=name, grid=(nblk,),
        in_specs=[pl.BlockSpec((ts, 2 * C), lambda i: (i, COL_CONV // (2 * C))),
                  pl.BlockSpec((ts, C), lambda i: (i, 0)),
                  pl.BlockSpec((CONV_HALO, C), lambda i: (jnp.minimum((i + 1) * per, last_halo), 0)),
                  pl.BlockSpec((32, C), lambda i: (0, 0))],
        out_specs=pl.BlockSpec((ts, 2 * C), lambda i: (i, 0)),
        out_shape=jax.ShapeDtypeStruct((S, 2 * C), BF16),
        scratch_shapes=[pltpu.VMEM((ts + CONV_HALO, C), F32)],
        compiler_params=_params(("parallel",)),
    )(proj, du1, du1, w)


def _mesh_pos():
    return lax.axis_index("x"), lax.axis_index("y"), lax.axis_index("c")


def _exchange(x, gather, name):
    R = x.shape[-2]

    def body(x_ref, out_ref, send_sems, recv_sems, local_sem):
        mx, my, mc = _mesh_pos()
        me = 4 * mx + 2 * my + mc

        def src(dst_dev):
            return x_ref if gather else x_ref.at[dst_dev]

        local = pltpu.make_async_copy(src(me), out_ref.at[me], local_sem)
        local.start()
        copies = []
        for k in range(1, N_DEV):
            px, py, pc = mx ^ (k >> 2), my ^ ((k >> 1) & 1), mc ^ (k & 1)
            peer = 4 * px + 2 * py + pc
            cp = pltpu.make_async_remote_copy(
                src_ref=src(peer), dst_ref=out_ref.at[me], send_sem=send_sems.at[k - 1], recv_sem=recv_sems.at[k - 1],
                device_id=(px, py, pc), device_id_type=pl.DeviceIdType.MESH)
            cp.start()
            copies.append(cp)
        for cp in copies:
            cp.wait_recv()
        for cp in copies:
            cp.wait_send()
        local.wait()

    return pl.pallas_call(
        body, name=name,
        in_specs=[pl.BlockSpec(memory_space=pl.ANY)], out_specs=pl.BlockSpec(memory_space=pl.ANY),
        out_shape=jax.ShapeDtypeStruct((N_DEV, R, LANES), x.dtype),
        scratch_shapes=[pltpu.SemaphoreType.DMA((N_DEV - 1,)), pltpu.SemaphoreType.DMA((N_DEV - 1,)),
                        pltpu.SemaphoreType.DMA(())],
    )(x)


SEGMENT_ROWS = 16


def _seg_rows(shape):
    n = int(np.prod(shape))
    return -(-n // (SEGMENT_ROWS * LANES)) * SEGMENT_ROWS


def _flat_total_rows(shapes, multiple):
    rows = sum(_seg_rows(s) for s in shapes)
    return -(-rows // multiple) * multiple


def _to_rows(a, lead=()):
    shape = a.shape[len(lead):]
    n, rows = int(np.prod(shape)), _seg_rows(shape)
    if n == rows * LANES:
        return a.reshape(lead + (rows, LANES))
    flat = a.reshape(lead + (n,))
    flat = jnp.pad(flat, [(0, 0)] * len(lead) + [(0, rows * LANES - n)])
    return flat.reshape(lead + (rows, LANES))


def _from_rows(seg, shape, lead=()):
    n, rows = int(np.prod(shape)), _seg_rows(shape)
    if n == rows * LANES:
        return seg.reshape(lead + tuple(shape))
    return seg.reshape(lead + (rows * LANES,))[..., :n].reshape(lead + tuple(shape))


def _pack_flat(arrs, multiple, lead=()):
    parts = [_to_rows(a, lead) for a in arrs]
    rows = sum(p.shape[-2] for p in parts)
    total = -(-rows // multiple) * multiple
    if total != rows:
        parts.append(jnp.zeros(lead + (total - rows, LANES), parts[0].dtype))
    return jnp.concatenate(parts, axis=len(lead))


def _unpack_flat(flat, shapes, lead=()):
    out, r0 = [], 0
    for s in shapes:
        rows = _seg_rows(s)
        out.append(_from_rows(flat[..., r0:r0 + rows, :], s, lead))
        r0 += rows
    return out


def _adamw_sum(name, g_slabs, w, m, v, tr):
    R = w.shape[0]
    c1 = 1.0 - ADAM_B1 ** ADAM_STEP
    c2 = 1.0 - ADAM_B2 ** ADAM_STEP

    def fn(*a):
        g = a[0].astype(F32)
        for d in range(1, N_DEV):
            g = g + a[d].astype(F32)
        w, m, v = a[N_DEV:]
        m_new = ADAM_B1 * m + (1.0 - ADAM_B1) * g
        v_new = ADAM_B2 * v + (1.0 - ADAM_B2) * (g * g)
        delta = -ADAM_LR * ((m_new / c1) / (jnp.sqrt(v_new / c2) + ADAM_EPS) + ADAM_WD * w)
        return g, delta, m_new, v_new

    tr = min(tr, R)
    ins = [(g_slabs, LANES, 0, d * (R // tr)) for d in range(N_DEV)] + [_whole(w), _whole(m), _whole(v)]
    return _rowwise(name, fn, R, ins, [], [(LANES, F32)] * 4, tr=tr)


def _full_weights(gathered, local_shapes):
    segs = _unpack_flat(gathered, local_shapes, lead=(N_DEV,))
    out = {}
    for (name, kind), shp, seg in zip(SHARDED, local_shapes, segs):
        L, a, b = shp
        if kind == 'col':
            out[name] = seg.transpose(1, 2, 0, 3).reshape(L, a, N_DEV * b)
        else:
            out[name] = seg.transpose(1, 0, 2, 3).reshape(L, N_DEV * a, b)
    return out


def _pack_grads(grads, local_shapes):
    parts = []
    for (name, kind), shp in zip(SHARDED, local_shapes):
        L, a, b = shp
        g = grads[name].astype(BF16)
        if kind == 'col':
            parts.append(g.reshape(L, a, N_DEV, b).transpose(2, 0, 1, 3))
        else:
            parts.append(g.reshape(L, N_DEV, a, b).transpose(1, 0, 2, 3))
    return _pack_flat(parts, FLAT_ROW_MULTIPLE, lead=(N_DEV,))


def _rearrange_w_in(w):
    z = lambda n: jnp.zeros((w.shape[0], n), w.dtype)
    return jnp.concatenate([
        w[:, O_GATE:O_END], w[:, O_CONV:O_GATE], w[:, O_QB:O_KB], w[:, O_KB:O_VB], w[:, O_VB:O_F],
        w[:, O_CQ:O_CKV], z(CQ_PAD - Q_LORA), w[:, O_CKV:O_KR], w[:, O_KR:O_QB], w[:, O_F:O_CONV],
        z(LANES - MLA_ROPE - FOX_HEADS), z(IN_COLS - COL_SMALL - LANES)], axis=1)


def _restore_w_in(g):
    return jnp.concatenate([
        g[:, COL_CQ:COL_CQ + Q_LORA], g[:, COL_CKV:COL_CKV + KV_LORA], g[:, COL_SMALL:COL_SMALL + MLA_ROPE],
        g[:, COL_QB:COL_KB], g[:, COL_KB:COL_VB], g[:, COL_VB:COL_CQ],
        g[:, COL_SMALL + MLA_ROPE:COL_SMALL + MLA_ROPE + FOX_HEADS], g[:, COL_CONV:COL_QB], g[:, COL_GATE:COL_CONV]],
        axis=1)


def _rearrange_w_uq(w):
    w3 = w.reshape(Q_LORA, MLA_HEADS, MLA_NOPE + MLA_ROPE)
    cols = jnp.concatenate([w3[:, :, :MLA_NOPE].reshape(Q_LORA, -1),
                            w3[:, :, MLA_NOPE:MLA_NOPE + ROPE_HALF].reshape(Q_LORA, -1),
                            w3[:, :, MLA_NOPE + ROPE_HALF:].reshape(Q_LORA, -1)], axis=1)
    return jnp.pad(cols, ((0, CQ_PAD - Q_LORA), (0, 0)))


def _restore_w_uq(g):
    g = g[:Q_LORA]
    n = MLA_HEADS * MLA_NOPE
    h = MLA_HEADS * ROPE_HALF
    parts = [g[:, :n].reshape(Q_LORA, MLA_HEADS, MLA_NOPE), g[:, n:n + h].reshape(Q_LORA, MLA_HEADS, ROPE_HALF),
             g[:, n + h:].reshape(Q_LORA, MLA_HEADS, ROPE_HALF)]
    return jnp.concatenate(parts, axis=2).reshape(Q_LORA, -1)


def _rearrange_w_ukv(w):
    w3 = w.reshape(KV_LORA, MLA_HEADS, MLA_NOPE + MLA_V)
    return jnp.concatenate([w3[:, :, :MLA_NOPE].reshape(KV_LORA, -1), w3[:, :, MLA_NOPE:].reshape(KV_LORA, -1)], axis=1)


def _restore_w_ukv(g):
    n = MLA_HEADS * MLA_NOPE
    parts = [g[:, :n].reshape(KV_LORA, MLA_HEADS, MLA_NOPE), g[:, n:].reshape(KV_LORA, MLA_HEADS, MLA_V)]
    return jnp.concatenate(parts, axis=2).reshape(KV_LORA, -1)


def _heads(a, H):
    S = a.shape[0]
    return a.reshape(S, H, -1).transpose(1, 0, 2)


def _unheads(a):
    H, S, d = a.shape
    return a.transpose(1, 0, 2).reshape(S, H * d)


def _rope_q(x_src, cos, sin, name):
    def fn(x1, x2, c, s):
        return x1 * c - x2 * s, x2 * c + x1 * s

    S = x_src.shape[0]
    return _rowwise(name, fn, S, [(x_src, LANES, 4, 0), (x_src, LANES, 5, 0), _whole(cos), _whole(sin)], [],
                    [(LANES, F32), (LANES, F32)], tr=512)


def _rope_k(x_in, cos_k, sin_k, fold_heads, name):
    def fn(x, c, s):
        if fold_heads:
            x = x[:, :LANES] + x[:, LANES:]
            x = x + pltpu.roll(x, 64, 1)
            x = x + pltpu.roll(x, 32, 1)
        lane = lax.broadcasted_iota(jnp.int32, x.shape, 1)
        partner = jnp.where(lane < ROPE_HALF, pltpu.roll(x, LANES - ROPE_HALF, 1), pltpu.roll(x, ROPE_HALF, 1))
        return (x * c + partner * s,)

    S = x_in[0].shape[0]
    return _rowwise(name, fn, S, [x_in, _whole(cos_k), _whole(sin_k)], [], [(LANES, F32)], tr=512)[0]


def _layer_forward(x, W, T, l):
    S = x.shape[0]
    nm = lambda s: f"{s}_l{l}"
    h1 = _rms_fwd(_whole(x), W['norm_mix_g'], D_MODEL, nm("rms_mix"))
    proj = _mm(h1, W['w_in'], name=nm("mm_in"))
    small = proj[:, COL_SMALL:COL_SMALL + LANES]

    cqn = _rms_fwd((proj, CQ_PAD, COL_CQ // CQ_PAD, 0), W['q_norm_g'], Q_LORA, nm("rms_q"))
    ckvn = _rms_fwd((proj, KV_LORA, COL_CKV // KV_LORA, 0), W['kv_norm_g'], KV_LORA, nm("rms_kv"))
    qa = _mm(cqn, W['w_uq'], name=nm("mm_uq"))
    kv = _mm(ckvn, W['w_ukv'], out_dtype=BF16, name=nm("mm_ukv"))
    q_r1, q_r2 = _rope_q(qa, T['cos_q'], T['sin_q'], nm("rope_q"))
    k_rope = _rope_k(_whole(small), T['cos_k'], T['sin_k'], False, nm("rope_k"))[:, :MLA_ROPE]
    n_nope = MLA_HEADS * MLA_NOPE
    mla_scale = (MLA_NOPE + MLA_ROPE) ** -0.5
    q_mla = (jnp.concatenate([qa[:, :n_nope].reshape(S, MLA_HEADS, MLA_NOPE), q_r1.reshape(S, MLA_HEADS, ROPE_HALF),
                              q_r2.reshape(S, MLA_HEADS, ROPE_HALF)], axis=2) * mla_scale).astype(BF16).transpose(1, 0, 2)
    k_mla = jnp.concatenate([kv[:, :n_nope].reshape(S, MLA_HEADS, MLA_NOPE),
                             jnp.broadcast_to(k_rope.astype(BF16)[:, None, :], (S, MLA_HEADS, MLA_ROPE))],
                            axis=2).transpose(1, 0, 2)
    v_mla = _heads(kv[:, n_nope:], MLA_HEADS)
    o_mla, lse_a = _attn_fwd(q_mla, k_mla, v_mla, CHUNK, None, nm("mla_fwd"))
    oa_cat = _unheads(o_mla)
    o_a = _mm(oa_cat, W['w_bo_a'], name=nm("mm_bo_a"))

    f_t = small[:, MLA_ROPE:MLA_ROPE + FOX_HEADS].T
    cum = _fox_gate_fwd(f_t, W['b_forget'], nm("fox_gate"))
    fox_scale = FOX_HEAD_DIM ** -0.5
    q_fox = _heads((proj[:, COL_QB:COL_KB] * fox_scale).astype(BF16), FOX_HEADS)
    k_fox = _heads(proj[:, COL_KB:COL_VB].astype(BF16), FOX_HEADS)
    v_fox = _heads(proj[:, COL_VB:COL_CQ].astype(BF16), FOX_HEADS)
    o_fox, lse_b = _attn_fwd(q_fox, k_fox, v_fox, 1, cum, nm("fox_fwd"))
    ob_cat = _unheads(o_fox)
    o_b = _mm(ob_cat, W['w_bo_b'], name=nm("mm_bo_b"))

    u3 = _conv_fwd(proj, W['dw_kernel'], W['dw_bias'], W['conv_ln_g'], W['conv_ln_b'], nm("conv_fwd"))
    o_c = _mm(u3, W['w_bo_c'], name=nm("mm_bo_c"))

    def gate_fn(la, lb, lc, oa, ob, oc, bg):
        ga = _sigmoid(la + bg[:, :D_MODEL])
        gb = _sigmoid(lb + bg[:, D_MODEL:2 * D_MODEL])
        gc = _sigmoid(lc + bg[:, 2 * D_MODEL:])
        return (ga * oa + gb * ob + gc * oc,)

    logit_ins = [(proj, D_MODEL, COL_GATE // D_MODEL + b, 0) for b in range(3)]
    y = _rowwise(nm("gate_fwd"), gate_fn, S, logit_ins + [_whole(o_a), _whole(o_b), _whole(o_c)], [W['b_gate']],
                 [(D_MODEL, BF16)])[0]
    x2 = _mm(y, W['w_out'], add=x, name=nm("mm_out"))

    h2 = _rms_fwd(_whole(x2), W['norm_ffn_g'], D_MODEL, nm("rms_ffn"))
    gu = _mm(h2, W['w_gu'], name=nm("mm_gu"))

    def swiglu_fn(gt, up):
        return (gt * _sigmoid(gt) * up,)

    ff = _rowwise(nm("swiglu_fwd"), swiglu_fn, S, [(gu, FFN_HIDDEN, 0, 0), (gu, FFN_HIDDEN, 1, 0)], [],
                  [(FFN_HIDDEN, BF16)])[0]
    x3 = _mm(ff, W['w_ffn_down'], add=x2, name=nm("mm_down"))

    saved = dict(x=x, h1=h1, proj=proj, small=small, cqn=cqn, ckvn=ckvn, q_mla=q_mla, k_mla=k_mla, v_mla=v_mla,
                 o_mla=o_mla, lse_a=lse_a, oa_cat=oa_cat, o_a=o_a, f_t=f_t, cum=cum, q_fox=q_fox, k_fox=k_fox,
                 v_fox=v_fox, o_fox=o_fox, lse_b=lse_b, ob_cat=ob_cat, o_b=o_b, u3=u3, o_c=o_c, y=y, x2=x2, h2=h2,
                 gu=gu, ff=ff)
    return x3, saved


def _layer_backward(dx3, dx3_b, sv, W, T, l):
    S = dx3.shape[0]
    nm = lambda s: f"{s}_l{l}"
    G = {}

    G['w_ffn_down'] = _mm(sv['ff'], dx3_b, mode="tn", name=nm("mm_down_dw"))
    dff = _mm(dx3_b, W['w_ffn_down'], mode="nt", name=nm("mm_down_dx"))

    def swiglu_bwd_fn(gt, up, d):
        sg = _sigmoid(gt)
        return (jnp.concatenate([d * up * (sg * (1.0 + gt * (1.0 - sg))), d * (gt * sg)], axis=1),)

    dgu = _rowwise(nm("swiglu_bwd"), swiglu_bwd_fn, S,
                   [(sv['gu'], FFN_HIDDEN, 0, 0), (sv['gu'], FFN_HIDDEN, 1, 0), _whole(dff)], [],
                   [(2 * FFN_HIDDEN, BF16)])[0]
    G['w_gu'] = _mm(sv['h2'], dgu, mode="tn", name=nm("mm_gu_dw"))
    dh2 = _mm(dgu, W['w_gu'], mode="nt", name=nm("mm_gu_dx"))
    dx2, dx2_b, G['norm_ffn_g'] = _rms_bwd(_whole(sv['x2']), dh2, W['norm_ffn_g'], D_MODEL, dx3, nm("rms_ffn_bwd"))

    G['w_out'] = _mm(sv['y'], dx2_b, mode="tn", name=nm("mm_out_dw"))
    dy = _mm(dx2_b, W['w_out'], mode="nt", name=nm("mm_out_dx"))

    def gate_bwd_fn(la, lb, lc, oa, ob, oc, dy, bg):
        outs, dls = [], []
        for k, (lg, o) in enumerate(((la, oa), (lb, ob), (lc, oc))):
            g = _sigmoid(lg + bg[:, k * D_MODEL:(k + 1) * D_MODEL])
            outs.append(dy * g)
            dls.append(dy * o * g * (1.0 - g))
        dl = jnp.concatenate(dls, axis=1)
        return (*outs, dl, _colsum(dl))

    proj = sv['proj']
    logit_ins = [(proj, D_MODEL, COL_GATE // D_MODEL + b, 0) for b in range(3)]
    do_a, do_b, do_c, dlogit, G['b_gate'] = _rowwise(
        nm("gate_bwd"), gate_bwd_fn, S, logit_ins + [_whole(sv['o_a']), _whole(sv['o_b']), _whole(sv['o_c']), _whole(dy)],
        [W['b_gate']], [(D_MODEL, BF16)] * 3 + [(3 * D_MODEL, BF16)], [(1, 3 * D_MODEL)], tr=128)

    G['w_bo_c'] = _mm(sv['u3'], do_c, mode="tn", name=nm("mm_bo_c_dw"))
    du3 = _mm(do_c, W['w_bo_c'], mode="nt", name=nm("mm_bo_c_dx"))
    du1, G['dw_kernel'], G['dw_bias'], G['conv_ln_g'], G['conv_ln_b'] = _conv_bwd_a(
        proj, du3, W['dw_kernel'], W['dw_bias'], W['conv_ln_g'], W['conv_ln_b'], nm("conv_bwd_a"))
    dconv = _conv_bwd_b(proj, du1, W['dw_kernel'], nm("conv_bwd_b"))

    G['w_bo_b'] = _mm(sv['ob_cat'], do_b, mode="tn", name=nm("mm_bo_b_dw"))
    dob = _heads(_mm(do_b, W['w_bo_b'], mode="nt", out_dtype=BF16, name=nm("mm_bo_b_dx")), FOX_HEADS)
    delta_b = _attn_delta(sv['o_fox'], dob, nm("fox_delta"))
    dq_f, dk_f, dv_f, dcum_k, dcum_q = _attn_bwd(
        sv['q_fox'], sv['k_fox'], sv['v_fox'], dob, sv['lse_b'].reshape(FOX_HEADS, S), delta_b, FOX_HEAD_DIM ** -0.5, 1,
        sv['cum'], nm("fox_bwd"))
    dz, G['b_forget'] = _fox_gate_bwd(sv['f_t'], W['b_forget'], dcum_k.reshape(FOX_HEADS, S),
                                      dcum_q.reshape(FOX_HEADS, S), nm("fox_gate_bwd"))

    G['w_bo_a'] = _mm(sv['oa_cat'], do_a, mode="tn", name=nm("mm_bo_a_dw"))
    doa = _heads(_mm(do_a, W['w_bo_a'], mode="nt", out_dtype=BF16, name=nm("mm_bo_a_dx")), MLA_HEADS)
    delta_a = _attn_delta(sv['o_mla'], doa, nm("mla_delta"))
    dq_m, dk_m, dv_m = _attn_bwd(sv['q_mla'], sv['k_mla'], sv['v_mla'], doa, sv['lse_a'].reshape(MLA_HEADS, S),
                                 delta_a, (MLA_NOPE + MLA_ROPE) ** -0.5, CHUNK, None, nm("mla_bwd"))
    dq_s = dq_m.transpose(1, 0, 2)
    dqr = jnp.concatenate([dq_s[:, :, MLA_NOPE:MLA_NOPE + ROPE_HALF].reshape(S, -1),
                           dq_s[:, :, MLA_NOPE + ROPE_HALF:].reshape(S, -1)], axis=1)

    def rope_q_bwd_fn(d1, d2, c, s):
        return d1 * c + d2 * s, d2 * c - d1 * s

    dq_r1, dq_r2 = _rowwise(nm("rope_q_bwd"), rope_q_bwd_fn, S,
                            [(dqr, LANES, 0, 0), (dqr, LANES, 1, 0), _whole(T['cos_q']), _whole(T['sin_q'])], [],
                            [(LANES, BF16), (LANES, BF16)], tr=512)
    dqa = jnp.concatenate([dq_s[:, :, :MLA_NOPE].reshape(S, -1).astype(BF16), dq_r1, dq_r2], axis=1)
    G['w_uq'] = _mm(sv['cqn'], dqa, mode="tn", name=nm("mm_uq_dw"))
    dcqn = _mm(dqa, W['w_uq'], mode="nt", name=nm("mm_uq_dx"))
    dcq, G['q_norm_g'] = _rms_bwd((proj, CQ_PAD, COL_CQ // CQ_PAD, 0), dcqn, W['q_norm_g'], Q_LORA, None,
                                  nm("rms_q_bwd"))
    dk_s = dk_m.transpose(1, 0, 2)
    dkv = jnp.concatenate([dk_s[:, :, :MLA_NOPE].reshape(S, -1), _unheads(dv_m)], axis=1).astype(BF16)
    G['w_ukv'] = _mm(sv['ckvn'], dkv, mode="tn", name=nm("mm_ukv_dw"))
    dckvn = _mm(dkv, W['w_ukv'], mode="nt", name=nm("mm_ukv_dx"))
    dckv, G['kv_norm_g'] = _rms_bwd((proj, KV_LORA, COL_CKV // KV_LORA, 0), dckvn, W['kv_norm_g'], KV_LORA, None,
                                    nm("rms_kv_bwd"))
    dk_rope_heads = dk_s[:, :, MLA_NOPE:].reshape(S, MLA_HEADS * MLA_ROPE)
    dkr = _rope_k(_whole(dk_rope_heads), T['cos_k'], T['sin_k_neg'], True, nm("rope_k_bwd"))

    dsmall = jnp.concatenate([dkr[:, :MLA_ROPE], dz.T, jnp.zeros((S, LANES - MLA_ROPE - FOX_HEADS), F32)], axis=1)
    dproj = jnp.concatenate([
        dlogit, dconv, _unheads(dq_f).astype(BF16), _unheads(dk_f).astype(BF16), _unheads(dv_f).astype(BF16),
        dcq.astype(BF16), dckv.astype(BF16), dsmall.astype(BF16),
        jnp.zeros((S, IN_COLS - COL_SMALL - LANES), BF16)], axis=1)
    G['w_in'] = _mm(sv['h1'], dproj, mode="tn", name=nm("mm_in_dw"))
    dh1 = _mm(dproj, W['w_in'], mode="nt", name=nm("mm_in_dx"))
    dx, dx_b, G['norm_mix_g'] = _rms_bwd(_whole(sv['x']), dh1, W['norm_mix_g'], D_MODEL, dx2, nm("rms_mix_bwd"))
    return dx, dx_b, G


def _loss_head(x, target, g, name):
    def fn(x, t, g):
        r = lax.rsqrt(jnp.mean(x * x, axis=1, keepdims=True) + RMS_EPS)
        xh = x * r
        e = xh * g - t
        part = 0.5 * jnp.sum(jnp.mean(e * e, axis=1, keepdims=True), axis=0, keepdims=True)
        dy = e * (1.0 / D_MODEL)
        dxh = dy * g
        dx = r * (dxh - xh * jnp.mean(dxh * xh, axis=1, keepdims=True))
        return dx, dx, jnp.broadcast_to(part, (1, LANES)), _colsum(dy * xh)

    S = x.shape[0]
    dx, dx_b, part, dg = _rowwise(name, fn, S, [_whole(x), _whole(target)], [g], [(D_MODEL, F32), (D_MODEL, BF16)],
                                  [(1, LANES), (1, D_MODEL)])
    return part[0, 0], dx, dx_b, dg


def kernel(x, positions, norm_mix_g, w_in, b_gate, q_norm_g, w_uq, kv_norm_g, w_ukv, b_forget, dw_kernel, dw_bias, conv_ln_g, conv_ln_b, w_bo_a, w_bo_b, w_bo_c, w_out, norm_ffn_g, w_ffn_gate, w_ffn_up, w_ffn_down, final_norm_g, loss_target, m_norm_mix_g, m_w_in, m_b_gate, m_q_norm_g, m_w_uq, m_kv_norm_g, m_w_ukv, m_b_forget, m_dw_kernel, m_dw_bias, m_conv_ln_g, m_conv_ln_b, m_w_bo_a, m_w_bo_b, m_w_bo_c, m_w_out, m_norm_ffn_g, m_w_ffn_gate, m_w_ffn_up, m_w_ffn_down, m_final_norm_g, v_norm_mix_g, v_w_in, v_b_gate, v_q_norm_g, v_w_uq, v_kv_norm_g, v_w_ukv, v_b_forget, v_dw_kernel, v_dw_bias, v_conv_ln_g, v_conv_ln_b, v_w_bo_a, v_w_bo_b, v_w_bo_c, v_w_out, v_norm_ffn_g, v_w_ffn_gate, v_w_ffn_up, v_w_ffn_down, v_final_norm_g):
    local = dict(norm_mix_g=norm_mix_g, w_in=w_in, b_gate=b_gate, q_norm_g=q_norm_g, w_uq=w_uq, kv_norm_g=kv_norm_g,
                 w_ukv=w_ukv, b_forget=b_forget, dw_kernel=dw_kernel, dw_bias=dw_bias, conv_ln_g=conv_ln_g,
                 conv_ln_b=conv_ln_b, w_bo_a=w_bo_a, w_bo_b=w_bo_b, w_bo_c=w_bo_c, w_out=w_out, norm_ffn_g=norm_ffn_g,
                 w_ffn_gate=w_ffn_gate, w_ffn_up=w_ffn_up, w_ffn_down=w_ffn_down, final_norm_g=final_norm_g)
    mom_m = dict(norm_mix_g=m_norm_mix_g, w_in=m_w_in, b_gate=m_b_gate, q_norm_g=m_q_norm_g, w_uq=m_w_uq,
                 kv_norm_g=m_kv_norm_g, w_ukv=m_w_ukv, b_forget=m_b_forget, dw_kernel=m_dw_kernel, dw_bias=m_dw_bias,
                 conv_ln_g=m_conv_ln_g, conv_ln_b=m_conv_ln_b, w_bo_a=m_w_bo_a, w_bo_b=m_w_bo_b, w_bo_c=m_w_bo_c,
                 w_out=m_w_out, norm_ffn_g=m_norm_ffn_g, w_ffn_gate=m_w_ffn_gate, w_ffn_up=m_w_ffn_up,
                 w_ffn_down=m_w_ffn_down, final_norm_g=m_final_norm_g)
    mom_v = dict(norm_mix_g=v_norm_mix_g, w_in=v_w_in, b_gate=v_b_gate, q_norm_g=v_q_norm_g, w_uq=v_w_uq,
                 kv_norm_g=v_kv_norm_g, w_ukv=v_w_ukv, b_forget=v_b_forget, dw_kernel=v_dw_kernel, dw_bias=v_dw_bias,
                 conv_ln_g=v_conv_ln_g, conv_ln_b=v_conv_ln_b, w_bo_a=v_w_bo_a, w_bo_b=v_w_bo_b, w_bo_c=v_w_bo_c,
                 w_out=v_w_out, norm_ffn_g=v_norm_ffn_g, w_ffn_gate=v_w_ffn_gate, w_ffn_up=v_w_ffn_up,
                 w_ffn_down=v_w_ffn_down, final_norm_g=v_final_norm_g)
    S = x.shape[1]
    xs = x[0]
    sh_names = [n for n, _ in SHARDED]
    sh_shapes = [local[n].shape for n in sh_names]
    rep_shapes = [local[n].shape for n in REPLICATED]

    w_flat = _pack_flat([local[n] for n in sh_names], FLAT_ROW_MULTIPLE)
    gathered = _exchange(w_flat.astype(BF16), True, "gather_weights")
    full = _full_weights(gathered, sh_shapes)

    def layer_weights(l):
        W = {n: full[n][l] for n in ('w_bo_a', 'w_bo_b', 'w_bo_c', 'w_out', 'w_ffn_down')}
        W['w_in'] = _rearrange_w_in(full['w_in'][l])
        W['w_uq'] = _rearrange_w_uq(full['w_uq'][l])
        W['w_ukv'] = _rearrange_w_ukv(full['w_ukv'][l])
        W['w_gu'] = jnp.concatenate([full['w_ffn_gate'][l], full['w_ffn_up'][l]], axis=1)
        W['dw_kernel'] = jnp.pad(full['dw_kernel'][l].astype(F32), ((0, 32 - CONV_WIDTH), (0, 0)))
        for n in ('norm_mix_g', 'b_gate', 'kv_norm_g', 'dw_bias', 'conv_ln_g', 'conv_ln_b', 'norm_ffn_g'):
            W[n] = local[n][l][None, :]
        W['q_norm_g'] = jnp.pad(local['q_norm_g'][l], (0, CQ_PAD - Q_LORA))[None, :]
        W['b_forget'] = local['b_forget'][l][:, None]
        return W

    inv_freq = 1.0 / (ROPE_THETA ** (jnp.arange(0, MLA_ROPE, 2, dtype=F32) / MLA_ROPE))
    ang = positions[0].astype(F32)[:, None] * inv_freq
    cos, sin = jnp.cos(ang), jnp.sin(ang)
    zpad = jnp.zeros((S, LANES - MLA_ROPE), F32)
    T = dict(cos_q=jnp.tile(cos, (1, MLA_HEADS)), sin_q=jnp.tile(sin, (1, MLA_HEADS)),
             cos_k=jnp.concatenate([cos, cos, zpad], axis=1), sin_k=jnp.concatenate([-sin, sin, zpad], axis=1),
             sin_k_neg=jnp.concatenate([sin, -sin, zpad], axis=1))

    Ws, saved = [], []
    h = xs
    for l in range(DEPTH):
        W = layer_weights(l)
        h, sv = _layer_forward(h, W, T, l)
        Ws.append(W)
        saved.append(sv)
    loss_part, dh, dh_b, dg_final = _loss_head(h, loss_target[0], local['final_norm_g'][None, :], "loss_head")
    loss = lax.psum(loss_part, ("x", "y", "c"))
    layer_grads = [None] * DEPTH
    for l in range(DEPTH - 1, -1, -1):
        dh, dh_b, layer_grads[l] = _layer_backward(dh, dh_b, saved[l], Ws[l], T, l)
    grad_x = dh[None]

    grads_full = {}
    grads_full['w_in'] = jnp.stack([_restore_w_in(g['w_in']) for g in layer_grads])
    grads_full['w_uq'] = jnp.stack([_restore_w_uq(g['w_uq']) for g in layer_grads])
    grads_full['w_ukv'] = jnp.stack([_restore_w_ukv(g['w_ukv']) for g in layer_grads])
    grads_full['dw_kernel'] = jnp.stack([g['dw_kernel'][:CONV_WIDTH] for g in layer_grads])
    for n in ('w_bo_a', 'w_bo_b', 'w_bo_c', 'w_out', 'w_ffn_down'):
        grads_full[n] = jnp.stack([g[n] for g in layer_grads])
    grads_full['w_ffn_gate'] = jnp.stack([g['w_gu'][:, :FFN_HIDDEN] for g in layer_grads])
    grads_full['w_ffn_up'] = jnp.stack([g['w_gu'][:, FFN_HIDDEN:] for g in layer_grads])
    packed = _pack_grads(grads_full, sh_shapes)
    received = _exchange(packed, False, "scatter_grads")
    R = w_flat.shape[0]
    m_flat = _pack_flat([mom_m[n] for n in sh_names], FLAT_ROW_MULTIPLE)
    v_flat = _pack_flat([mom_v[n] for n in sh_names], FLAT_ROW_MULTIPLE)
    g_sh, d_sh, nm_sh, nv_sh = _adamw_sum("adamw_sharded", received.reshape(N_DEV * R, LANES), w_flat, m_flat, v_flat,
                                          512)

    rep_grads = {
        'norm_mix_g': jnp.concatenate([g['norm_mix_g'] for g in layer_grads]),
        'b_gate': jnp.concatenate([g['b_gate'] for g in layer_grads]),
        'q_norm_g': jnp.concatenate([g['q_norm_g'][:, :Q_LORA] for g in layer_grads]),
        'kv_norm_g': jnp.concatenate([g['kv_norm_g'] for g in layer_grads]),
        'b_forget': jnp.concatenate([g['b_forget'].T for g in layer_grads]),
        'dw_bias': jnp.concatenate([g['dw_bias'] for g in layer_grads]),
        'conv_ln_g': jnp.concatenate([g['conv_ln_g'] for g in layer_grads]),
        'conv_ln_b': jnp.concatenate([g['conv_ln_b'] for g in layer_grads]),
        'norm_ffn_g': jnp.concatenate([g['norm_ffn_g'] for g in layer_grads]),
        'final_norm_g': dg_final[0],
    }
    REP_ROWS = 256
    rg_flat = _pack_flat([rep_grads[n] for n in REPLICATED], REP_ROWS)
    rg_all = _exchange(rg_flat, True, "gather_replicated_grads")
    Rr = rg_flat.shape[0]
    rw = _pack_flat([local[n] for n in REPLICATED], REP_ROWS)
    rm = _pack_flat([mom_m[n] for n in REPLICATED], REP_ROWS)
    rv = _pack_flat([mom_v[n] for n in REPLICATED], REP_ROWS)
    g_rp, d_rp, nm_rp, nv_rp = _adamw_sum("adamw_replicated", rg_all.reshape(N_DEV * Rr, LANES), rw, rm, rv, Rr)

    def by_name(flat_sh, flat_rp):
        d = dict(zip(sh_names, _unpack_flat(flat_sh, sh_shapes)))
        d.update(zip(REPLICATED, _unpack_flat(flat_rp, rep_shapes)))
        return [d[n] for n in WEIGHT_NAMES]

    return (loss, grad_x, *by_name(g_sh, g_rp), *by_name(d_sh, d_rp), *by_name(nm_sh, nm_rp), *by_name(nv_sh, nv_rp))
```

```python
import functools

import numpy as np
import jax
import jax.numpy as jnp
from jax import lax
from jax.experimental import pallas as pl
from jax.experimental.pallas import tpu as pltpu

F32 = jnp.float32
BF16 = jnp.bfloat16

D_MODEL = 1024
DEPTH = 4
CHUNK = 64
MLA_HEADS, MLA_NOPE, MLA_ROPE, MLA_V = 8, 64, 32, 64
Q_LORA, KV_LORA = 384, 256
ROPE_THETA = 10000.0
FOX_HEADS, FOX_HEAD_DIM = 8, 64
CONV_CHANNELS, CONV_WIDTH = 512, 31
FFN_HIDDEN = 2816
RMS_EPS = 1e-6
LN_EPS = 1e-5
ADAM_LR, ADAM_B1, ADAM_B2, ADAM_EPS, ADAM_WD, ADAM_STEP = 0.001, 0.9, 0.999, 1e-08, 0.01, 10

N_DEV = 8
LANES = 128
VMEM_LIMIT_BYTES = 56 * 1024 * 1024
NEG_BIG = -1e30
ROPE_HALF = MLA_ROPE // 2
CONV_HALO = 32

COL_GATE = 0
COL_CONV = 3072
COL_QB = 4096
COL_KB = 4608
COL_VB = 5120
COL_CQ = 5632
COL_CKV = 6144
COL_SMALL = 6400
IN_COLS = 6656
CQ_PAD = 512
O_CQ, O_CKV, O_KR, O_QB, O_KB, O_VB, O_F, O_CONV, O_GATE, O_END = 0, 384, 640, 672, 1184, 1696, 2208, 2216, 3240, 6312

WEIGHT_NAMES = ['norm_mix_g', 'w_in', 'b_gate', 'q_norm_g', 'w_uq', 'kv_norm_g', 'w_ukv', 'b_forget', 'dw_kernel',
                'dw_bias', 'conv_ln_g', 'conv_ln_b', 'w_bo_a', 'w_bo_b', 'w_bo_c', 'w_out', 'norm_ffn_g',
                'w_ffn_gate', 'w_ffn_up', 'w_ffn_down', 'final_norm_g']
SHARDED = [('w_in', 'col'), ('w_uq', 'col'), ('w_ukv', 'col'), ('dw_kernel', 'col'), ('w_bo_a', 'col'),
           ('w_bo_b', 'col'), ('w_bo_c', 'col'), ('w_out', 'row'), ('w_ffn_gate', 'col'), ('w_ffn_up', 'col'),
           ('w_ffn_down', 'row')]
REPLICATED = ['norm_mix_g', 'b_gate', 'q_norm_g', 'kv_norm_g', 'b_forget', 'dw_bias', 'conv_ln_g', 'conv_ln_b',
              'norm_ffn_g', 'final_norm_g']
FLAT_ROW_MULTIPLE = 512


def _pick(n, cands):
    for c in cands:
        if n % c == 0:
            return c
    raise ValueError(f"no tile for {n}")


def _params(sem):
    return pltpu.CompilerParams(dimension_semantics=sem, vmem_limit_bytes=VMEM_LIMIT_BYTES)


MM_ACC_BYTES = 8 * 1024 * 1024
MM_FULL_K = 2816
_LANE_TILES = (2048, 1664, 1536, 1408, 1024, 768, 512, 384, 256, 128)


def _mm_tiles(mode, M, N, K):
    if mode == "tn":
        tm = _pick(M, tuple(c for c in _LANE_TILES if c <= 1408))
        tn = _pick(N, tuple(c for c in _LANE_TILES if tm * c * 4 <= MM_ACC_BYTES))
        tk = _pick(K, (1024, 512, 256, 128))
    else:
        tm = _pick(M, (1024, 512, 256, 128))
        tn = _pick(N, (512, 384, 256, 128))
        tk = K if K <= MM_FULL_K else _pick(K, _LANE_TILES)
    return tm, tn, tk


def _mm(a, b, *, mode="nn", out_dtype=F32, add=None, name):
    if mode == "nn":
        (M, K), N = a.shape, b.shape[1]
    elif mode == "nt":
        (M, K), N = a.shape, b.shape[0]
    else:
        (K, M), N = a.shape, b.shape[1]
    tm, tn, tk = _mm_tiles(mode, M, N, K)
    nk = K // tk
    dims = {"nn": (((1,), (0,)), ((), ())), "nt": (((1,), (1,)), ((), ())), "tn": (((0,), (0,)), ((), ()))}[mode]
    has_add = add is not None

    def body(*refs):
        if has_add:
            a_ref, b_ref, add_ref, o_ref, acc_ref = refs
        else:
            a_ref, b_ref, o_ref, acc_ref = refs
        k = pl.program_id(2)

        @pl.when(k == 0)
        def _():
            acc_ref[...] = jnp.zeros_like(acc_ref)

        acc_ref[...] += lax.dot_general(a_ref[...].astype(BF16), b_ref[...].astype(BF16), dims,
                                        preferred_element_type=F32)

        @pl.when(k == nk - 1)
        def _():
            r = acc_ref[...]
            if has_add:
                r = r + add_ref[...]
            o_ref[...] = r.astype(o_ref.dtype)

    if mode == "nn":
        a_spec = pl.BlockSpec((tm, tk), lambda i, j, k: (i, k))
        b_spec = pl.BlockSpec((tk, tn), lambda i, j, k: (k, j))
    elif mode == "nt":
        a_spec = pl.BlockSpec((tm, tk), lambda i, j, k: (i, k))
        b_spec = pl.BlockSpec((tn, tk), lambda i, j, k: (j, k))
    else:
        a_spec = pl.BlockSpec((tk, tm), lambda i, j, k: (k, i))
        b_spec = pl.BlockSpec((tk, tn), lambda i, j, k: (k, j))
    o_spec = pl.BlockSpec((tm, tn), lambda i, j, k: (i, j))
    in_specs = [a_spec, b_spec] + ([o_spec] if has_add else [])
    args = (a, b) + ((add,) if has_add else ())
    return pl.pallas_call(
        body, name=name, grid=(M // tm, N // tn, nk), in_specs=in_specs, out_specs=o_spec,
        out_shape=jax.ShapeDtypeStruct((M, N), out_dtype),
        scratch_shapes=[pltpu.VMEM((tm, tn), F32)],
        compiler_params=_params(("parallel", "parallel", "arbitrary")),
    )(*args)


def _rowwise(name, fn, rows, row_ins, full_ins, outs, reds=(), tr=256):
    tr = min(tr, rows)
    assert rows % tr == 0
    n_r, n_f, n_o, n_d = len(row_ins), len(full_ins), len(outs), len(reds)

    def body(*refs):
        ins = [r[...] for r in refs[:n_r + n_f]]
        o_refs = refs[n_r + n_f:n_r + n_f + n_o]
        d_refs = refs[n_r + n_f + n_o:]
        res = fn(*ins)
        for o, v in zip(o_refs, res[:n_o]):
            o[...] = v.astype(o.dtype)
        if n_d:
            @pl.when(pl.program_id(0) == 0)
            def _():
                for d in d_refs:
                    d[...] = jnp.zeros_like(d)

            for d, v in zip(d_refs, res[n_o:]):
                d[...] += v

    in_specs = []
    for (arr, w, cidx, roff) in row_ins:
        in_specs.append(pl.BlockSpec((tr, w), functools.partial(lambda i, c, r: (i + r, c), c=cidx, r=roff)))
    for f in full_ins:
        in_specs.append(pl.BlockSpec(f.shape, lambda i: (0, 0)))
    out_specs = [pl.BlockSpec((tr, w), lambda i: (i, 0)) for (w, _) in outs]
    out_specs += [pl.BlockSpec((r, w), lambda i: (0, 0)) for (r, w) in reds]
    out_shape = [jax.ShapeDtypeStruct((rows, w), dt) for (w, dt) in outs]
    out_shape += [jax.ShapeDtypeStruct((r, w), F32) for (r, w) in reds]
    res = pl.pallas_call(
        body, name=name, grid=(rows // tr,), in_specs=in_specs, out_specs=out_specs, out_shape=out_shape,
        compiler_params=_params(("arbitrary",) if n_d else ("parallel",)),
    )(*[a for (a, _, _, _) in row_ins], *full_ins)
    return res


def _whole(arr, width=None, cidx=0, roff=0):
    return (arr, arr.shape[1] if width is None else width, cidx, roff)


def _colsum(v):
    return jnp.sum(v, axis=0, keepdims=True)


def _sigmoid(z):
    return 1.0 / (1.0 + jnp.exp(-z))


def _rms_fwd(x_in, g, n_true, name):
    def fn(x, g):
        x = x.astype(F32)
        r = lax.rsqrt(jnp.sum(x * x, axis=1, keepdims=True) * (1.0 / n_true) + RMS_EPS)
        return (x * r * g,)

    rows = x_in[0].shape[0]
    return _rowwise(name, fn, rows, [x_in], [g], [(x_in[1], BF16)])[0]


def _rms_bwd(x_in, dh, g, n_true, res, name):
    has_res = res is not None

    def fn(*a):
        if has_res:
            x, dh, rs, g = a
        else:
            x, dh, g = a
        x = x.astype(F32)
        dh = dh.astype(F32)
        r = lax.rsqrt(jnp.sum(x * x, axis=1, keepdims=True) * (1.0 / n_true) + RMS_EPS)
        xh = x * r
        dxh = dh * g
        dx = r * (dxh - xh * (jnp.sum(dxh * xh, axis=1, keepdims=True) * (1.0 / n_true)))
        if has_res:
            dx = dx + rs
            return dx, dx, _colsum(dh * xh)
        return dx, _colsum(dh * xh)

    rows, w = x_in[0].shape[0], x_in[1]
    ins = [x_in, _whole(dh)] + ([_whole(res)] if has_res else [])
    outs = [(w, F32), (w, BF16)] if has_res else [(w, F32)]
    return _rowwise(name, fn, rows, ins, [g], outs, [(1, w)])


ATT_SUB = 2
ATT_SUB_FWD = 2
ATT_HEADS_FWD = 2
ATT_HEADS_BWD = 2
LOG2E = 1.4426950408889634
LN2 = 0.6931471805599453


def _att_tile(S):
    return min(512, S // 2)


def _visible(q_idx, k_idx, group):
    if group == 1:
        return q_idx >= k_idx
    return (q_idx // group) >= (k_idx // group)


def _attn_fwd(q, k, v_t, group, cum2, name):
    H, S, dk = q.shape
    dv = v_t.shape[2]
    t = _att_tile(S)
    ts = t // ATT_SUB_FWD
    n = S // t
    HP = ATT_HEADS_FWD
    bias = cum2 is not None

    def body(*refs):
        if bias:
            q_ref, k_ref, vt_ref, ck_ref, o_ref, lse_ref = refs
        else:
            q_ref, k_ref, vt_ref, o_ref, lse_ref = refs
        i = pl.program_id(1)

        def step(j, carry, masked):
            start = pl.multiple_of(j * t, t)
            subs = []
            for g in range(HP):
                qv = q_ref[g]
                for h in range(ATT_SUB_FWD):
                    rs = pl.multiple_of(start + h * ts, ts)
                    kh = k_ref[g, pl.ds(rs, ts), :]
                    s = lax.dot_general(kh, qv, (((1,), (1,)), ((), ())), preferred_element_type=F32)
                    if bias:
                        s = s - ck_ref[g, pl.ds(rs, ts), :]
                    if masked:
                        kr = lax.broadcasted_iota(jnp.int32, (ts, t), 0) + h * ts
                        qc = lax.broadcasted_iota(jnp.int32, (ts, t), 1)
                        s = jnp.where(_visible(qc, kr, group), s, NEG_BIG)
                    subs.append(s)
            out = []
            for g in range(HP):
                m, l, acc = carry[g]
                for h in range(ATT_SUB_FWD):
                    s = subs[g * ATT_SUB_FWD + h]
                    m_new = jnp.maximum(m, jnp.max(s, axis=0, keepdims=True))
                    p = jnp.exp2(s - m_new)
                    a = jnp.exp2(m - m_new)
                    l = a * l + jnp.sum(p, axis=0, keepdims=True)
                    vh = vt_ref[g, j, :, h * ts:(h + 1) * ts]
                    acc = a * acc + jnp.dot(vh, p.astype(BF16), preferred_element_type=F32)
                    m = m_new
                out.append((m, l, acc))
            return tuple(out)

        init = (jnp.full((1, t), NEG_BIG, F32), jnp.zeros((1, t), F32), jnp.zeros((dv, t), F32))
        carry = lax.fori_loop(0, i, lambda j, cr: step(j, cr, False), (init,) * HP)
        carry = step(i, carry, True)
        for g in range(HP):
            m, l, acc = carry[g]
            o_ref[g] = (acc / l).astype(o_ref.dtype)
            lse_ref[g, 0] = m + jnp.log(l) * LOG2E

    in_specs = [pl.BlockSpec((HP, t, dk), lambda h, i: (h, i, 0)),
                pl.BlockSpec((HP, S, dk), lambda h, i: (h, 0, 0)),
                pl.BlockSpec((HP, n, dv, t), lambda h, i: (h, 0, 0, 0))]
    args = [q, k, v_t]
    if bias:
        in_specs.append(pl.BlockSpec((HP, S, 1), lambda h, i: (h, 0, 0)))
        args.append(cum2.reshape(H, S, 1))
    return pl.pallas_call(
        body, name=name, grid=(H // HP, n), in_specs=in_specs,
        out_specs=[pl.BlockSpec((HP, dv, t), lambda h, i: (h, 0, i)),
                   pl.BlockSpec((HP, 1, 1, t), lambda h, i: (h, i, 0, 0))],
        out_shape=[jax.ShapeDtypeStruct((H, dv, S), BF16), jax.ShapeDtypeStruct((H, n, 1, t), F32)],
        compiler_params=_params(("parallel", "arbitrary")),
    )(*args)


def _attn_bwd(q, k, v, do, lse, delta, scale, group, cum, name):
    H, S, dk = q.shape
    dv = v.shape[-1]
    t = _att_tile(S)
    ts = t // ATT_SUB
    n = S // t
    HP = ATT_HEADS_BWD
    bias = cum is not None

    def body(*refs):
        if bias:
            q_ref, k_ref, v_ref, do_ref, lse_ref, dl_ref, ck_ref, dq_ref, dk_ref, dv_ref, dc_ref, dcq_ref = refs
        else:
            q_ref, k_ref, v_ref, do_ref, lse_ref, dl_ref, dq_ref, dk_ref, dv_ref = refs
        j = pl.program_id(1)

        @pl.when(j == 0)
        def _():
            dq_ref[...] = jnp.zeros_like(dq_ref)
            if bias:
                dcq_ref[...] = jnp.zeros_like(dcq_ref)

        def step(i, carry, masked):
            start = pl.multiple_of(i * t, t)
            subs = []
            for g in range(HP):
                kj = k_ref[g]
                vj = v_ref[g]
                for h in range(ATT_SUB):
                    rs = pl.multiple_of(start + h * ts, ts)
                    qi = q_ref[g, pl.ds(rs, ts), :]
                    doi = do_ref[g, pl.ds(rs, ts), :]
                    s_t = lax.dot_general(kj, qi, (((1,), (1,)), ((), ())), preferred_element_type=F32)
                    if bias:
                        s_t = s_t - ck_ref[g]
                    if masked:
                        kr = lax.broadcasted_iota(jnp.int32, (t, ts), 0)
                        qc = lax.broadcasted_iota(jnp.int32, (t, ts), 1) + h * ts
                        s_t = jnp.where(_visible(qc, kr, group), s_t, NEG_BIG)
                    dp_t = lax.dot_general(vj, doi, (((1,), (1,)), ((), ())), preferred_element_type=F32)
                    subs.append((rs, qi, doi, s_t, dp_t))
            out = []
            for g in range(HP):
                dk_acc, dv_acc, dc_acc = carry[g]
                kj = k_ref[g]
                for h in range(ATT_SUB):
                    rs, qi, doi, s_t, dp_t = subs[g * ATT_SUB + h]
                    lanes = slice(h * ts, (h + 1) * ts)
                    p_t = jnp.exp2(s_t - lse_ref[g, pl.ds(i, 1), lanes])
                    ds_t = p_t * (dp_t - dl_ref[g, pl.ds(i, 1), lanes])
                    ds_b = ds_t.astype(BF16)
                    dv_acc = dv_acc + jnp.dot(p_t.astype(BF16), doi, preferred_element_type=F32)
                    dk_acc = dk_acc + jnp.dot(ds_b, qi, preferred_element_type=F32)
                    if bias:
                        dc_acc = dc_acc - jnp.sum(ds_t, axis=1, keepdims=True)
                        dcq_ref[g, pl.ds(i, 1), lanes] += jnp.sum(ds_t, axis=0, keepdims=True)
                    dq_ref[g, pl.ds(rs, ts), :] += lax.dot_general(
                        ds_b, kj, (((0,), (0,)), ((), ())), preferred_element_type=F32) * scale
                out.append((dk_acc, dv_acc, dc_acc))
            return tuple(out)

        init = (jnp.zeros((t, dk), F32), jnp.zeros((t, dv), F32), jnp.zeros((t, 1), F32))
        carry = step(j, (init,) * HP, True)
        carry = lax.fori_loop(j + 1, n, lambda i, cr: step(i, cr, False), carry)
        for g in range(HP):
            dk_ref[g] = carry[g][0] * LN2
            dv_ref[g] = carry[g][1]
            if bias:
                dc_ref[g] = carry[g][2]

    in_specs = [pl.BlockSpec((HP, S, dk), lambda h, j: (h, 0, 0)),
                pl.BlockSpec((HP, t, dk), lambda h, j: (h, j, 0)),
                pl.BlockSpec((HP, t, dv), lambda h, j: (h, j, 0)),
                pl.BlockSpec((HP, S, dv), lambda h, j: (h, 0, 0)),
                pl.BlockSpec((HP, n, t), lambda h, j: (h, 0, 0)),
                pl.BlockSpec((HP, n, t), lambda h, j: (h, 0, 0))]
    args = [q, k, v, do, lse.reshape(H, n, t), delta.reshape(H, n, t)]
    out_specs = [pl.BlockSpec((HP, S, dk), lambda h, j: (h, 0, 0)),
                 pl.BlockSpec((HP, t, dk), lambda h, j: (h, j, 0)),
                 pl.BlockSpec((HP, t, dv), lambda h, j: (h, j, 0))]
    out_shape = [jax.ShapeDtypeStruct((H, S, dk), F32), jax.ShapeDtypeStruct((H, S, dk), F32),
                 jax.ShapeDtypeStruct((H, S, dv), F32)]
    if bias:
        in_specs.append(pl.BlockSpec((HP, t, 1), lambda h, j: (h, j, 0)))
        args.append(cum.reshape(H, S, 1))
        out_specs += [pl.BlockSpec((HP, t, 1), lambda h, j: (h, j, 0)),
                      pl.BlockSpec((HP, n, t), lambda h, j: (h, 0, 0))]
        out_shape += [jax.ShapeDtypeStruct((H, S, 1), F32), jax.ShapeDtypeStruct((H, n, t), F32)]
    return pl.pallas_call(
        body, name=name, grid=(H // HP, n), in_specs=in_specs, out_specs=out_specs, out_shape=out_shape,
        compiler_params=_params(("parallel", "arbitrary")),
    )(*args)


def _attn_delta(o, do, name):
    H, S, dv = o.shape

    def fn(o, do):
        return (jnp.sum(o.astype(F32) * do.astype(F32), axis=1, keepdims=True),)

    d = _rowwise(name, fn, H * S, [_whole(o.reshape(H * S, dv)), _whole(do.reshape(H * S, dv))], [], [(1, F32)],
                 tr=1024)[0]
    return d.reshape(H, S)


def _fox_gate_fwd(f_t, b, name):
    Hh, S = f_t.shape
    nb = S // LANES

    def body(f_ref, b_ref, cum_ref):
        r = lax.broadcasted_iota(jnp.int32, (LANES, LANES), 0)
        c = lax.broadcasted_iota(jnp.int32, (LANES, LANES), 1)
        upper = (r <= c).astype(F32)
        carry = jnp.zeros((Hh, 1), F32)
        for blk in range(nb):
            z = f_ref[:, blk * LANES:(blk + 1) * LANES] + b_ref[...]
            logf = jnp.minimum(z, 0.0) - jnp.log(1.0 + jnp.exp(-jnp.abs(z)))
            cs = jnp.dot(logf, upper, preferred_element_type=F32, precision=lax.Precision.HIGHEST) + carry
            cum_ref[:, blk * LANES:(blk + 1) * LANES] = cs * LOG2E
            carry = cs[:, LANES - 1:LANES]

    return pl.pallas_call(body, name=name, out_shape=jax.ShapeDtypeStruct((Hh, S), F32),
                          compiler_params=pltpu.CompilerParams(vmem_limit_bytes=VMEM_LIMIT_BYTES))(f_t, b)


def _fox_gate_bwd(f_t, b, dcum_k, dcum_q, name):
    Hh, S = f_t.shape
    nb = S // LANES

    def body(f_ref, b_ref, dck_ref, dcq_ref, dz_ref, db_ref):
        r = lax.broadcasted_iota(jnp.int32, (LANES, LANES), 0)
        c = lax.broadcasted_iota(jnp.int32, (LANES, LANES), 1)
        lower = (r >= c).astype(F32)
        carry = jnp.zeros((Hh, 1), F32)
        db = jnp.zeros((Hh, 1), F32)
        for blk in range(nb - 1, -1, -1):
            sl = slice(blk * LANES, (blk + 1) * LANES)
            rc = jnp.dot(dck_ref[:, sl] + dcq_ref[:, sl], lower, preferred_element_type=F32,
                         precision=lax.Precision.HIGHEST) + carry
            carry = rc[:, 0:1]
            z = f_ref[:, sl] + b_ref[...]
            dz = rc * (1.0 - _sigmoid(z))
            dz_ref[:, sl] = dz
            db = db + jnp.sum(dz, axis=1, keepdims=True)
        db_ref[...] = db

    return pl.pallas_call(
        body, name=name,
        out_shape=[jax.ShapeDtypeStruct((Hh, S), F32), jax.ShapeDtypeStruct((Hh, 1), F32)],
        compiler_params=pltpu.CompilerParams(vmem_limit_bytes=VMEM_LIMIT_BYTES))(f_t, b, dcum_k, dcum_q)


def _conv_tile(S):
    return min(512, S // 2)


def _glu(cin):
    a = cin[:, :CONV_CHANNELS].astype(F32)
    b = cin[:, CONV_CHANNELS:].astype(F32)
    return a * _sigmoid(b)


def _conv_taps(ext_ref, w_ref, ts, first):
    acc = jnp.zeros((ts, CONV_CHANNELS), F32)
    for j in range(CONV_WIDTH):
        acc = acc + w_ref[j:j + 1, :] * ext_ref[first + j:first + j + ts, :]
    return acc


def _fill_u0_ext(ext_ref, cin_ref, halo_ref, i):
    ext_ref[0:CONV_HALO, :] = jnp.where(i > 0, _glu(halo_ref[...]), 0.0)
    ext_ref[CONV_HALO:, :] = _glu(cin_ref[...])


def _conv_specs(ts, cidx):
    per = ts // CONV_HALO
    wide = 2 * CONV_CHANNELS
    return [pl.BlockSpec((ts, wide), lambda i: (i, cidx)),
            pl.BlockSpec((CONV_HALO, wide), lambda i: (jnp.maximum(i * per - 1, 0), cidx))]


def _conv_fwd(proj, w, bias, ln_g, ln_b, name):
    S = proj.shape[0]
    ts = _conv_tile(S)
    C = CONV_CHANNELS

    def body(cin_ref, halo_ref, w_ref, b_ref, g_ref, bb_ref, o_ref, ext_ref):
        _fill_u0_ext(ext_ref, cin_ref, halo_ref, pl.program_id(0))
        u1 = _conv_taps(ext_ref, w_ref, ts, CONV_HALO - (CONV_WIDTH - 1)) + b_ref[...]
        mu = jnp.mean(u1, axis=1, keepdims=True)
        xc = u1 - mu
        rstd = lax.rsqrt(jnp.mean(xc * xc, axis=1, keepdims=True) + LN_EPS)
        u2 = xc * rstd * g_ref[...] + bb_ref[...]
        o_ref[...] = (u2 * _sigmoid(u2)).astype(o_ref.dtype)

    vec = pl.BlockSpec((1, C), lambda i: (0, 0))
    return pl.pallas_call(
        body, name=name, grid=(S // ts,),
        in_specs=_conv_specs(ts, COL_CONV // (2 * C)) + [pl.BlockSpec((32, C), lambda i: (0, 0)), vec, vec, vec],
        out_specs=pl.BlockSpec((ts, C), lambda i: (i, 0)),
        out_shape=jax.ShapeDtypeStruct((S, C), BF16),
        scratch_shapes=[pltpu.VMEM((ts + CONV_HALO, C), F32)],
        compiler_params=_params(("parallel",)),
    )(proj, proj, w, bias, ln_g, ln_b)


def _conv_bwd_a(proj, du3, w, bias, ln_g, ln_b, name):
    S = proj.shape[0]
    ts = _conv_tile(S)
    C = CONV_CHANNELS
    first = CONV_HALO - (CONV_WIDTH - 1)

    def body(cin_ref, halo_ref, du3_ref, w_ref, b_ref, g_ref, bb_ref, du1_ref, dw_ref, dbias_ref, dg_ref, dbb_ref,
             ext_ref):
        i = pl.program_id(0)

        @pl.when(i == 0)
        def _():
            dw_ref[...] = jnp.zeros_like(dw_ref)
            dbias_ref[...] = jnp.zeros_like(dbias_ref)
            dg_ref[...] = jnp.zeros_like(dg_ref)
            dbb_ref[...] = jnp.zeros_like(dbb_ref)

        _fill_u0_ext(ext_ref, cin_ref, halo_ref, i)
        u1 = _conv_taps(ext_ref, w_ref, ts, first) + b_ref[...]
        mu = jnp.mean(u1, axis=1, keepdims=True)
        xc = u1 - mu
        rstd = lax.rsqrt(jnp.mean(xc * xc, axis=1, keepdims=True) + LN_EPS)
        xh = xc * rstd
        u2 = xh * g_ref[...] + bb_ref[...]
        sg = _sigmoid(u2)
        du2 = du3_ref[...].astype(F32) * (sg * (1.0 + u2 * (1.0 - sg)))
        dg_ref[...] += _colsum(du2 * xh)
        dbb_ref[...] += _colsum(du2)
        dxh = du2 * g_ref[...]
        du1 = rstd * (dxh - jnp.mean(dxh, axis=1, keepdims=True) - xh * jnp.mean(dxh * xh, axis=1, keepdims=True))
        du1_ref[...] = du1
        dbias_ref[...] += _colsum(du1)
        for j in range(CONV_WIDTH):
            dw_ref[j:j + 1, :] += _colsum(du1 * ext_ref[first + j:first + j + ts, :])

    vec = pl.BlockSpec((1, C), lambda i: (0, 0))
    taps = pl.BlockSpec((32, C), lambda i: (0, 0))
    return pl.pallas_call(
        body, name=name, grid=(S // ts,),
        in_specs=_conv_specs(ts, COL_CONV // (2 * C)) + [pl.BlockSpec((ts, C), lambda i: (i, 0)), taps, vec, vec, vec],
        out_specs=[pl.BlockSpec((ts, C), lambda i: (i, 0)), taps, vec, vec, vec],
        out_shape=[jax.ShapeDtypeStruct((S, C), F32), jax.ShapeDtypeStruct((32, C), F32)]
        + [jax.ShapeDtypeStruct((1, C), F32)] * 3,
        scratch_shapes=[pltpu.VMEM((ts + CONV_HALO, C), F32)],
        compiler_params=_params(("arbitrary",)),
    )(proj, proj, du3, w, bias, ln_g, ln_b)


def _conv_bwd_b(proj, du1, w, name):
    S = proj.shape[0]
    ts = _conv_tile(S)
    C = CONV_CHANNELS
    per = ts // CONV_HALO
    nblk = S // ts
    last_halo = S // CONV_HALO - 1

    def body(cin_ref, du1_ref, nxt_ref, w_ref, o_ref, ext_ref):
        i = pl.program_id(0)
        ext_ref[0:ts, :] = du1_ref[...]
        ext_ref[ts:, :] = jnp.where(i < nblk - 1, nxt_ref[...], 0.0)
        du0 = jnp.zeros((ts, C), F32)
        for j in range(CONV_WIDTH):
            off = CONV_WIDTH - 1 - j
            du0 = du0 + w_ref[j:j + 1, :] * ext_ref[off:off + ts, :]
        a = cin_ref[:, :C].astype(F32)
        sg = _sigmoid(cin_ref[:, C:].astype(F32))
        o_ref[:, :C] = (du0 * sg).astype(o_ref.dtype)
        o_ref[:, C:] = (du0 * a * sg * (1.0 - sg)).astype(o_ref.dtype)

    return pl.pallas_call(
        body, name=name, grid=(nblk,),
        in_specs=[pl.BlockSpec((ts, 2 * C), lambda i: (i, COL_CONV // (2 * C))),
                  pl.BlockSpec((ts, C), lambda i: (i, 0)),
                  pl.BlockSpec((CONV_HALO, C), lambda i: (jnp.minimum((i + 1) * per, last_halo), 0)),
                  pl.BlockSpec((32, C), lambda i: (0, 0))],
        out_specs=pl.BlockSpec((ts, 2 * C), lambda i: (i, 0)),
        out_shape=jax.ShapeDtypeStruct((S, 2 * C), BF16),
        scratch_shapes=[pltpu.VMEM((ts + CONV_HALO, C), F32)],
        compiler_params=_params(("parallel",)),
    )(proj, du1, du1, w)


def _mesh_pos():
    return lax.axis_index("x"), lax.axis_index("y"), lax.axis_index("c")


def _exchange(x, gather, name):
    R = x.shape[-2]

    def body(x_ref, out_ref, send_sems, recv_sems, local_sem):
        mx, my, mc = _mesh_pos()
        me = 4 * mx + 2 * my + mc

        def src(dst_dev):
            return x_ref if gather else x_ref.at[dst_dev]

        local = pltpu.make_async_copy(src(me), out_ref.at[me], local_sem)
        local.start()
        copies = []
        for k in range(1, N_DEV):
            px, py, pc = mx ^ (k >> 2), my ^ ((k >> 1) & 1), mc ^ (k & 1)
            peer = 4 * px + 2 * py + pc
            cp = pltpu.make_async_remote_copy(
                src_ref=src(peer), dst_ref=out_ref.at[me], send_sem=send_sems.at[k - 1], recv_sem=recv_sems.at[k - 1],
                device_id=(px, py, pc), device_id_type=pl.DeviceIdType.MESH)
            cp.start()
            copies.append(cp)
        for cp in copies:
            cp.wait_recv()
        for cp in copies:
            cp.wait_send()
        local.wait()

    return pl.pallas_call(
        body, name=name,
        in_specs=[pl.BlockSpec(memory_space=pl.ANY)], out_specs=pl.BlockSpec(memory_space=pl.ANY),
        out_shape=jax.ShapeDtypeStruct((N_DEV, R, LANES), x.dtype),
        scratch_shapes=[pltpu.SemaphoreType.DMA((N_DEV - 1,)), pltpu.SemaphoreType.DMA((N_DEV - 1,)),
                        pltpu.SemaphoreType.DMA(())],
    )(x)


def _gather_two_level(x, name):
    R = x.shape[0]

    def body(x_ref, out_ref, send_sems, recv_sems, local_sem):
        mx, my, mc = _mesh_pos()
        me, sibling = (mx, my, mc), (mx, my, 1 - mc)
        chips = [(1 - mx, my), (mx, 1 - my), (1 - mx, 1 - my)]

        def slot(px, py, pc):
            return out_ref.at[4 * px + 2 * py + pc]

        def copy(k, block, to, src=None):
            return pltpu.make_async_remote_copy(
                src_ref=slot(*block) if src is None else src, dst_ref=slot(*block), send_sem=send_sems.at[k],
                recv_sem=recv_sems.at[k], device_id=to, device_id_type=pl.DeviceIdType.MESH)

        mine = pltpu.make_async_copy(x_ref, slot(*me), local_sem)
        mine.start()
        first = [copy(0, me, sibling, src=x_ref)]
        first += [copy(1 + j, me, (*chip, mc), src=x_ref) for j, chip in enumerate(chips)]
        for cp in first:
            cp.start()
        passed = [copy(4 + j, (*chip, mc), sibling) for j, chip in enumerate(chips)]
        for j, chip in enumerate(chips):
            copy(1 + j, (*chip, mc), me).wait_recv()
            passed[j].start()
        copy(0, sibling, me).wait_recv()
        for j, chip in enumerate(chips):
            copy(4 + j, (*chip, 1 - mc), me).wait_recv()
        for cp in first + passed:
            cp.wait_send()
        mine.wait()

    return pl.pallas_call(
        body, name=name,
        in_specs=[pl.BlockSpec(memory_space=pl.ANY)], out_specs=pl.BlockSpec(memory_space=pl.ANY),
        out_shape=jax.ShapeDtypeStruct((N_DEV, R, LANES), x.dtype),
        scratch_shapes=[pltpu.SemaphoreType.DMA((N_DEV - 1,)), pltpu.SemaphoreType.DMA((N_DEV - 1,)),
                        pltpu.SemaphoreType.DMA(())],
    )(x)


SEGMENT_ROWS = 16


def _seg_rows(shape):
    n = int(np.prod(shape))
    return -(-n // (SEGMENT_ROWS * LANES)) * SEGMENT_ROWS


def _flat_total_rows(shapes, multiple):
    rows = sum(_seg_rows(s) for s in shapes)
    return -(-rows // multiple) * multiple


def _to_rows(a, lead=()):
    shape = a.shape[len(lead):]
    n, rows = int(np.prod(shape)), _seg_rows(shape)
    if n == rows * LANES:
        return a.reshape(lead + (rows, LANES))
    flat = a.reshape(lead + (n,))
    flat = jnp.pad(flat, [(0, 0)] * len(lead) + [(0, rows * LANES - n)])
    return flat.reshape(lead + (rows, LANES))


def _from_rows(seg, shape, lead=()):
    n, rows = int(np.prod(shape)), _seg_rows(shape)
    if n == rows * LANES:
        return seg.reshape(lead + tuple(shape))
    return seg.reshape(lead + (rows * LANES,))[..., :n].reshape(lead + tuple(shape))


def _pack_flat(arrs, multiple, lead=()):
    parts = [_to_rows(a, lead) for a in arrs]
    rows = sum(p.shape[-2] for p in parts)
    total = -(-rows // multiple) * multiple
    if total != rows:
        parts.append(jnp.zeros(lead + (total - rows, LANES), parts[0].dtype))
    return jnp.concatenate(parts, axis=len(lead))


def _unpack_flat(flat, shapes, lead=()):
    out, r0 = [], 0
    for s in shapes:
        rows = _seg_rows(s)
        out.append(_from_rows(flat[..., r0:r0 + rows, :], s, lead))
        r0 += rows
    return out


def _adamw_sum(name, g_slabs, w, m, v, tr):
    R = w.shape[0]
    c1 = 1.0 - ADAM_B1 ** ADAM_STEP
    c2 = 1.0 - ADAM_B2 ** ADAM_STEP

    def fn(*a):
        g = a[0].astype(F32)
        for d in range(1, N_DEV):
            g = g + a[d].astype(F32)
        w, m, v = a[N_DEV:]
        m_new = ADAM_B1 * m + (1.0 - ADAM_B1) * g
        v_new = ADAM_B2 * v + (1.0 - ADAM_B2) * (g * g)
        delta = -ADAM_LR * ((m_new / c1) / (jnp.sqrt(v_new / c2) + ADAM_EPS) + ADAM_WD * w)
        return g, delta, m_new, v_new

    tr = min(tr, R)
    ins = [(g_slabs, LANES, 0, d * (R // tr)) for d in range(N_DEV)] + [_whole(w), _whole(m), _whole(v)]
    return _rowwise(name, fn, R, ins, [], [(LANES, F32)] * 4, tr=tr)


def _full_weights(gathered, local_shapes):
    segs = _unpack_flat(gathered, local_shapes, lead=(N_DEV,))
    out = {}
    for (name, kind), shp, seg in zip(SHARDED, local_shapes, segs):
        L, a, b = shp
        if kind == 'col':
            out[name] = seg.transpose(1, 2, 0, 3).reshape(L, a, N_DEV * b)
        else:
            out[name] = seg.transpose(1, 0, 2, 3).reshape(L, N_DEV * a, b)
    return out


def _pack_grads(grads, local_shapes):
    parts = []
    for (name, kind), shp in zip(SHARDED, local_shapes):
        L, a, b = shp
        g = grads[name].astype(BF16)
        if kind == 'col':
            parts.append(g.reshape(L, a, N_DEV, b).transpose(2, 0, 1, 3))
        else:
            parts.append(g.reshape(L, N_DEV, a, b).transpose(1, 0, 2, 3))
    return _pack_flat(parts, FLAT_ROW_MULTIPLE, lead=(N_DEV,))


def _rearrange_w_in(w):
    z = lambda n: jnp.zeros((w.shape[0], n), w.dtype)
    return jnp.concatenate([
        w[:, O_GATE:O_END], w[:, O_CONV:O_GATE], w[:, O_QB:O_KB], w[:, O_KB:O_VB], w[:, O_VB:O_F],
        w[:, O_CQ:O_CKV], z(CQ_PAD - Q_LORA), w[:, O_CKV:O_KR], w[:, O_KR:O_QB], w[:, O_F:O_CONV],
        z(LANES - MLA_ROPE - FOX_HEADS), z(IN_COLS - COL_SMALL - LANES)], axis=1)


def _restore_w_in(g):
    return jnp.concatenate([
        g[:, COL_CQ:COL_CQ + Q_LORA], g[:, COL_CKV:COL_CKV + KV_LORA], g[:, COL_SMALL:COL_SMALL + MLA_ROPE],
        g[:, COL_QB:COL_KB], g[:, COL_KB:COL_VB], g[:, COL_VB:COL_CQ],
        g[:, COL_SMALL + MLA_ROPE:COL_SMALL + MLA_ROPE + FOX_HEADS], g[:, COL_CONV:COL_QB], g[:, COL_GATE:COL_CONV]],
        axis=1)


def _rearrange_w_uq(w):
    w3 = w.reshape(Q_LORA, MLA_HEADS, MLA_NOPE + MLA_ROPE)
    cols = jnp.concatenate([w3[:, :, :MLA_NOPE].reshape(Q_LORA, -1),
                            w3[:, :, MLA_NOPE:MLA_NOPE + ROPE_HALF].reshape(Q_LORA, -1),
                            w3[:, :, MLA_NOPE + ROPE_HALF:].reshape(Q_LORA, -1)], axis=1)
    return jnp.pad(cols, ((0, CQ_PAD - Q_LORA), (0, 0)))


def _restore_w_uq(g):
    g = g[:Q_LORA]
    n = MLA_HEADS * MLA_NOPE
    h = MLA_HEADS * ROPE_HALF
    parts = [g[:, :n].reshape(Q_LORA, MLA_HEADS, MLA_NOPE), g[:, n:n + h].reshape(Q_LORA, MLA_HEADS, ROPE_HALF),
             g[:, n + h:].reshape(Q_LORA, MLA_HEADS, ROPE_HALF)]
    return jnp.concatenate(parts, axis=2).reshape(Q_LORA, -1)


def _rearrange_w_ukv(w):
    w3 = w.reshape(KV_LORA, MLA_HEADS, MLA_NOPE + MLA_V)
    return jnp.concatenate([w3[:, :, :MLA_NOPE].reshape(KV_LORA, -1), w3[:, :, MLA_NOPE:].reshape(KV_LORA, -1)], axis=1)


def _restore_w_ukv(g):
    n = MLA_HEADS * MLA_NOPE
    parts = [g[:, :n].reshape(KV_LORA, MLA_HEADS, MLA_NOPE), g[:, n:].reshape(KV_LORA, MLA_HEADS, MLA_V)]
    return jnp.concatenate(parts, axis=2).reshape(KV_LORA, -1)


def _heads(a, H):
    S = a.shape[0]
    return a.reshape(S, H, -1).transpose(1, 0, 2)


def _tiles_t(a, H):
    S = a.shape[0]
    t = _att_tile(S)
    return a.reshape(S // t, t, H, -1).transpose(2, 0, 3, 1)


def _unheads(a):
    H, S, d = a.shape
    return a.transpose(1, 0, 2).reshape(S, H * d)


def _rope_q(x_src, cos, sin, name):
    def fn(x1, x2, c, s):
        return x1 * c - x2 * s, x2 * c + x1 * s

    S = x_src.shape[0]
    return _rowwise(name, fn, S, [(x_src, LANES, 4, 0), (x_src, LANES, 5, 0), _whole(cos), _whole(sin)], [],
                    [(LANES, F32), (LANES, F32)], tr=512)


def _rope_k(x_in, cos_k, sin_k, fold_heads, name):
    def fn(x, c, s):
        if fold_heads:
            x = x[:, :LANES] + x[:, LANES:]
            x = x + pltpu.roll(x, 64, 1)
            x = x + pltpu.roll(x, 32, 1)
        lane = lax.broadcasted_iota(jnp.int32, x.shape, 1)
        partner = jnp.where(lane < ROPE_HALF, pltpu.roll(x, LANES - ROPE_HALF, 1), pltpu.roll(x, ROPE_HALF, 1))
        return (x * c + partner * s,)

    S = x_in[0].shape[0]
    return _rowwise(name, fn, S, [x_in, _whole(cos_k), _whole(sin_k)], [], [(LANES, F32)], tr=512)[0]


def _layer_forward(x, W, T, l):
    S = x.shape[0]
    nm = lambda s: f"{s}_l{l}"
    h1 = _rms_fwd(_whole(x), W['norm_mix_g'], D_MODEL, nm("rms_mix"))
    proj = _mm(h1, W['w_in'], name=nm("mm_in"))
    small = proj[:, COL_SMALL:COL_SMALL + LANES]

    cqn = _rms_fwd((proj, CQ_PAD, COL_CQ // CQ_PAD, 0), W['q_norm_g'], Q_LORA, nm("rms_q"))
    ckvn = _rms_fwd((proj, KV_LORA, COL_CKV // KV_LORA, 0), W['kv_norm_g'], KV_LORA, nm("rms_kv"))
    qa = _mm(cqn, W['w_uq'], name=nm("mm_uq"))
    kv = _mm(ckvn, W['w_ukv'], out_dtype=BF16, name=nm("mm_ukv"))
    q_r1, q_r2 = _rope_q(qa, T['cos_q'], T['sin_q'], nm("rope_q"))
    k_rope = _rope_k(_whole(small), T['cos_k'], T['sin_k'], False, nm("rope_k"))[:, :MLA_ROPE]
    n_nope = MLA_HEADS * MLA_NOPE
    mla_scale = (MLA_NOPE + MLA_ROPE) ** -0.5
    q_mla = (jnp.concatenate([qa[:, :n_nope].reshape(S, MLA_HEADS, MLA_NOPE), q_r1.reshape(S, MLA_HEADS, ROPE_HALF),
                              q_r2.reshape(S, MLA_HEADS, ROPE_HALF)], axis=2)
             * (mla_scale * LOG2E)).astype(BF16).transpose(1, 0, 2)
    k_mla = jnp.concatenate([kv[:, :n_nope].reshape(S, MLA_HEADS, MLA_NOPE),
                             jnp.broadcast_to(k_rope.astype(BF16)[:, None, :], (S, MLA_HEADS, MLA_ROPE))],
                            axis=2).transpose(1, 0, 2)
    v_mla = _heads(kv[:, n_nope:], MLA_HEADS)
    o_mla_t, lse_a = _attn_fwd(q_mla, k_mla, _tiles_t(kv[:, n_nope:], MLA_HEADS), CHUNK, None, nm("mla_fwd"))
    o_mla = o_mla_t.transpose(0, 2, 1)
    oa_cat = o_mla_t.transpose(2, 0, 1).reshape(S, MLA_HEADS * MLA_V)
    o_a = _mm(oa_cat, W['w_bo_a'], name=nm("mm_bo_a"))

    f_t = small[:, MLA_ROPE:MLA_ROPE + FOX_HEADS].T
    cum = _fox_gate_fwd(f_t, W['b_forget'], nm("fox_gate"))
    fox_scale = FOX_HEAD_DIM ** -0.5
    q_fox = _heads((proj[:, COL_QB:COL_KB] * (fox_scale * LOG2E)).astype(BF16), FOX_HEADS)
    k_fox = _heads(proj[:, COL_KB:COL_VB].astype(BF16), FOX_HEADS)
    vb = proj[:, COL_VB:COL_CQ].astype(BF16)
    v_fox = _heads(vb, FOX_HEADS)
    o_fox_t, lse_b = _attn_fwd(q_fox, k_fox, _tiles_t(vb, FOX_HEADS), 1, cum, nm("fox_fwd"))
    o_fox = o_fox_t.transpose(0, 2, 1)
    ob_cat = o_fox_t.transpose(2, 0, 1).reshape(S, FOX_HEADS * FOX_HEAD_DIM)
    o_b = _mm(ob_cat, W['w_bo_b'], name=nm("mm_bo_b"))

    u3 = _conv_fwd(proj, W['dw_kernel'], W['dw_bias'], W['conv_ln_g'], W['conv_ln_b'], nm("conv_fwd"))
    o_c = _mm(u3, W['w_bo_c'], name=nm("mm_bo_c"))

    def gate_fn(la, lb, lc, oa, ob, oc, bg):
        ga = _sigmoid(la + bg[:, :D_MODEL])
        gb = _sigmoid(lb + bg[:, D_MODEL:2 * D_MODEL])
        gc = _sigmoid(lc + bg[:, 2 * D_MODEL:])
        return (ga * oa + gb * ob + gc * oc,)

    logit_ins = [(proj, D_MODEL, COL_GATE // D_MODEL + b, 0) for b in range(3)]
    y = _rowwise(nm("gate_fwd"), gate_fn, S, logit_ins + [_whole(o_a), _whole(o_b), _whole(o_c)], [W['b_gate']],
                 [(D_MODEL, BF16)])[0]
    x2 = _mm(y, W['w_out'], add=x, name=nm("mm_out"))

    h2 = _rms_fwd(_whole(x2), W['norm_ffn_g'], D_MODEL, nm("rms_ffn"))
    gu = _mm(h2, W['w_gu'], name=nm("mm_gu"))

    def swiglu_fn(gt, up):
        return (gt * _sigmoid(gt) * up,)

    ff = _rowwise(nm("swiglu_fwd"), swiglu_fn, S, [(gu, FFN_HIDDEN, 0, 0), (gu, FFN_HIDDEN, 1, 0)], [],
                  [(FFN_HIDDEN, BF16)])[0]
    x3 = _mm(ff, W['w_ffn_down'], add=x2, name=nm("mm_down"))

    saved = dict(x=x, h1=h1, proj=proj, small=small, cqn=cqn, ckvn=ckvn, q_mla=q_mla, k_mla=k_mla, v_mla=v_mla,
                 o_mla=o_mla, lse_a=lse_a, oa_cat=oa_cat, o_a=o_a, f_t=f_t, cum=cum, q_fox=q_fox, k_fox=k_fox,
                 v_fox=v_fox, o_fox=o_fox, lse_b=lse_b, ob_cat=ob_cat, o_b=o_b, u3=u3, o_c=o_c, y=y, x2=x2, h2=h2,
                 gu=gu, ff=ff)
    return x3, saved


def _layer_backward(dx3, dx3_b, sv, W, T, l):
    S = dx3.shape[0]
    nm = lambda s: f"{s}_l{l}"
    G = {}

    G['w_ffn_down'] = _mm(sv['ff'], dx3_b, mode="tn", name=nm("mm_down_dw"))
    dff = _mm(dx3_b, W['w_ffn_down'], mode="nt", name=nm("mm_down_dx"))

    def swiglu_bwd_fn(gt, up, d):
        sg = _sigmoid(gt)
        return (jnp.concatenate([d * up * (sg * (1.0 + gt * (1.0 - sg))), d * (gt * sg)], axis=1),)

    dgu = _rowwise(nm("swiglu_bwd"), swiglu_bwd_fn, S,
                   [(sv['gu'], FFN_HIDDEN, 0, 0), (sv['gu'], FFN_HIDDEN, 1, 0), _whole(dff)], [],
                   [(2 * FFN_HIDDEN, BF16)])[0]
    G['w_gu'] = _mm(sv['h2'], dgu, mode="tn", name=nm("mm_gu_dw"))
    dh2 = _mm(dgu, W['w_gu'], mode="nt", name=nm("mm_gu_dx"))
    dx2, dx2_b, G['norm_ffn_g'] = _rms_bwd(_whole(sv['x2']), dh2, W['norm_ffn_g'], D_MODEL, dx3, nm("rms_ffn_bwd"))

    G['w_out'] = _mm(sv['y'], dx2_b, mode="tn", name=nm("mm_out_dw"))
    dy = _mm(dx2_b, W['w_out'], mode="nt", name=nm("mm_out_dx"))

    def gate_bwd_fn(la, lb, lc, oa, ob, oc, dy, bg):
        outs, dls = [], []
        for k, (lg, o) in enumerate(((la, oa), (lb, ob), (lc, oc))):
            g = _sigmoid(lg + bg[:, k * D_MODEL:(k + 1) * D_MODEL])
            outs.append(dy * g)
            dls.append(dy * o * g * (1.0 - g))
        dl = jnp.concatenate(dls, axis=1)
        return (*outs, dl, _colsum(dl))

    proj = sv['proj']
    logit_ins = [(proj, D_MODEL, COL_GATE // D_MODEL + b, 0) for b in range(3)]
    do_a, do_b, do_c, dlogit, G['b_gate'] = _rowwise(
        nm("gate_bwd"), gate_bwd_fn, S, logit_ins + [_whole(sv['o_a']), _whole(sv['o_b']), _whole(sv['o_c']), _whole(dy)],
        [W['b_gate']], [(D_MODEL, BF16)] * 3 + [(3 * D_MODEL, BF16)], [(1, 3 * D_MODEL)], tr=128)

    G['w_bo_c'] = _mm(sv['u3'], do_c, mode="tn", name=nm("mm_bo_c_dw"))
    du3 = _mm(do_c, W['w_bo_c'], mode="nt", name=nm("mm_bo_c_dx"))
    du1, G['dw_kernel'], G['dw_bias'], G['conv_ln_g'], G['conv_ln_b'] = _conv_bwd_a(
        proj, du3, W['dw_kernel'], W['dw_bias'], W['conv_ln_g'], W['conv_ln_b'], nm("conv_bwd_a"))
    dconv = _conv_bwd_b(proj, du1, W['dw_kernel'], nm("conv_bwd_b"))

    G['w_bo_b'] = _mm(sv['ob_cat'], do_b, mode="tn", name=nm("mm_bo_b_dw"))
    dob = _heads(_mm(do_b, W['w_bo_b'], mode="nt", out_dtype=BF16, name=nm("mm_bo_b_dx")), FOX_HEADS)
    delta_b = _attn_delta(sv['o_fox'], dob, nm("fox_delta"))
    dq_f, dk_f, dv_f, dcum_k, dcum_q = _attn_bwd(
        sv['q_fox'], sv['k_fox'], sv['v_fox'], dob, sv['lse_b'].reshape(FOX_HEADS, S), delta_b, FOX_HEAD_DIM ** -0.5, 1,
        sv['cum'], nm("fox_bwd"))
    dz, G['b_forget'] = _fox_gate_bwd(sv['f_t'], W['b_forget'], dcum_k.reshape(FOX_HEADS, S),
                                      dcum_q.reshape(FOX_HEADS, S), nm("fox_gate_bwd"))

    G['w_bo_a'] = _mm(sv['oa_cat'], do_a, mode="tn", name=nm("mm_bo_a_dw"))
    doa = _heads(_mm(do_a, W['w_bo_a'], mode="nt", out_dtype=BF16, name=nm("mm_bo_a_dx")), MLA_HEADS)
    delta_a = _attn_delta(sv['o_mla'], doa, nm("mla_delta"))
    dq_m, dk_m, dv_m = _attn_bwd(sv['q_mla'], sv['k_mla'], sv['v_mla'], doa, sv['lse_a'].reshape(MLA_HEADS, S),
                                 delta_a, (MLA_NOPE + MLA_ROPE) ** -0.5, CHUNK, None, nm("mla_bwd"))
    dq_s = dq_m.transpose(1, 0, 2)
    dqr = jnp.concatenate([dq_s[:, :, MLA_NOPE:MLA_NOPE + ROPE_HALF].reshape(S, -1),
                           dq_s[:, :, MLA_NOPE + ROPE_HALF:].reshape(S, -1)], axis=1)

    def rope_q_bwd_fn(d1, d2, c, s):
        return d1 * c + d2 * s, d2 * c - d1 * s

    dq_r1, dq_r2 = _rowwise(nm("rope_q_bwd"), rope_q_bwd_fn, S,
                            [(dqr, LANES, 0, 0), (dqr, LANES, 1, 0), _whole(T['cos_q']), _whole(T['sin_q'])], [],
                            [(LANES, BF16), (LANES, BF16)], tr=512)
    dqa = jnp.concatenate([dq_s[:, :, :MLA_NOPE].reshape(S, -1).astype(BF16), dq_r1, dq_r2], axis=1)
    G['w_uq'] = _mm(sv['cqn'], dqa, mode="tn", name=nm("mm_uq_dw"))
    dcqn = _mm(dqa, W['w_uq'], mode="nt", name=nm("mm_uq_dx"))
    dcq, G['q_norm_g'] = _rms_bwd((proj, CQ_PAD, COL_CQ // CQ_PAD, 0), dcqn, W['q_norm_g'], Q_LORA, None,
                                  nm("rms_q_bwd"))
    dk_s = dk_m.transpose(1, 0, 2)
    dkv = jnp.concatenate([dk_s[:, :, :MLA_NOPE].reshape(S, -1), _unheads(dv_m)], axis=1).astype(BF16)
    G['w_ukv'] = _mm(sv['ckvn'], dkv, mode="tn", name=nm("mm_ukv_dw"))
    dckvn = _mm(dkv, W['w_ukv'], mode="nt", name=nm("mm_ukv_dx"))
    dckv, G['kv_norm_g'] = _rms_bwd((proj, KV_LORA, COL_CKV // KV_LORA, 0), dckvn, W['kv_norm_g'], KV_LORA, None,
                                    nm("rms_kv_bwd"))
    dk_rope_heads = dk_s[:, :, MLA_NOPE:].reshape(S, MLA_HEADS * MLA_ROPE)
    dkr = _rope_k(_whole(dk_rope_heads), T['cos_k'], T['sin_k_neg'], True, nm("rope_k_bwd"))

    dsmall = jnp.concatenate([dkr[:, :MLA_ROPE], dz.T, jnp.zeros((S, LANES - MLA_ROPE - FOX_HEADS), F32)], axis=1)
    dproj = jnp.concatenate([
        dlogit, dconv, _unheads(dq_f).astype(BF16), _unheads(dk_f).astype(BF16), _unheads(dv_f).astype(BF16),
        dcq.astype(BF16), dckv.astype(BF16), dsmall.astype(BF16),
        jnp.zeros((S, IN_COLS - COL_SMALL - LANES), BF16)], axis=1)
    G['w_in'] = _mm(sv['h1'], dproj, mode="tn", name=nm("mm_in_dw"))
    dh1 = _mm(dproj, W['w_in'], mode="nt", name=nm("mm_in_dx"))
    dx, dx_b, G['norm_mix_g'] = _rms_bwd(_whole(sv['x']), dh1, W['norm_mix_g'], D_MODEL, dx2, nm("rms_mix_bwd"))
    return dx, dx_b, G


def _loss_head(x, target, g, name):
    def fn(x, t, g):
        r = lax.rsqrt(jnp.mean(x * x, axis=1, keepdims=True) + RMS_EPS)
        xh = x * r
        e = xh * g - t
        part = 0.5 * jnp.sum(jnp.mean(e * e, axis=1, keepdims=True), axis=0, keepdims=True)
        dy = e * (1.0 / D_MODEL)
        dxh = dy * g
        dx = r * (dxh - xh * jnp.mean(dxh * xh, axis=1, keepdims=True))
        return dx, dx, jnp.broadcast_to(part, (1, LANES)), _colsum(dy * xh)

    S = x.shape[0]
    dx, dx_b, part, dg = _rowwise(name, fn, S, [_whole(x), _whole(target)], [g], [(D_MODEL, F32), (D_MODEL, BF16)],
                                  [(1, LANES), (1, D_MODEL)])
    return part[0, 0], dx, dx_b, dg


def kernel(x, positions, norm_mix_g, w_in, b_gate, q_norm_g, w_uq, kv_norm_g, w_ukv, b_forget, dw_kernel, dw_bias, conv_ln_g, conv_ln_b, w_bo_a, w_bo_b, w_bo_c, w_out, norm_ffn_g, w_ffn_gate, w_ffn_up, w_ffn_down, final_norm_g, loss_target, m_norm_mix_g, m_w_in, m_b_gate, m_q_norm_g, m_w_uq, m_kv_norm_g, m_w_ukv, m_b_forget, m_dw_kernel, m_dw_bias, m_conv_ln_g, m_conv_ln_b, m_w_bo_a, m_w_bo_b, m_w_bo_c, m_w_out, m_norm_ffn_g, m_w_ffn_gate, m_w_ffn_up, m_w_ffn_down, m_final_norm_g, v_norm_mix_g, v_w_in, v_b_gate, v_q_norm_g, v_w_uq, v_kv_norm_g, v_w_ukv, v_b_forget, v_dw_kernel, v_dw_bias, v_conv_ln_g, v_conv_ln_b, v_w_bo_a, v_w_bo_b, v_w_bo_c, v_w_out, v_norm_ffn_g, v_w_ffn_gate, v_w_ffn_up, v_w_ffn_down, v_final_norm_g):
    local = dict(norm_mix_g=norm_mix_g, w_in=w_in, b_gate=b_gate, q_norm_g=q_norm_g, w_uq=w_uq, kv_norm_g=kv_norm_g,
                 w_ukv=w_ukv, b_forget=b_forget, dw_kernel=dw_kernel, dw_bias=dw_bias, conv_ln_g=conv_ln_g,
                 conv_ln_b=conv_ln_b, w_bo_a=w_bo_a, w_bo_b=w_bo_b, w_bo_c=w_bo_c, w_out=w_out, norm_ffn_g=norm_ffn_g,
                 w_ffn_gate=w_ffn_gate, w_ffn_up=w_ffn_up, w_ffn_down=w_ffn_down, final_norm_g=final_norm_g)
    mom_m = dict(norm_mix_g=m_norm_mix_g, w_in=m_w_in, b_gate=m_b_gate, q_norm_g=m_q_norm_g, w_uq=m_w_uq,
                 kv_norm_g=m_kv_norm_g, w_ukv=m_w_ukv, b_forget=m_b_forget, dw_kernel=m_dw_kernel, dw_bias=m_dw_bias,
                 conv_ln_g=m_conv_ln_g, conv_ln_b=m_conv_ln_b, w_bo_a=m_w_bo_a, w_bo_b=m_w_bo_b, w_bo_c=m_w_bo_c,
                 w_out=m_w_out, norm_ffn_g=m_norm_ffn_g, w_ffn_gate=m_w_ffn_gate, w_ffn_up=m_w_ffn_up,
                 w_ffn_down=m_w_ffn_down, final_norm_g=m_final_norm_g)
    mom_v = dict(norm_mix_g=v_norm_mix_g, w_in=v_w_in, b_gate=v_b_gate, q_norm_g=v_q_norm_g, w_uq=v_w_uq,
                 kv_norm_g=v_kv_norm_g, w_ukv=v_w_ukv, b_forget=v_b_forget, dw_kernel=v_dw_kernel, dw_bias=v_dw_bias,
                 conv_ln_g=v_conv_ln_g, conv_ln_b=v_conv_ln_b, w_bo_a=v_w_bo_a, w_bo_b=v_w_bo_b, w_bo_c=v_w_bo_c,
                 w_out=v_w_out, norm_ffn_g=v_norm_ffn_g, w_ffn_gate=v_w_ffn_gate, w_ffn_up=v_w_ffn_up,
                 w_ffn_down=v_w_ffn_down, final_norm_g=v_final_norm_g)
    S = x.shape[1]
    xs = x[0]
    sh_names = [n for n, _ in SHARDED]
    sh_shapes = [local[n].shape for n in sh_names]
    rep_shapes = [local[n].shape for n in REPLICATED]

    w_flat = _pack_flat([local[n] for n in sh_names], FLAT_ROW_MULTIPLE)
    gathered = _gather_two_level(w_flat.astype(BF16), "gather_weights")
    full = _full_weights(gathered, sh_shapes)

    def layer_weights(l):
        W = {n: full[n][l] for n in ('w_bo_a', 'w_bo_b', 'w_bo_c', 'w_out', 'w_ffn_down')}
        W['w_in'] = _rearrange_w_in(full['w_in'][l])
        W['w_uq'] = _rearrange_w_uq(full['w_uq'][l])
        W['w_ukv'] = _rearrange_w_ukv(full['w_ukv'][l])
        W['w_gu'] = jnp.concatenate([full['w_ffn_gate'][l], full['w_ffn_up'][l]], axis=1)
        W['dw_kernel'] = jnp.pad(full['dw_kernel'][l].astype(F32), ((0, 32 - CONV_WIDTH), (0, 0)))
        for n in ('norm_mix_g', 'b_gate', 'kv_norm_g', 'dw_bias', 'conv_ln_g', 'conv_ln_b', 'norm_ffn_g'):
            W[n] = local[n][l][None, :]
        W['q_norm_g'] = jnp.pad(local['q_norm_g'][l], (0, CQ_PAD - Q_LORA))[None, :]
        W['b_forget'] = local['b_forget'][l][:, None]
        return W

    inv_freq = 1.0 / (ROPE_THETA ** (jnp.arange(0, MLA_ROPE, 2, dtype=F32) / MLA_ROPE))
    ang = positions[0].astype(F32)[:, None] * inv_freq
    cos, sin = jnp.cos(ang), jnp.sin(ang)
    zpad = jnp.zeros((S, LANES - MLA_ROPE), F32)
    T = dict(cos_q=jnp.tile(cos, (1, MLA_HEADS)), sin_q=jnp.tile(sin, (1, MLA_HEADS)),
             cos_k=jnp.concatenate([cos, cos, zpad], axis=1), sin_k=jnp.concatenate([-sin, sin, zpad], axis=1),
             sin_k_neg=jnp.concatenate([sin, -sin, zpad], axis=1))

    Ws, saved = [], []
    h = xs
    for l in range(DEPTH):
        W = layer_weights(l)
        h, sv = _layer_forward(h, W, T, l)
        Ws.append(W)
        saved.append(sv)
    loss_part, dh, dh_b, dg_final = _loss_head(h, loss_target[0], local['final_norm_g'][None, :], "loss_head")
    loss = lax.psum(loss_part, ("x", "y", "c"))
    layer_grads = [None] * DEPTH
    for l in range(DEPTH - 1, -1, -1):
        dh, dh_b, layer_grads[l] = _layer_backward(dh, dh_b, saved[l], Ws[l], T, l)
    grad_x = dh[None]

    grads_full = {}
    grads_full['w_in'] = jnp.stack([_restore_w_in(g['w_in']) for g in layer_grads])
    grads_full['w_uq'] = jnp.stack([_restore_w_uq(g['w_uq']) for g in layer_grads])
    grads_full['w_ukv'] = jnp.stack([_restore_w_ukv(g['w_ukv']) for g in layer_grads])
    grads_full['dw_kernel'] = jnp.stack([g['dw_kernel'][:CONV_WIDTH] for g in layer_grads])
    for n in ('w_bo_a', 'w_bo_b', 'w_bo_c', 'w_out', 'w_ffn_down'):
        grads_full[n] = jnp.stack([g[n] for g in layer_grads])
    grads_full['w_ffn_gate'] = jnp.stack([g['w_gu'][:, :FFN_HIDDEN] for g in layer_grads])
    grads_full['w_ffn_up'] = jnp.stack([g['w_gu'][:, FFN_HIDDEN:] for g in layer_grads])
    packed = _pack_grads(grads_full, sh_shapes)
    received = _exchange(packed, False, "scatter_grads")
    R = w_flat.shape[0]
    m_flat = _pack_flat([mom_m[n] for n in sh_names], FLAT_ROW_MULTIPLE)
    v_flat = _pack_flat([mom_v[n] for n in sh_names], FLAT_ROW_MULTIPLE)
    g_sh, d_sh, nm_sh, nv_sh = _adamw_sum("adamw_sharded", received.reshape(N_DEV * R, LANES), w_flat, m_flat, v_flat,
                                          512)

    rep_grads = {
        'norm_mix_g': jnp.concatenate([g['norm_mix_g'] for g in layer_grads]),
        'b_gate': jnp.concatenate([g['b_gate'] for g in layer_grads]),
        'q_norm_g': jnp.concatenate([g['q_norm_g'][:, :Q_LORA] for g in layer_grads]),
        'kv_norm_g': jnp.concatenate([g['kv_norm_g'] for g in layer_grads]),
        'b_forget': jnp.concatenate([g['b_forget'].T for g in layer_grads]),
        'dw_bias': jnp.concatenate([g['dw_bias'] for g in layer_grads]),
        'conv_ln_g': jnp.concatenate([g['conv_ln_g'] for g in layer_grads]),
        'conv_ln_b': jnp.concatenate([g['conv_ln_b'] for g in layer_grads]),
        'norm_ffn_g': jnp.concatenate([g['norm_ffn_g'] for g in layer_grads]),
        'final_norm_g': dg_final[0],
    }
    REP_ROWS = 256
    rg_flat = _pack_flat([rep_grads[n] for n in REPLICATED], REP_ROWS)
    rg_all = _exchange(rg_flat, True, "gather_replicated_grads")
    Rr = rg_flat.shape[0]
    rw = _pack_flat([local[n] for n in REPLICATED], REP_ROWS)
    rm = _pack_flat([mom_m[n] for n in REPLICATED], REP_ROWS)
    rv = _pack_flat([mom_v[n] for n in REPLICATED], REP_ROWS)
    g_rp, d_rp, nm_rp, nv_rp = _adamw_sum("adamw_replicated", rg_all.reshape(N_DEV * Rr, LANES), rw, rm, rv, Rr)

    def by_name(flat_sh, flat_rp):
        d = dict(zip(sh_names, _unpack_flat(flat_sh, sh_shapes)))
        d.update(zip(REPLICATED, _unpack_flat(flat_rp, rep_shapes)))
        return [d[n] for n in WEIGHT_NAMES]

    return (loss, grad_x, *by_name(g_sh, g_rp), *by_name(d_sh, d_rp), *by_name(nm_sh, nm_rp), *by_name(nv_sh, nv_rp))
```

```python
import functools

import numpy as np
import jax
import jax.numpy as jnp
from jax import lax
from jax.experimental import pallas as pl
from jax.experimental.pallas import tpu as pltpu

F32 = jnp.float32
BF16 = jnp.bfloat16

D_MODEL = 1024
DEPTH = 4
CHUNK = 64
MLA_HEADS, MLA_NOPE, MLA_ROPE, MLA_V = 8, 64, 32, 64
Q_LORA, KV_LORA = 384, 256
ROPE_THETA = 10000.0
FOX_HEADS, FOX_HEAD_DIM = 8, 64
FOX_FEATURES = 80
CONV_CHANNELS, CONV_WIDTH = 512, 31
FFN_HIDDEN = 2816
RMS_EPS = 1e-6
LN_EPS = 1e-5
ADAM_LR, ADAM_B1, ADAM_B2, ADAM_EPS, ADAM_WD, ADAM_STEP = 0.001, 0.9, 0.999, 1e-08, 0.01, 10

N_DEV = 8
LANES = 128
VMEM_LIMIT_BYTES = 56 * 1024 * 1024
NEG_BIG = -1e30
ROPE_HALF = MLA_ROPE // 2
CONV_HALO = 32

COL_GATE = 0
COL_CONV = 3072
COL_QB = 4096
COL_KB = 4608
COL_VB = 5120
COL_CQ = 5632
COL_CKV = 6144
COL_SMALL = 6400
IN_COLS = 6656
CQ_PAD = 512
O_CQ, O_CKV, O_KR, O_QB, O_KB, O_VB, O_F, O_CONV, O_GATE, O_END = 0, 384, 640, 672, 1184, 1696, 2208, 2216, 3240, 6312

WEIGHT_NAMES = ['norm_mix_g', 'w_in', 'b_gate', 'q_norm_g', 'w_uq', 'kv_norm_g', 'w_ukv', 'b_forget', 'dw_kernel',
                'dw_bias', 'conv_ln_g', 'conv_ln_b', 'w_bo_a', 'w_bo_b', 'w_bo_c', 'w_out', 'norm_ffn_g',
                'w_ffn_gate', 'w_ffn_up', 'w_ffn_down', 'final_norm_g']
SHARDED = [('w_in', 'col'), ('w_uq', 'col'), ('w_ukv', 'col'), ('dw_kernel', 'col'), ('w_bo_a', 'col'),
           ('w_bo_b', 'col'), ('w_bo_c', 'col'), ('w_out', 'row'), ('w_ffn_gate', 'col'), ('w_ffn_up', 'col'),
           ('w_ffn_down', 'row')]
REPLICATED = ['norm_mix_g', 'b_gate', 'q_norm_g', 'kv_norm_g', 'b_forget', 'dw_bias', 'conv_ln_g', 'conv_ln_b',
              'norm_ffn_g', 'final_norm_g']
FLAT_ROW_MULTIPLE = 512


def _pick(n, cands):
    for c in cands:
        if n % c == 0:
            return c
    raise ValueError(f"no tile for {n}")


def _params(sem):
    return pltpu.CompilerParams(dimension_semantics=sem, vmem_limit_bytes=VMEM_LIMIT_BYTES)


MM_ACC_BYTES = 8 * 1024 * 1024
MM_FULL_K = 2816
_LANE_TILES = (2048, 1664, 1536, 1408, 1024, 768, 512, 384, 256, 128)


def _mm_tiles(mode, M, N, K):
    if mode == "tn":
        tm = _pick(M, tuple(c for c in _LANE_TILES if c <= 1408))
        tn = _pick(N, tuple(c for c in _LANE_TILES if tm * c * 4 <= MM_ACC_BYTES))
        tk = _pick(K, (1024, 512, 256, 128))
    else:
        tm = _pick(M, (1024, 512, 256, 128))
        tn = _pick(N, (512, 384, 256, 128))
        tk = K if K <= MM_FULL_K else _pick(K, _LANE_TILES)
    return tm, tn, tk


def _mm(a, b, *, mode="nn", out_dtype=F32, add=None, name):
    if mode == "nn":
        (M, K), N = a.shape, b.shape[1]
    elif mode == "nt":
        (M, K), N = a.shape, b.shape[0]
    else:
        (K, M), N = a.shape, b.shape[1]
    tm, tn, tk = _mm_tiles(mode, M, N, K)
    nk = K // tk
    dims = {"nn": (((1,), (0,)), ((), ())), "nt": (((1,), (1,)), ((), ())), "tn": (((0,), (0,)), ((), ()))}[mode]
    has_add = add is not None

    def body(*refs):
        if has_add:
            a_ref, b_ref, add_ref, o_ref, acc_ref = refs
        else:
            a_ref, b_ref, o_ref, acc_ref = refs
        k = pl.program_id(2)

        @pl.when(k == 0)
        def _():
            acc_ref[...] = jnp.zeros_like(acc_ref)

        acc_ref[...] += lax.dot_general(a_ref[...].astype(BF16), b_ref[...].astype(BF16), dims,
                                        preferred_element_type=F32)

        @pl.when(k == nk - 1)
        def _():
            r = acc_ref[...]
            if has_add:
                r = r + add_ref[...]
            o_ref[...] = r.astype(o_ref.dtype)

    if mode == "nn":
        a_spec = pl.BlockSpec((tm, tk), lambda i, j, k: (i, k))
        b_spec = pl.BlockSpec((tk, tn), lambda i, j, k: (k, j))
    elif mode == "nt":
        a_spec = pl.BlockSpec((tm, tk), lambda i, j, k: (i, k))
        b_spec = pl.BlockSpec((tn, tk), lambda i, j, k: (j, k))
    else:
        a_spec = pl.BlockSpec((tk, tm), lambda i, j, k: (k, i))
        b_spec = pl.BlockSpec((tk, tn), lambda i, j, k: (k, j))
    o_spec = pl.BlockSpec((tm, tn), lambda i, j, k: (i, j))
    in_specs = [a_spec, b_spec] + ([o_spec] if has_add else [])
    args = (a, b) + ((add,) if has_add else ())
    return pl.pallas_call(
        body, name=name, grid=(M // tm, N // tn, nk), in_specs=in_specs, out_specs=o_spec,
        out_shape=jax.ShapeDtypeStruct((M, N), out_dtype),
        scratch_shapes=[pltpu.VMEM((tm, tn), F32)],
        compiler_params=_params(("parallel", "parallel", "arbitrary")),
    )(*args)


def _rowwise(name, fn, rows, row_ins, full_ins, outs, reds=(), tr=256):
    tr = min(tr, rows)
    assert rows % tr == 0
    n_r, n_f, n_o, n_d = len(row_ins), len(full_ins), len(outs), len(reds)

    def body(*refs):
        ins = [r[...] for r in refs[:n_r + n_f]]
        o_refs = refs[n_r + n_f:n_r + n_f + n_o]
        d_refs = refs[n_r + n_f + n_o:]
        res = fn(*ins)
        for o, v in zip(o_refs, res[:n_o]):
            o[...] = v.astype(o.dtype)
        if n_d:
            @pl.when(pl.program_id(0) == 0)
            def _():
                for d in d_refs:
                    d[...] = jnp.zeros_like(d)

            for d, v in zip(d_refs, res[n_o:]):
                d[...] += v

    in_specs = []
    for (arr, w, cidx, roff) in row_ins:
        in_specs.append(pl.BlockSpec((tr, w), functools.partial(lambda i, c, r: (i + r, c), c=cidx, r=roff)))
    for f in full_ins:
        in_specs.append(pl.BlockSpec(f.shape, lambda i: (0, 0)))
    out_specs = [pl.BlockSpec((tr, w), lambda i: (i, 0)) for (w, _) in outs]
    out_specs += [pl.BlockSpec((r, w), lambda i: (0, 0)) for (r, w) in reds]
    out_shape = [jax.ShapeDtypeStruct((rows, w), dt) for (w, dt) in outs]
    out_shape += [jax.ShapeDtypeStruct((r, w), F32) for (r, w) in reds]
    res = pl.pallas_call(
        body, name=name, grid=(rows // tr,), in_specs=in_specs, out_specs=out_specs, out_shape=out_shape,
        compiler_params=_params(("arbitrary",) if n_d else ("parallel",)),
    )(*[a for (a, _, _, _) in row_ins], *full_ins)
    return res


def _whole(arr, width=None, cidx=0, roff=0):
    return (arr, arr.shape[1] if width is None else width, cidx, roff)


def _colsum(v):
    return jnp.sum(v, axis=0, keepdims=True)


def _sigmoid(z):
    return 1.0 / (1.0 + jnp.exp(-z))


def _rms_fwd(x_in, g, n_true, name):
    def fn(x, g):
        x = x.astype(F32)
        r = lax.rsqrt(jnp.sum(x * x, axis=1, keepdims=True) * (1.0 / n_true) + RMS_EPS)
        return (x * r * g,)

    rows = x_in[0].shape[0]
    return _rowwise(name, fn, rows, [x_in], [g], [(x_in[1], BF16)])[0]


def _rms_bwd(x_in, dh, g, n_true, res, name):
    has_res = res is not None

    def fn(*a):
        if has_res:
            x, dh, rs, g = a
        else:
            x, dh, g = a
        x = x.astype(F32)
        dh = dh.astype(F32)
        r = lax.rsqrt(jnp.sum(x * x, axis=1, keepdims=True) * (1.0 / n_true) + RMS_EPS)
        xh = x * r
        dxh = dh * g
        dx = r * (dxh - xh * (jnp.sum(dxh * xh, axis=1, keepdims=True) * (1.0 / n_true)))
        if has_res:
            dx = dx + rs
            return dx, dx, _colsum(dh * xh)
        return dx, _colsum(dh * xh)

    rows, w = x_in[0].shape[0], x_in[1]
    ins = [x_in, _whole(dh)] + ([_whole(res)] if has_res else [])
    outs = [(w, F32), (w, BF16)] if has_res else [(w, F32)]
    return _rowwise(name, fn, rows, ins, [g], outs, [(1, w)])


ATT_SUB = 2
ATT_SUB_FWD = 2
ATT_HEADS_FWD = 2
ATT_HEADS_BWD = 2
LOG2E = 1.4426950408889634
LN2 = 0.6931471805599453


def _att_tile(S):
    return min(512, S // 2)


def _visible(q_idx, k_idx, group):
    if group == 1:
        return q_idx >= k_idx
    return (q_idx // group) >= (k_idx // group)


def _attn_fwd(q, k, v_t, group, cum2, name):
    H, S, dk = q.shape
    dv = v_t.shape[2]
    t = _att_tile(S)
    ts = t // ATT_SUB_FWD
    n = S // t
    HP = ATT_HEADS_FWD
    bias = cum2 is not None

    def body(*refs):
        if bias:
            q_ref, k_ref, vt_ref, ck_ref, o_ref, lse_ref = refs
        else:
            q_ref, k_ref, vt_ref, o_ref, lse_ref = refs
        i = pl.program_id(1)

        def step(j, carry, masked):
            start = pl.multiple_of(j * t, t)
            subs = []
            for g in range(HP):
                qv = q_ref[g]
                for h in range(ATT_SUB_FWD):
                    rs = pl.multiple_of(start + h * ts, ts)
                    kh = k_ref[g, pl.ds(rs, ts), :]
                    s = lax.dot_general(kh, qv, (((1,), (1,)), ((), ())), preferred_element_type=F32)
                    if bias:
                        s = s - ck_ref[g, pl.ds(rs, ts), :]
                    if masked:
                        kr = lax.broadcasted_iota(jnp.int32, (ts, t), 0) + h * ts
                        qc = lax.broadcasted_iota(jnp.int32, (ts, t), 1)
                        s = jnp.where(_visible(qc, kr, group), s, NEG_BIG)
                    subs.append(s)
            out = []
            for g in range(HP):
                m, l, acc = carry[g]
                for h in range(ATT_SUB_FWD):
                    s = subs[g * ATT_SUB_FWD + h]
                    m_new = jnp.maximum(m, jnp.max(s, axis=0, keepdims=True))
                    p = jnp.exp2(s - m_new)
                    a = jnp.exp2(m - m_new)
                    l = a * l + jnp.sum(p, axis=0, keepdims=True)
                    vh = vt_ref[g, j, :, h * ts:(h + 1) * ts]
                    acc = a * acc + jnp.dot(vh, p.astype(BF16), preferred_element_type=F32)
                    m = m_new
                out.append((m, l, acc))
            return tuple(out)

        init = (jnp.full((1, t), NEG_BIG, F32), jnp.zeros((1, t), F32), jnp.zeros((dv, t), F32))
        carry = lax.fori_loop(0, i, lambda j, cr: step(j, cr, False), (init,) * HP)
        carry = step(i, carry, True)
        for g in range(HP):
            m, l, acc = carry[g]
            o_ref[g] = (acc / l).astype(o_ref.dtype)
            lse_ref[g, 0] = m + jnp.log(l) * LOG2E

    in_specs = [pl.BlockSpec((HP, t, dk), lambda h, i: (h, i, 0)),
                pl.BlockSpec((HP, S, dk), lambda h, i: (h, 0, 0)),
                pl.BlockSpec((HP, n, dv, t), lambda h, i: (h, 0, 0, 0))]
    args = [q, k, v_t]
    if bias:
        in_specs.append(pl.BlockSpec((HP, S, 1), lambda h, i: (h, 0, 0)))
        args.append(cum2.reshape(H, S, 1))
    return pl.pallas_call(
        body, name=name, grid=(H // HP, n), in_specs=in_specs,
        out_specs=[pl.BlockSpec((HP, dv, t), lambda h, i: (h, 0, i)),
                   pl.BlockSpec((HP, 1, 1, t), lambda h, i: (h, i, 0, 0))],
        out_shape=[jax.ShapeDtypeStruct((H, dv, S), BF16), jax.ShapeDtypeStruct((H, n, 1, t), F32)],
        compiler_params=_params(("parallel", "arbitrary")),
    )(*args)


def _attn_bwd_q(q, q_t, k, v, do, do_t, lse, delta, scale, group, cum, name):
    H, S, dk = q.shape
    dv = v.shape[-1]
    t = _att_tile(S)
    ts = t // ATT_SUB
    n = S // t
    HP = ATT_HEADS_BWD
    bias = cum is not None

    def body(*refs):
        if bias:
            q_ref, qt_ref, k_ref, v_ref, do_ref, dot_ref, lse_ref, dl_ref, ck_ref, dq_ref, dkt_ref, dvt_ref = refs
        else:
            q_ref, qt_ref, k_ref, v_ref, do_ref, dot_ref, lse_ref, dl_ref, dq_ref, dkt_ref, dvt_ref = refs
        i = pl.program_id(1)

        @pl.when(i == 0)
        def _():
            dkt_ref[...] = jnp.zeros_like(dkt_ref)
            dvt_ref[...] = jnp.zeros_like(dvt_ref)

        def step(j, carry, masked):
            start = pl.multiple_of(j * t, t)
            subs = []
            for g in range(HP):
                qi = q_ref[g]
                doi = do_ref[g]
                for h in range(ATT_SUB):
                    rs = pl.multiple_of(start + h * ts, ts)
                    kh = k_ref[g, pl.ds(rs, ts), :]
                    vh = v_ref[g, pl.ds(rs, ts), :]
                    s = lax.dot_general(qi, kh, (((1,), (1,)), ((), ())), preferred_element_type=F32)
                    if bias:
                        s = s - ck_ref[g, pl.ds(j, 1), h * ts:(h + 1) * ts]
                    if masked:
                        qr = lax.broadcasted_iota(jnp.int32, (t, ts), 0)
                        kc = lax.broadcasted_iota(jnp.int32, (t, ts), 1) + h * ts
                        s = jnp.where(_visible(qr, kc, group), s, NEG_BIG)
                    dp = lax.dot_general(doi, vh, (((1,), (1,)), ((), ())), preferred_element_type=F32)
                    subs.append((kh, s, dp))
            out = []
            for g in range(HP):
                dq_acc = carry[g]
                for h in range(ATT_SUB):
                    kh, s, dp = subs[g * ATT_SUB + h]
                    lanes = slice(h * ts, (h + 1) * ts)
                    p = jnp.exp2(s - lse_ref[g])
                    ds_b = (p * (dp - dl_ref[g])).astype(BF16)
                    dvt_ref[g, j, :, lanes] += jnp.dot(dot_ref[g, 0], p.astype(BF16), preferred_element_type=F32)
                    dkt_ref[g, j, :, lanes] += jnp.dot(qt_ref[g, 0], ds_b, preferred_element_type=F32)
                    dq_acc = dq_acc + jnp.dot(ds_b, kh, preferred_element_type=F32)
                out.append(dq_acc)
            return tuple(out)

        carry = lax.fori_loop(0, i, lambda j, cr: step(j, cr, False), (jnp.zeros((t, dk), F32),) * HP)
        carry = step(i, carry, True)
        for g in range(HP):
            dq_ref[g] = carry[g] * scale

    row = lambda d: pl.BlockSpec((HP, t, d), lambda h, i: (h, i, 0))
    tile_t = lambda d: pl.BlockSpec((HP, 1, d, t), lambda h, i: (h, i, 0, 0))
    whole = lambda d: pl.BlockSpec((HP, S, d), lambda h, i: (h, 0, 0))
    tiles_out = lambda d: pl.BlockSpec((HP, n, d, t), lambda h, i: (h, 0, 0, 0))
    in_specs = [row(dk), tile_t(dk), whole(dk), whole(dv), row(dv), tile_t(dv), row(1), row(1)]
    args = [q, q_t, k, v, do, do_t, lse.reshape(H, S, 1), delta.reshape(H, S, 1)]
    out_specs = [row(dk), tiles_out(dk), tiles_out(dv)]
    out_shape = [jax.ShapeDtypeStruct((H, S, dk), F32), jax.ShapeDtypeStruct((H, n, dk, t), F32),
                 jax.ShapeDtypeStruct((H, n, dv, t), F32)]
    if bias:
        in_specs.append(pl.BlockSpec((HP, n, t), lambda h, i: (h, 0, 0)))
        args.append(cum.reshape(H, n, t))
    return pl.pallas_call(
        body, name=name, grid=(H // HP, n), in_specs=in_specs, out_specs=out_specs, out_shape=out_shape,
        compiler_params=_params(("parallel", "arbitrary")),
    )(*args)


def _attn_delta(o, do, name):
    H, S, dv = o.shape

    def fn(o, do):
        return (jnp.sum(o.astype(F32) * do.astype(F32), axis=1, keepdims=True),)

    d = _rowwise(name, fn, H * S, [_whole(o.reshape(H * S, dv)), _whole(do.reshape(H * S, dv))], [], [(1, F32)],
                 tr=1024)[0]
    return d.reshape(H, S)


def _fox_gate_fwd(f_t, b, name):
    Hh, S = f_t.shape
    nb = S // LANES

    def body(f_ref, b_ref, cum_ref):
        r = lax.broadcasted_iota(jnp.int32, (LANES, LANES), 0)
        c = lax.broadcasted_iota(jnp.int32, (LANES, LANES), 1)
        upper = (r <= c).astype(F32)
        carry = jnp.zeros((Hh, 1), F32)
        for blk in range(nb):
            z = f_ref[:, blk * LANES:(blk + 1) * LANES] + b_ref[...]
            logf = jnp.minimum(z, 0.0) - jnp.log(1.0 + jnp.exp(-jnp.abs(z)))
            cs = jnp.dot(logf, upper, preferred_element_type=F32, precision=lax.Precision.HIGHEST) + carry
            cum_ref[:, blk * LANES:(blk + 1) * LANES] = cs * LOG2E
            carry = cs[:, LANES - 1:LANES]

    return pl.pallas_call(body, name=name, out_shape=jax.ShapeDtypeStruct((Hh, S), F32),
                          compiler_params=pltpu.CompilerParams(vmem_limit_bytes=VMEM_LIMIT_BYTES))(f_t, b)


def _fox_gate_bwd(f_t, b, dcum_k, dcum_q, name):
    Hh, S = f_t.shape
    nb = S // LANES

    def body(f_ref, b_ref, dck_ref, dcq_ref, dz_ref, db_ref):
        r = lax.broadcasted_iota(jnp.int32, (LANES, LANES), 0)
        c = lax.broadcasted_iota(jnp.int32, (LANES, LANES), 1)
        lower = (r >= c).astype(F32)
        carry = jnp.zeros((Hh, 1), F32)
        db = jnp.zeros((Hh, 1), F32)
        for blk in range(nb - 1, -1, -1):
            sl = slice(blk * LANES, (blk + 1) * LANES)
            rc = jnp.dot(dck_ref[:, sl] + dcq_ref[:, sl], lower, preferred_element_type=F32,
                         precision=lax.Precision.HIGHEST) + carry
            carry = rc[:, 0:1]
            z = f_ref[:, sl] + b_ref[...]
            dz = rc * (1.0 - _sigmoid(z))
            dz_ref[:, sl] = dz
            db = db + jnp.sum(dz, axis=1, keepdims=True)
        db_ref[...] = db

    return pl.pallas_call(
        body, name=name,
        out_shape=[jax.ShapeDtypeStruct((Hh, S), F32), jax.ShapeDtypeStruct((Hh, 1), F32)],
        compiler_params=pltpu.CompilerParams(vmem_limit_bytes=VMEM_LIMIT_BYTES))(f_t, b, dcum_k, dcum_q)


def _conv_tile(S):
    return min(512, S // 2)


def _glu(cin):
    a = cin[:, :CONV_CHANNELS].astype(F32)
    b = cin[:, CONV_CHANNELS:].astype(F32)
    return a * _sigmoid(b)


def _conv_taps(ext_ref, w_ref, ts, first):
    acc = jnp.zeros((ts, CONV_CHANNELS), F32)
    for j in range(CONV_WIDTH):
        acc = acc + w_ref[j:j + 1, :] * ext_ref[first + j:first + j + ts, :]
    return acc


def _fill_u0_ext(ext_ref, cin_ref, halo_ref, i):
    ext_ref[0:CONV_HALO, :] = jnp.where(i > 0, _glu(halo_ref[...]), 0.0)
    ext_ref[CONV_HALO:, :] = _glu(cin_ref[...])


def _conv_specs(ts, cidx):
    per = ts // CONV_HALO
    wide = 2 * CONV_CHANNELS
    return [pl.BlockSpec((ts, wide), lambda i: (i, cidx)),
            pl.BlockSpec((CONV_HALO, wide), lambda i: (jnp.maximum(i * per - 1, 0), cidx))]


def _conv_fwd(proj, w, bias, ln_g, ln_b, name):
    S = proj.shape[0]
    ts = _conv_tile(S)
    C = CONV_CHANNELS

    def body(cin_ref, halo_ref, w_ref, b_ref, g_ref, bb_ref, o_ref, ext_ref):
        _fill_u0_ext(ext_ref, cin_ref, halo_ref, pl.program_id(0))
        u1 = _conv_taps(ext_ref, w_ref, ts, CONV_HALO - (CONV_WIDTH - 1)) + b_ref[...]
        mu = jnp.mean(u1, axis=1, keepdims=True)
        xc = u1 - mu
        rstd = lax.rsqrt(jnp.mean(xc * xc, axis=1, keepdims=True) + LN_EPS)
        u2 = xc * rstd * g_ref[...] + bb_ref[...]
        o_ref[...] = (u2 * _sigmoid(u2)).astype(o_ref.dtype)

    vec = pl.BlockSpec((1, C), lambda i: (0, 0))
    return pl.pallas_call(
        body, name=name, grid=(S // ts,),
        in_specs=_conv_specs(ts, COL_CONV // (2 * C)) + [pl.BlockSpec((32, C), lambda i: (0, 0)), vec, vec, vec],
        out_specs=pl.BlockSpec((ts, C), lambda i: (i, 0)),
        out_shape=jax.ShapeDtypeStruct((S, C), BF16),
        scratch_shapes=[pltpu.VMEM((ts + CONV_HALO, C), F32)],
        compiler_params=_params(("parallel",)),
    )(proj, proj, w, bias, ln_g, ln_b)


def _conv_bwd_a(proj, du3, w, bias, ln_g, ln_b, name):
    S = proj.shape[0]
    ts = _conv_tile(S)
    C = CONV_CHANNELS
    first = CONV_HALO - (CONV_WIDTH - 1)

    def body(cin_ref, halo_ref, du3_ref, w_ref, b_ref, g_ref, bb_ref, du1_ref, dw_ref, dbias_ref, dg_ref, dbb_ref,
             ext_ref):
        i = pl.program_id(0)

        @pl.when(i == 0)
        def _():
            dw_ref[...] = jnp.zeros_like(dw_ref)
            dbias_ref[...] = jnp.zeros_like(dbias_ref)
            dg_ref[...] = jnp.zeros_like(dg_ref)
            dbb_ref[...] = jnp.zeros_like(dbb_ref)

        _fill_u0_ext(ext_ref, cin_ref, halo_ref, i)
        u1 = _conv_taps(ext_ref, w_ref, ts, first) + b_ref[...]
        mu = jnp.mean(u1, axis=1, keepdims=True)
        xc = u1 - mu
        rstd = lax.rsqrt(jnp.mean(xc * xc, axis=1, keepdims=True) + LN_EPS)
        xh = xc * rstd
        u2 = xh * g_ref[...] + bb_ref[...]
        sg = _sigmoid(u2)
        du2 = du3_ref[...].astype(F32) * (sg * (1.0 + u2 * (1.0 - sg)))
        dg_ref[...] += _colsum(du2 * xh)
        dbb_ref[...] += _colsum(du2)
        dxh = du2 * g_ref[...]
        du1 = rstd * (dxh - jnp.mean(dxh, axis=1, keepdims=True) - xh * jnp.mean(dxh * xh, axis=1, keepdims=True))
        du1_ref[...] = du1
        dbias_ref[...] += _colsum(du1)
        for j in range(CONV_WIDTH):
            dw_ref[j:j + 1, :] += _colsum(du1 * ext_ref[first + j:first + j + ts, :])

    vec = pl.BlockSpec((1, C), lambda i: (0, 0))
    taps = pl.BlockSpec((32, C), lambda i: (0, 0))
    return pl.pallas_call(
        body, name=name, grid=(S // ts,),
        in_specs=_conv_specs(ts, COL_CONV // (2 * C)) + [pl.BlockSpec((ts, C), lambda i: (i, 0)), taps, vec, vec, vec],
        out_specs=[pl.BlockSpec((ts, C), lambda i: (i, 0)), taps, vec, vec, vec],
        out_shape=[jax.ShapeDtypeStruct((S, C), F32), jax.ShapeDtypeStruct((32, C), F32)]
        + [jax.ShapeDtypeStruct((1, C), F32)] * 3,
        scratch_shapes=[pltpu.VMEM((ts + CONV_HALO, C), F32)],
        compiler_params=_params(("arbitrary",)),
    )(proj, proj, du3, w, bias, ln_g, ln_b)


def _conv_bwd_b(proj, du1, w, name):
    S = proj.shape[0]
    ts = _conv_tile(S)
    C = CONV_CHANNELS
    per = ts // CONV_HALO
    nblk = S // ts
    last_halo = S // CONV_HALO - 1

    def body(cin_ref, du1_ref, nxt_ref, w_ref, o_ref, ext_ref):
        i = pl.program_id(0)
        ext_ref[0:ts, :] = du1_ref[...]
        ext_ref[ts:, :] = jnp.where(i < nblk - 1, nxt_ref[...], 0.0)
        du0 = jnp.zeros((ts, C), F32)
        for j in range(CONV_WIDTH):
            off = CONV_WIDTH - 1 - j
            du0 = du0 + w_ref[j:j + 1, :] * ext_ref[off:off + ts, :]
        a = cin_ref[:, :C].astype(F32)
        sg = _sigmoid(cin_ref[:, C:].astype(F32))
        o_ref[:, :C] = (du0 * sg).astype(o_ref.dtype)
        o_ref[:, C:] = (du0 * a * sg * (1.0 - sg)).astype(o_ref.dtype)

    return pl.pallas_call(
        body, name=name, grid=(nblk,),
        in_specs=[pl.BlockSpec((ts, 2 * C), lambda i: (i, COL_CONV // (2 * C))),
                  pl.BlockSpec((ts, C), lambda i: (i, 0)),
                  pl.BlockSpec((CONV_HALO, C), lambda i: (jnp.minimum((i + 1) * per, last_halo), 0)),
                  pl.BlockSpec((32, C), lambda i: (0, 0))],
        out_specs=pl.BlockSpec((ts, 2 * C), lambda i: (i, 0)),
        out_shape=jax.ShapeDtypeStruct((S, 2 * C), BF16),
        scratch_shapes=[pltpu.VMEM((ts + CONV_HALO, C), F32)],
        compiler_params=_params(("parallel",)),
    )(proj, du1, du1, w)


def _mesh_pos():
    return lax.axis_index("x"), lax.axis_index("y"), lax.axis_index("c")


def _exchange(x, gather, name):
    R = x.shape[-2]

    def body(x_ref, out_ref, send_sems, recv_sems, local_sem):
        mx, my, mc = _mesh_pos()
        me = 4 * mx + 2 * my + mc

        def src(dst_dev):
            return x_ref if gather else x_ref.at[dst_dev]

        local = pltpu.make_async_copy(src(me), out_ref.at[me], local_sem)
        local.start()
        copies = []
        for k in range(1, N_DEV):
            px, py, pc = mx ^ (k >> 2), my ^ ((k >> 1) & 1), mc ^ (k & 1)
            peer = 4 * px + 2 * py + pc
            cp = pltpu.make_async_remote_copy(
                src_ref=src(peer), dst_ref=out_ref.at[me], send_sem=send_sems.at[k - 1], recv_sem=recv_sems.at[k - 1],
                device_id=(px, py, pc), device_id_type=pl.DeviceIdType.MESH)
            cp.start()
            copies.append(cp)
        for cp in copies:
            cp.wait_recv()
        for cp in copies:
            cp.wait_send()
        local.wait()

    return pl.pallas_call(
        body, name=name,
        in_specs=[pl.BlockSpec(memory_space=pl.ANY)], out_specs=pl.BlockSpec(memory_space=pl.ANY),
        out_shape=jax.ShapeDtypeStruct((N_DEV, R, LANES), x.dtype),
        scratch_shapes=[pltpu.SemaphoreType.DMA((N_DEV - 1,)), pltpu.SemaphoreType.DMA((N_DEV - 1,)),
                        pltpu.SemaphoreType.DMA(())],
    )(x)


def _gather_two_level(x, name):
    R = x.shape[0]

    def body(x_ref, out_ref, send_sems, recv_sems, local_sem):
        mx, my, mc = _mesh_pos()
        me, sibling = (mx, my, mc), (mx, my, 1 - mc)
        chips = [(1 - mx, my), (mx, 1 - my), (1 - mx, 1 - my)]

        def slot(px, py, pc):
            return out_ref.at[4 * px + 2 * py + pc]

        def copy(k, block, to, src=None):
            return pltpu.make_async_remote_copy(
                src_ref=slot(*block) if src is None else src, dst_ref=slot(*block), send_sem=send_sems.at[k],
                recv_sem=recv_sems.at[k], device_id=to, device_id_type=pl.DeviceIdType.MESH)

        mine = pltpu.make_async_copy(x_ref, slot(*me), local_sem)
        mine.start()
        first = [copy(0, me, sibling, src=x_ref)]
        first += [copy(1 + j, me, (*chip, mc), src=x_ref) for j, chip in enumerate(chips)]
        for cp in first:
            cp.start()
        passed = [copy(4 + j, (*chip, mc), sibling) for j, chip in enumerate(chips)]
        for j, chip in enumerate(chips):
            copy(1 + j, (*chip, mc), me).wait_recv()
            passed[j].start()
        copy(0, sibling, me).wait_recv()
        for j, chip in enumerate(chips):
            copy(4 + j, (*chip, 1 - mc), me).wait_recv()
        for cp in first + passed:
            cp.wait_send()
        mine.wait()

    return pl.pallas_call(
        body, name=name,
        in_specs=[pl.BlockSpec(memory_space=pl.ANY)], out_specs=pl.BlockSpec(memory_space=pl.ANY),
        out_shape=jax.ShapeDtypeStruct((N_DEV, R, LANES), x.dtype),
        scratch_shapes=[pltpu.SemaphoreType.DMA((N_DEV - 1,)), pltpu.SemaphoreType.DMA((N_DEV - 1,)),
                        pltpu.SemaphoreType.DMA(())],
    )(x)


SEGMENT_ROWS = 16


def _seg_rows(shape):
    n = int(np.prod(shape))
    return -(-n // (SEGMENT_ROWS * LANES)) * SEGMENT_ROWS


def _flat_total_rows(shapes, multiple):
    rows = sum(_seg_rows(s) for s in shapes)
    return -(-rows // multiple) * multiple


def _to_rows(a, lead=()):
    shape = a.shape[len(lead):]
    n, rows = int(np.prod(shape)), _seg_rows(shape)
    if n == rows * LANES:
        return a.reshape(lead + (rows, LANES))
    flat = a.reshape(lead + (n,))
    flat = jnp.pad(flat, [(0, 0)] * len(lead) + [(0, rows * LANES - n)])
    return flat.reshape(lead + (rows, LANES))


def _from_rows(seg, shape, lead=()):
    n, rows = int(np.prod(shape)), _seg_rows(shape)
    if n == rows * LANES:
        return seg.reshape(lead + tuple(shape))
    return seg.reshape(lead + (rows * LANES,))[..., :n].reshape(lead + tuple(shape))


def _pack_flat(arrs, multiple, lead=()):
    parts = [_to_rows(a, lead) for a in arrs]
    rows = sum(p.shape[-2] for p in parts)
    total = -(-rows // multiple) * multiple
    if total != rows:
        parts.append(jnp.zeros(lead + (total - rows, LANES), parts[0].dtype))
    return jnp.concatenate(parts, axis=len(lead))


def _unpack_flat(flat, shapes, lead=()):
    out, r0 = [], 0
    for s in shapes:
        rows = _seg_rows(s)
        out.append(_from_rows(flat[..., r0:r0 + rows, :], s, lead))
        r0 += rows
    return out


def _adamw_sum(name, g_slabs, w, m, v, tr):
    R = w.shape[0]
    c1 = 1.0 - ADAM_B1 ** ADAM_STEP
    c2 = 1.0 - ADAM_B2 ** ADAM_STEP

    def fn(*a):
        g = a[0].astype(F32)
        for d in range(1, N_DEV):
            g = g + a[d].astype(F32)
        w, m, v = a[N_DEV:]
        m_new = ADAM_B1 * m + (1.0 - ADAM_B1) * g
        v_new = ADAM_B2 * v + (1.0 - ADAM_B2) * (g * g)
        delta = -ADAM_LR * ((m_new / c1) / (jnp.sqrt(v_new / c2) + ADAM_EPS) + ADAM_WD * w)
        return g, delta, m_new, v_new

    tr = min(tr, R)
    ins = [(g_slabs, LANES, 0, d * (R // tr)) for d in range(N_DEV)] + [_whole(w), _whole(m), _whole(v)]
    return _rowwise(name, fn, R, ins, [], [(LANES, F32)] * 4, tr=tr)


def _full_weights(gathered, local_shapes):
    segs = _unpack_flat(gathered, local_shapes, lead=(N_DEV,))
    out = {}
    for (name, kind), shp, seg in zip(SHARDED, local_shapes, segs):
        L, a, b = shp
        if kind == 'col':
            out[name] = seg.transpose(1, 2, 0, 3).reshape(L, a, N_DEV * b)
        else:
            out[name] = seg.transpose(1, 0, 2, 3).reshape(L, N_DEV * a, b)
    return out


def _pack_grads(grads, local_shapes):
    parts = []
    for (name, kind), shp in zip(SHARDED, local_shapes):
        L, a, b = shp
        g = grads[name].astype(BF16)
        if kind == 'col':
            parts.append(g.reshape(L, a, N_DEV, b).transpose(2, 0, 1, 3))
        else:
            parts.append(g.reshape(L, N_DEV, a, b).transpose(1, 0, 2, 3))
    return _pack_flat(parts, FLAT_ROW_MULTIPLE, lead=(N_DEV,))


def _rearrange_w_in(w):
    z = lambda n: jnp.zeros((w.shape[0], n), w.dtype)
    return jnp.concatenate([
        w[:, O_GATE:O_END], w[:, O_CONV:O_GATE], w[:, O_QB:O_KB], w[:, O_KB:O_VB], w[:, O_VB:O_F],
        w[:, O_CQ:O_CKV], z(CQ_PAD - Q_LORA), w[:, O_CKV:O_KR], w[:, O_KR:O_QB], w[:, O_F:O_CONV],
        z(LANES - MLA_ROPE - FOX_HEADS), z(IN_COLS - COL_SMALL - LANES)], axis=1)


def _restore_w_in(g):
    return jnp.concatenate([
        g[:, COL_CQ:COL_CQ + Q_LORA], g[:, COL_CKV:COL_CKV + KV_LORA], g[:, COL_SMALL:COL_SMALL + MLA_ROPE],
        g[:, COL_QB:COL_KB], g[:, COL_KB:COL_VB], g[:, COL_VB:COL_CQ],
        g[:, COL_SMALL + MLA_ROPE:COL_SMALL + MLA_ROPE + FOX_HEADS], g[:, COL_CONV:COL_QB], g[:, COL_GATE:COL_CONV]],
        axis=1)


def _rearrange_w_uq(w):
    w3 = w.reshape(Q_LORA, MLA_HEADS, MLA_NOPE + MLA_ROPE)
    cols = jnp.concatenate([w3[:, :, :MLA_NOPE].reshape(Q_LORA, -1),
                            w3[:, :, MLA_NOPE:MLA_NOPE + ROPE_HALF].reshape(Q_LORA, -1),
                            w3[:, :, MLA_NOPE + ROPE_HALF:].reshape(Q_LORA, -1)], axis=1)
    return jnp.pad(cols, ((0, CQ_PAD - Q_LORA), (0, 0)))


def _restore_w_uq(g):
    g = g[:Q_LORA]
    n = MLA_HEADS * MLA_NOPE
    h = MLA_HEADS * ROPE_HALF
    parts = [g[:, :n].reshape(Q_LORA, MLA_HEADS, MLA_NOPE), g[:, n:n + h].reshape(Q_LORA, MLA_HEADS, ROPE_HALF),
             g[:, n + h:].reshape(Q_LORA, MLA_HEADS, ROPE_HALF)]
    return jnp.concatenate(parts, axis=2).reshape(Q_LORA, -1)


def _rearrange_w_ukv(w):
    w3 = w.reshape(KV_LORA, MLA_HEADS, MLA_NOPE + MLA_V)
    return jnp.concatenate([w3[:, :, :MLA_NOPE].reshape(KV_LORA, -1), w3[:, :, MLA_NOPE:].reshape(KV_LORA, -1)], axis=1)


def _restore_w_ukv(g):
    n = MLA_HEADS * MLA_NOPE
    parts = [g[:, :n].reshape(KV_LORA, MLA_HEADS, MLA_NOPE), g[:, n:].reshape(KV_LORA, MLA_HEADS, MLA_V)]
    return jnp.concatenate(parts, axis=2).reshape(KV_LORA, -1)


def _heads(a, H):
    S = a.shape[0]
    return a.reshape(S, H, -1).transpose(1, 0, 2)


def _tiles_t(a, H):
    S = a.shape[0]
    t = _att_tile(S)
    return a.reshape(S // t, t, H, -1).transpose(2, 0, 3, 1)


def _untiles_t(a):
    H, n, d, t = a.shape
    return a.transpose(1, 3, 0, 2).reshape(n * t, H * d)


def _unheads(a):
    H, S, d = a.shape
    return a.transpose(1, 0, 2).reshape(S, H * d)


def _rope_q(x_src, cos, sin, name):
    def fn(x1, x2, c, s):
        return x1 * c - x2 * s, x2 * c + x1 * s

    S = x_src.shape[0]
    return _rowwise(name, fn, S, [(x_src, LANES, 4, 0), (x_src, LANES, 5, 0), _whole(cos), _whole(sin)], [],
                    [(LANES, F32), (LANES, F32)], tr=512)


def _rope_k(x_in, cos_k, sin_k, fold_heads, name):
    def fn(x, c, s):
        if fold_heads:
            x = x[:, :LANES] + x[:, LANES:]
            x = x + pltpu.roll(x, 64, 1)
            x = x + pltpu.roll(x, 32, 1)
        lane = lax.broadcasted_iota(jnp.int32, x.shape, 1)
        partner = jnp.where(lane < ROPE_HALF, pltpu.roll(x, LANES - ROPE_HALF, 1), pltpu.roll(x, ROPE_HALF, 1))
        return (x * c + partner * s,)

    S = x_in[0].shape[0]
    return _rowwise(name, fn, S, [x_in, _whole(cos_k), _whole(sin_k)], [], [(LANES, F32)], tr=512)[0]


def _layer_forward(x, W, T, l):
    S = x.shape[0]
    nm = lambda s: f"{s}_l{l}"
    h1 = _rms_fwd(_whole(x), W['norm_mix_g'], D_MODEL, nm("rms_mix"))
    proj = _mm(h1, W['w_in'], name=nm("mm_in"))
    small = proj[:, COL_SMALL:COL_SMALL + LANES]

    cqn = _rms_fwd((proj, CQ_PAD, COL_CQ // CQ_PAD, 0), W['q_norm_g'], Q_LORA, nm("rms_q"))
    ckvn = _rms_fwd((proj, KV_LORA, COL_CKV // KV_LORA, 0), W['kv_norm_g'], KV_LORA, nm("rms_kv"))
    qa = _mm(cqn, W['w_uq'], name=nm("mm_uq"))
    kv = _mm(ckvn, W['w_ukv'], out_dtype=BF16, name=nm("mm_ukv"))
    q_r1, q_r2 = _rope_q(qa, T['cos_q'], T['sin_q'], nm("rope_q"))
    k_rope = _rope_k(_whole(small), T['cos_k'], T['sin_k'], False, nm("rope_k"))[:, :MLA_ROPE]
    n_nope = MLA_HEADS * MLA_NOPE
    mla_scale = (MLA_NOPE + MLA_ROPE) ** -0.5
    q_mla_s = (jnp.concatenate([qa[:, :n_nope].reshape(S, MLA_HEADS, MLA_NOPE), q_r1.reshape(S, MLA_HEADS, ROPE_HALF),
                                q_r2.reshape(S, MLA_HEADS, ROPE_HALF)], axis=2) * (mla_scale * LOG2E)).astype(BF16)
    q_mla = q_mla_s.transpose(1, 0, 2)
    q_mla_t = _tiles_t(q_mla_s.reshape(S, -1), MLA_HEADS)
    k_mla = jnp.concatenate([kv[:, :n_nope].reshape(S, MLA_HEADS, MLA_NOPE),
                             jnp.broadcast_to(k_rope.astype(BF16)[:, None, :], (S, MLA_HEADS, MLA_ROPE))],
                            axis=2).transpose(1, 0, 2)
    v_mla = _heads(kv[:, n_nope:], MLA_HEADS)
    o_mla_t, lse_a = _attn_fwd(q_mla, k_mla, _tiles_t(kv[:, n_nope:], MLA_HEADS), CHUNK, None, nm("mla_fwd"))
    o_mla = o_mla_t.transpose(0, 2, 1)
    oa_cat = o_mla_t.transpose(2, 0, 1).reshape(S, MLA_HEADS * MLA_V)
    o_a = _mm(oa_cat, W['w_bo_a'], name=nm("mm_bo_a"))

    f_t = small[:, MLA_ROPE:MLA_ROPE + FOX_HEADS].T
    cum = _fox_gate_fwd(f_t, W['b_forget'], nm("fox_gate"))
    fox_scale = FOX_HEAD_DIM ** -0.5
    qb = (proj[:, COL_QB:COL_KB] * (fox_scale * LOG2E)).astype(BF16).reshape(S, FOX_HEADS, FOX_HEAD_DIM)
    kb = proj[:, COL_KB:COL_VB].astype(BF16).reshape(S, FOX_HEADS, FOX_HEAD_DIM)
    one = jnp.ones((S, FOX_HEADS, 1), BF16)
    zero = lambda w: jnp.zeros((S, FOX_HEADS, w), BF16)
    extra = FOX_FEATURES - FOX_HEAD_DIM
    q_fox = jnp.concatenate([qb, zero(extra)], axis=2).transpose(1, 0, 2)
    k_fox = jnp.concatenate([kb, one, zero(extra - 1)], axis=2).transpose(1, 0, 2)
    q_fox_t = _tiles_t(jnp.concatenate([qb, one, zero(extra - 1)], axis=2).reshape(S, -1), FOX_HEADS)
    vb = proj[:, COL_VB:COL_CQ].astype(BF16)
    v_fox = _heads(vb, FOX_HEADS)
    o_fox_t, lse_b = _attn_fwd(q_fox, k_fox, _tiles_t(vb, FOX_HEADS), 1, cum, nm("fox_fwd"))
    o_fox = o_fox_t.transpose(0, 2, 1)
    ob_cat = o_fox_t.transpose(2, 0, 1).reshape(S, FOX_HEADS * FOX_HEAD_DIM)
    o_b = _mm(ob_cat, W['w_bo_b'], name=nm("mm_bo_b"))

    u3 = _conv_fwd(proj, W['dw_kernel'], W['dw_bias'], W['conv_ln_g'], W['conv_ln_b'], nm("conv_fwd"))
    o_c = _mm(u3, W['w_bo_c'], name=nm("mm_bo_c"))

    def gate_fn(la, lb, lc, oa, ob, oc, bg):
        ga = _sigmoid(la + bg[:, :D_MODEL])
        gb = _sigmoid(lb + bg[:, D_MODEL:2 * D_MODEL])
        gc = _sigmoid(lc + bg[:, 2 * D_MODEL:])
        return (ga * oa + gb * ob + gc * oc,)

    logit_ins = [(proj, D_MODEL, COL_GATE // D_MODEL + b, 0) for b in range(3)]
    y = _rowwise(nm("gate_fwd"), gate_fn, S, logit_ins + [_whole(o_a), _whole(o_b), _whole(o_c)], [W['b_gate']],
                 [(D_MODEL, BF16)])[0]
    x2 = _mm(y, W['w_out'], add=x, name=nm("mm_out"))

    h2 = _rms_fwd(_whole(x2), W['norm_ffn_g'], D_MODEL, nm("rms_ffn"))
    gu = _mm(h2, W['w_gu'], name=nm("mm_gu"))

    def swiglu_fn(gt, up):
        return (gt * _sigmoid(gt) * up,)

    ff = _rowwise(nm("swiglu_fwd"), swiglu_fn, S, [(gu, FFN_HIDDEN, 0, 0), (gu, FFN_HIDDEN, 1, 0)], [],
                  [(FFN_HIDDEN, BF16)])[0]
    x3 = _mm(ff, W['w_ffn_down'], add=x2, name=nm("mm_down"))

    saved = dict(x=x, h1=h1, proj=proj, small=small, cqn=cqn, ckvn=ckvn, q_mla=q_mla, k_mla=k_mla, v_mla=v_mla,
                 q_mla_t=q_mla_t, q_fox_t=q_fox_t,
                 o_mla=o_mla, lse_a=lse_a, oa_cat=oa_cat, o_a=o_a, f_t=f_t, cum=cum, q_fox=q_fox, k_fox=k_fox,
                 v_fox=v_fox, o_fox=o_fox, lse_b=lse_b, ob_cat=ob_cat, o_b=o_b, u3=u3, o_c=o_c, y=y, x2=x2, h2=h2,
                 gu=gu, ff=ff)
    return x3, saved


def _layer_backward(dx3, dx3_b, sv, W, T, l):
    S = dx3.shape[0]
    nm = lambda s: f"{s}_l{l}"
    G = {}

    G['w_ffn_down'] = _mm(sv['ff'], dx3_b, mode="tn", name=nm("mm_down_dw"))
    dff = _mm(dx3_b, W['w_ffn_down'], mode="nt", name=nm("mm_down_dx"))

    def swiglu_bwd_fn(gt, up, d):
        sg = _sigmoid(gt)
        return (jnp.concatenate([d * up * (sg * (1.0 + gt * (1.0 - sg))), d * (gt * sg)], axis=1),)

    dgu = _rowwise(nm("swiglu_bwd"), swiglu_bwd_fn, S,
                   [(sv['gu'], FFN_HIDDEN, 0, 0), (sv['gu'], FFN_HIDDEN, 1, 0), _whole(dff)], [],
                   [(2 * FFN_HIDDEN, BF16)])[0]
    G['w_gu'] = _mm(sv['h2'], dgu, mode="tn", name=nm("mm_gu_dw"))
    dh2 = _mm(dgu, W['w_gu'], mode="nt", name=nm("mm_gu_dx"))
    dx2, dx2_b, G['norm_ffn_g'] = _rms_bwd(_whole(sv['x2']), dh2, W['norm_ffn_g'], D_MODEL, dx3, nm("rms_ffn_bwd"))

    G['w_out'] = _mm(sv['y'], dx2_b, mode="tn", name=nm("mm_out_dw"))
    dy = _mm(dx2_b, W['w_out'], mode="nt", name=nm("mm_out_dx"))

    def gate_bwd_fn(la, lb, lc, oa, ob, oc, dy, bg):
        outs, dls = [], []
        for k, (lg, o) in enumerate(((la, oa), (lb, ob), (lc, oc))):
            g = _sigmoid(lg + bg[:, k * D_MODEL:(k + 1) * D_MODEL])
            outs.append(dy * g)
            dls.append(dy * o * g * (1.0 - g))
        dl = jnp.concatenate(dls, axis=1)
        return (*outs, dl, _colsum(dl))

    proj = sv['proj']
    logit_ins = [(proj, D_MODEL, COL_GATE // D_MODEL + b, 0) for b in range(3)]
    do_a, do_b, do_c, dlogit, G['b_gate'] = _rowwise(
        nm("gate_bwd"), gate_bwd_fn, S, logit_ins + [_whole(sv['o_a']), _whole(sv['o_b']), _whole(sv['o_c']), _whole(dy)],
        [W['b_gate']], [(D_MODEL, BF16)] * 3 + [(3 * D_MODEL, BF16)], [(1, 3 * D_MODEL)], tr=128)

    G['w_bo_c'] = _mm(sv['u3'], do_c, mode="tn", name=nm("mm_bo_c_dw"))
    du3 = _mm(do_c, W['w_bo_c'], mode="nt", name=nm("mm_bo_c_dx"))
    du1, G['dw_kernel'], G['dw_bias'], G['conv_ln_g'], G['conv_ln_b'] = _conv_bwd_a(
        proj, du3, W['dw_kernel'], W['dw_bias'], W['conv_ln_g'], W['conv_ln_b'], nm("conv_bwd_a"))
    dconv = _conv_bwd_b(proj, du1, W['dw_kernel'], nm("conv_bwd_b"))

    G['w_bo_b'] = _mm(sv['ob_cat'], do_b, mode="tn", name=nm("mm_bo_b_dw"))
    dob_cat = _mm(do_b, W['w_bo_b'], mode="nt", out_dtype=BF16, name=nm("mm_bo_b_dx"))
    dob = _heads(dob_cat, FOX_HEADS)
    delta_b = _attn_delta(sv['o_fox'], dob, nm("fox_delta"))
    fox_scale = FOX_HEAD_DIM ** -0.5
    dq_fx, dk_fxt, dv_ft = _attn_bwd_q(
        sv['q_fox'], sv['q_fox_t'], sv['k_fox'], sv['v_fox'], dob, _tiles_t(dob_cat, FOX_HEADS),
        sv['lse_b'].reshape(FOX_HEADS, S), delta_b, fox_scale, 1, sv['cum'], nm("fox_bwd"))
    dq_f = dq_fx[:, :, :FOX_HEAD_DIM]
    dk_ft = dk_fxt[:, :, :FOX_HEAD_DIM, :]
    dcum_q = dq_fx[:, :, FOX_HEAD_DIM] * (1.0 / fox_scale)
    dcum_k = -dk_fxt[:, :, FOX_HEAD_DIM, :].reshape(FOX_HEADS, S)
    dz, G['b_forget'] = _fox_gate_bwd(sv['f_t'], W['b_forget'], dcum_k, dcum_q, nm("fox_gate_bwd"))

    G['w_bo_a'] = _mm(sv['oa_cat'], do_a, mode="tn", name=nm("mm_bo_a_dw"))
    doa_cat = _mm(do_a, W['w_bo_a'], mode="nt", out_dtype=BF16, name=nm("mm_bo_a_dx"))
    doa = _heads(doa_cat, MLA_HEADS)
    delta_a = _attn_delta(sv['o_mla'], doa, nm("mla_delta"))
    dq_m, dk_mt, dv_mt = _attn_bwd_q(sv['q_mla'], sv['q_mla_t'], sv['k_mla'], sv['v_mla'], doa,
                                     _tiles_t(doa_cat, MLA_HEADS), sv['lse_a'].reshape(MLA_HEADS, S), delta_a,
                                     (MLA_NOPE + MLA_ROPE) ** -0.5, CHUNK, None, nm("mla_bwd"))
    dq_s = dq_m.transpose(1, 0, 2)
    dqr = jnp.concatenate([dq_s[:, :, MLA_NOPE:MLA_NOPE + ROPE_HALF].reshape(S, -1),
                           dq_s[:, :, MLA_NOPE + ROPE_HALF:].reshape(S, -1)], axis=1)

    def rope_q_bwd_fn(d1, d2, c, s):
        return d1 * c + d2 * s, d2 * c - d1 * s

    dq_r1, dq_r2 = _rowwise(nm("rope_q_bwd"), rope_q_bwd_fn, S,
                            [(dqr, LANES, 0, 0), (dqr, LANES, 1, 0), _whole(T['cos_q']), _whole(T['sin_q'])], [],
                            [(LANES, BF16), (LANES, BF16)], tr=512)
    dqa = jnp.concatenate([dq_s[:, :, :MLA_NOPE].reshape(S, -1).astype(BF16), dq_r1, dq_r2], axis=1)
    G['w_uq'] = _mm(sv['cqn'], dqa, mode="tn", name=nm("mm_uq_dw"))
    dcqn = _mm(dqa, W['w_uq'], mode="nt", name=nm("mm_uq_dx"))
    dcq, G['q_norm_g'] = _rms_bwd((proj, CQ_PAD, COL_CQ // CQ_PAD, 0), dcqn, W['q_norm_g'], Q_LORA, None,
                                  nm("rms_q_bwd"))
    dk_s = (_untiles_t(dk_mt) * LN2).reshape(S, MLA_HEADS, MLA_NOPE + MLA_ROPE)
    dkv = jnp.concatenate([dk_s[:, :, :MLA_NOPE].reshape(S, -1), _untiles_t(dv_mt)], axis=1).astype(BF16)
    G['w_ukv'] = _mm(sv['ckvn'], dkv, mode="tn", name=nm("mm_ukv_dw"))
    dckvn = _mm(dkv, W['w_ukv'], mode="nt", name=nm("mm_ukv_dx"))
    dckv, G['kv_norm_g'] = _rms_bwd((proj, KV_LORA, COL_CKV // KV_LORA, 0), dckvn, W['kv_norm_g'], KV_LORA, None,
                                    nm("rms_kv_bwd"))
    dk_rope_heads = dk_s[:, :, MLA_NOPE:].reshape(S, MLA_HEADS * MLA_ROPE)
    dkr = _rope_k(_whole(dk_rope_heads), T['cos_k'], T['sin_k_neg'], True, nm("rope_k_bwd"))

    dsmall = jnp.concatenate([dkr[:, :MLA_ROPE], dz.T, jnp.zeros((S, LANES - MLA_ROPE - FOX_HEADS), F32)], axis=1)
    dproj = jnp.concatenate([
        dlogit, dconv, _unheads(dq_f).astype(BF16), (_untiles_t(dk_ft) * LN2).astype(BF16),
        _untiles_t(dv_ft).astype(BF16),
        dcq.astype(BF16), dckv.astype(BF16), dsmall.astype(BF16),
        jnp.zeros((S, IN_COLS - COL_SMALL - LANES), BF16)], axis=1)
    G['w_in'] = _mm(sv['h1'], dproj, mode="tn", name=nm("mm_in_dw"))
    dh1 = _mm(dproj, W['w_in'], mode="nt", name=nm("mm_in_dx"))
    dx, dx_b, G['norm_mix_g'] = _rms_bwd(_whole(sv['x']), dh1, W['norm_mix_g'], D_MODEL, dx2, nm("rms_mix_bwd"))
    return dx, dx_b, G


def _loss_head(x, target, g, name):
    def fn(x, t, g):
        r = lax.rsqrt(jnp.mean(x * x, axis=1, keepdims=True) + RMS_EPS)
        xh = x * r
        e = xh * g - t
        part = 0.5 * jnp.sum(jnp.mean(e * e, axis=1, keepdims=True), axis=0, keepdims=True)
        dy = e * (1.0 / D_MODEL)
        dxh = dy * g
        dx = r * (dxh - xh * jnp.mean(dxh * xh, axis=1, keepdims=True))
        return dx, dx, jnp.broadcast_to(part, (1, LANES)), _colsum(dy * xh)

    S = x.shape[0]
    dx, dx_b, part, dg = _rowwise(name, fn, S, [_whole(x), _whole(target)], [g], [(D_MODEL, F32), (D_MODEL, BF16)],
                                  [(1, LANES), (1, D_MODEL)])
    return part[0, 0], dx, dx_b, dg


def kernel(x, positions, norm_mix_g, w_in, b_gate, q_norm_g, w_uq, kv_norm_g, w_ukv, b_forget, dw_kernel, dw_bias, conv_ln_g, conv_ln_b, w_bo_a, w_bo_b, w_bo_c, w_out, norm_ffn_g, w_ffn_gate, w_ffn_up, w_ffn_down, final_norm_g, loss_target, m_norm_mix_g, m_w_in, m_b_gate, m_q_norm_g, m_w_uq, m_kv_norm_g, m_w_ukv, m_b_forget, m_dw_kernel, m_dw_bias, m_conv_ln_g, m_conv_ln_b, m_w_bo_a, m_w_bo_b, m_w_bo_c, m_w_out, m_norm_ffn_g, m_w_ffn_gate, m_w_ffn_up, m_w_ffn_down, m_final_norm_g, v_norm_mix_g, v_w_in, v_b_gate, v_q_norm_g, v_w_uq, v_kv_norm_g, v_w_ukv, v_b_forget, v_dw_kernel, v_dw_bias, v_conv_ln_g, v_conv_ln_b, v_w_bo_a, v_w_bo_b, v_w_bo_c, v_w_out, v_norm_ffn_g, v_w_ffn_gate, v_w_ffn_up, v_w_ffn_down, v_final_norm_g):
    local = dict(norm_mix_g=norm_mix_g, w_in=w_in, b_gate=b_gate, q_norm_g=q_norm_g, w_uq=w_uq, kv_norm_g=kv_norm_g,
                 w_ukv=w_ukv, b_forget=b_forget, dw_kernel=dw_kernel, dw_bias=dw_bias, conv_ln_g=conv_ln_g,
                 conv_ln_b=conv_ln_b, w_bo_a=w_bo_a, w_bo_b=w_bo_b, w_bo_c=w_bo_c, w_out=w_out, norm_ffn_g=norm_ffn_g,
                 w_ffn_gate=w_ffn_gate, w_ffn_up=w_ffn_up, w_ffn_down=w_ffn_down, final_norm_g=final_norm_g)
    mom_m = dict(norm_mix_g=m_norm_mix_g, w_in=m_w_in, b_gate=m_b_gate, q_norm_g=m_q_norm_g, w_uq=m_w_uq,
                 kv_norm_g=m_kv_norm_g, w_ukv=m_w_ukv, b_forget=m_b_forget, dw_kernel=m_dw_kernel, dw_bias=m_dw_bias,
                 conv_ln_g=m_conv_ln_g, conv_ln_b=m_conv_ln_b, w_bo_a=m_w_bo_a, w_bo_b=m_w_bo_b, w_bo_c=m_w_bo_c,
                 w_out=m_w_out, norm_ffn_g=m_norm_ffn_g, w_ffn_gate=m_w_ffn_gate, w_ffn_up=m_w_ffn_up,
                 w_ffn_down=m_w_ffn_down, final_norm_g=m_final_norm_g)
    mom_v = dict(norm_mix_g=v_norm_mix_g, w_in=v_w_in, b_gate=v_b_gate, q_norm_g=v_q_norm_g, w_uq=v_w_uq,
                 kv_norm_g=v_kv_norm_g, w_ukv=v_w_ukv, b_forget=v_b_forget, dw_kernel=v_dw_kernel, dw_bias=v_dw_bias,
                 conv_ln_g=v_conv_ln_g, conv_ln_b=v_conv_ln_b, w_bo_a=v_w_bo_a, w_bo_b=v_w_bo_b, w_bo_c=v_w_bo_c,
                 w_out=v_w_out, norm_ffn_g=v_norm_ffn_g, w_ffn_gate=v_w_ffn_gate, w_ffn_up=v_w_ffn_up,
                 w_ffn_down=v_w_ffn_down, final_norm_g=v_final_norm_g)
    S = x.shape[1]
    xs = x[0]
    sh_names = [n for n, _ in SHARDED]
    sh_shapes = [local[n].shape for n in sh_names]
    rep_shapes = [local[n].shape for n in REPLICATED]

    w_flat = _pack_flat([local[n] for n in sh_names], FLAT_ROW_MULTIPLE)
    gathered = _gather_two_level(w_flat.astype(BF16), "gather_weights")
    full = _full_weights(gathered, sh_shapes)

    def layer_weights(l):
        W = {n: full[n][l] for n in ('w_bo_a', 'w_bo_b', 'w_bo_c', 'w_out', 'w_ffn_down')}
        W['w_in'] = _rearrange_w_in(full['w_in'][l])
        W['w_uq'] = _rearrange_w_uq(full['w_uq'][l])
        W['w_ukv'] = _rearrange_w_ukv(full['w_ukv'][l])
        W['w_gu'] = jnp.concatenate([full['w_ffn_gate'][l], full['w_ffn_up'][l]], axis=1)
        W['dw_kernel'] = jnp.pad(full['dw_kernel'][l].astype(F32), ((0, 32 - CONV_WIDTH), (0, 0)))
        for n in ('norm_mix_g', 'b_gate', 'kv_norm_g', 'dw_bias', 'conv_ln_g', 'conv_ln_b', 'norm_ffn_g'):
            W[n] = local[n][l][None, :]
        W['q_norm_g'] = jnp.pad(local['q_norm_g'][l], (0, CQ_PAD - Q_LORA))[None, :]
        W['b_forget'] = local['b_forget'][l][:, None]
        return W

    inv_freq = 1.0 / (ROPE_THETA ** (jnp.arange(0, MLA_ROPE, 2, dtype=F32) / MLA_ROPE))
    ang = positions[0].astype(F32)[:, None] * inv_freq
    cos, sin = jnp.cos(ang), jnp.sin(ang)
    zpad = jnp.zeros((S, LANES - MLA_ROPE), F32)
    T = dict(cos_q=jnp.tile(cos, (1, MLA_HEADS)), sin_q=jnp.tile(sin, (1, MLA_HEADS)),
             cos_k=jnp.concatenate([cos, cos, zpad], axis=1), sin_k=jnp.concatenate([-sin, sin, zpad], axis=1),
             sin_k_neg=jnp.concatenate([sin, -sin, zpad], axis=1))

    Ws, saved = [], []
    h = xs
    for l in range(DEPTH):
        W = layer_weights(l)
        h, sv = _layer_forward(h, W, T, l)
        Ws.append(W)
        saved.append(sv)
    loss_part, dh, dh_b, dg_final = _loss_head(h, loss_target[0], local['final_norm_g'][None, :], "loss_head")
    loss = lax.psum(loss_part, ("x", "y", "c"))
    layer_grads = [None] * DEPTH
    for l in range(DEPTH - 1, -1, -1):
        dh, dh_b, layer_grads[l] = _layer_backward(dh, dh_b, saved[l], Ws[l], T, l)
    grad_x = dh[None]

    grads_full = {}
    grads_full['w_in'] = jnp.stack([_restore_w_in(g['w_in']) for g in layer_grads])
    grads_full['w_uq'] = jnp.stack([_restore_w_uq(g['w_uq']) for g in layer_grads])
    grads_full['w_ukv'] = jnp.stack([_restore_w_ukv(g['w_ukv']) for g in layer_grads])
    grads_full['dw_kernel'] = jnp.stack([g['dw_kernel'][:CONV_WIDTH] for g in layer_grads])
    for n in ('w_bo_a', 'w_bo_b', 'w_bo_c', 'w_out', 'w_ffn_down'):
        grads_full[n] = jnp.stack([g[n] for g in layer_grads])
    grads_full['w_ffn_gate'] = jnp.stack([g['w_gu'][:, :FFN_HIDDEN] for g in layer_grads])
    grads_full['w_ffn_up'] = jnp.stack([g['w_gu'][:, FFN_HIDDEN:] for g in layer_grads])
    packed = _pack_grads(grads_full, sh_shapes)
    received = _exchange(packed, False, "scatter_grads")
    R = w_flat.shape[0]
    m_flat = _pack_flat([mom_m[n] for n in sh_names], FLAT_ROW_MULTIPLE)
    v_flat = _pack_flat([mom_v[n] for n in sh_names], FLAT_ROW_MULTIPLE)
    g_sh, d_sh, nm_sh, nv_sh = _adamw_sum("adamw_sharded", received.reshape(N_DEV * R, LANES), w_flat, m_flat, v_flat,
                                          512)

    rep_grads = {
        'norm_mix_g': jnp.concatenate([g['norm_mix_g'] for g in layer_grads]),
        'b_gate': jnp.concatenate([g['b_gate'] for g in layer_grads]),
        'q_norm_g': jnp.concatenate([g['q_norm_g'][:, :Q_LORA] for g in layer_grads]),
        'kv_norm_g': jnp.concatenate([g['kv_norm_g'] for g in layer_grads]),
        'b_forget': jnp.concatenate([g['b_forget'].T for g in layer_grads]),
        'dw_bias': jnp.concatenate([g['dw_bias'] for g in layer_grads]),
        'conv_ln_g': jnp.concatenate([g['conv_ln_g'] for g in layer_grads]),
        'conv_ln_b': jnp.concatenate([g['conv_ln_b'] for g in layer_grads]),
        'norm_ffn_g': jnp.concatenate([g['norm_ffn_g'] for g in layer_grads]),
        'final_norm_g': dg_final[0],
    }
    REP_ROWS = 256
    rg_flat = _pack_flat([rep_grads[n] for n in REPLICATED], REP_ROWS)
    rg_all = _exchange(rg_flat, True, "gather_replicated_grads")
    Rr = rg_flat.shape[0]
    rw = _pack_flat([local[n] for n in REPLICATED], REP_ROWS)
    rm = _pack_flat([mom_m[n] for n in REPLICATED], REP_ROWS)
    rv = _pack_flat([mom_v[n] for n in REPLICATED], REP_ROWS)
    g_rp, d_rp, nm_rp, nv_rp = _adamw_sum("adamw_replicated", rg_all.reshape(N_DEV * Rr, LANES), rw, rm, rv, Rr)

    def by_name(flat_sh, flat_rp):
        d = dict(zip(sh_names, _unpack_flat(flat_sh, sh_shapes)))
        d.update(zip(REPLICATED, _unpack_flat(flat_rp, rep_shapes)))
        return [d[n] for n in WEIGHT_NAMES]

    return (loss, grad_x, *by_name(g_sh, g_rp), *by_name(d_sh, d_rp), *by_name(nm_sh, nm_rp), *by_name(nv_sh, nv_rp))
```

```python
import functools

import numpy as np
import jax
import jax.numpy as jnp
from jax import lax
from jax.experimental import pallas as pl
from jax.experimental.pallas import tpu as pltpu

F32 = jnp.float32
BF16 = jnp.bfloat16

D_MODEL = 1024
DEPTH = 4
CHUNK = 64
MLA_HEADS, MLA_NOPE, MLA_ROPE, MLA_V = 8, 64, 32, 64
Q_LORA, KV_LORA = 384, 256
ROPE_THETA = 10000.0
FOX_HEADS, FOX_HEAD_DIM = 8, 64
FOX_FEATURES = 80
CONV_CHANNELS, CONV_WIDTH = 512, 31
FFN_HIDDEN = 2816
RMS_EPS = 1e-6
LN_EPS = 1e-5
ADAM_LR, ADAM_B1, ADAM_B2, ADAM_EPS, ADAM_WD, ADAM_STEP = 0.001, 0.9, 0.999, 1e-08, 0.01, 10

N_DEV = 8
LANES = 128
VMEM_LIMIT_BYTES = 56 * 1024 * 1024
NEG_BIG = -1e30
ROPE_HALF = MLA_ROPE // 2
CONV_HALO = 32

COL_GATE = 0
COL_CONV = 3072
COL_QB = 4096
COL_KB = 4608
COL_VB = 5120
COL_CQ = 5632
COL_CKV = 6144
COL_SMALL = 6400
IN_COLS = 6656
CQ_PAD = 512
O_CQ, O_CKV, O_KR, O_QB, O_KB, O_VB, O_F, O_CONV, O_GATE, O_END = 0, 384, 640, 672, 1184, 1696, 2208, 2216, 3240, 6312

WEIGHT_NAMES = ['norm_mix_g', 'w_in', 'b_gate', 'q_norm_g', 'w_uq', 'kv_norm_g', 'w_ukv', 'b_forget', 'dw_kernel',
                'dw_bias', 'conv_ln_g', 'conv_ln_b', 'w_bo_a', 'w_bo_b', 'w_bo_c', 'w_out', 'norm_ffn_g',
                'w_ffn_gate', 'w_ffn_up', 'w_ffn_down', 'final_norm_g']
SHARDED = [('w_in', 'col'), ('w_uq', 'col'), ('w_ukv', 'col'), ('dw_kernel', 'col'), ('w_bo_a', 'col'),
           ('w_bo_b', 'col'), ('w_bo_c', 'col'), ('w_out', 'row'), ('w_ffn_gate', 'col'), ('w_ffn_up', 'col'),
           ('w_ffn_down', 'row')]
REPLICATED = ['norm_mix_g', 'b_gate', 'q_norm_g', 'kv_norm_g', 'b_forget', 'dw_bias', 'conv_ln_g', 'conv_ln_b',
              'norm_ffn_g', 'final_norm_g']
FLAT_ROW_MULTIPLE = 512


def _pick(n, cands):
    for c in cands:
        if n % c == 0:
            return c
    raise ValueError(f"no tile for {n}")


def _params(sem):
    return pltpu.CompilerParams(dimension_semantics=sem, vmem_limit_bytes=VMEM_LIMIT_BYTES)


MM_ACC_BYTES = 8 * 1024 * 1024
MM_FULL_K = 2816
_LANE_TILES = (2048, 1664, 1536, 1408, 1024, 768, 512, 384, 256, 128)


def _mm_tiles(mode, M, N, K):
    if mode == "tn":
        tm = _pick(M, tuple(c for c in _LANE_TILES if c <= 1408))
        tn = _pick(N, tuple(c for c in _LANE_TILES if tm * c * 4 <= MM_ACC_BYTES))
        tk = _pick(K, (1024, 512, 256, 128))
    else:
        tm = _pick(M, (1024, 512, 256, 128))
        tn = _pick(N, (512, 384, 256, 128))
        tk = K if K <= MM_FULL_K else _pick(K, _LANE_TILES)
    return tm, tn, tk


def _mm(a, b, *, mode="nn", out_dtype=F32, add=None, name):
    if mode == "nn":
        (M, K), N = a.shape, b.shape[1]
    elif mode == "nt":
        (M, K), N = a.shape, b.shape[0]
    else:
        (K, M), N = a.shape, b.shape[1]
    tm, tn, tk = _mm_tiles(mode, M, N, K)
    nk = K // tk
    dims = {"nn": (((1,), (0,)), ((), ())), "nt": (((1,), (1,)), ((), ())), "tn": (((0,), (0,)), ((), ()))}[mode]
    has_add = add is not None

    def body(*refs):
        if has_add:
            a_ref, b_ref, add_ref, o_ref, acc_ref = refs
        else:
            a_ref, b_ref, o_ref, acc_ref = refs
        k = pl.program_id(2)

        @pl.when(k == 0)
        def _():
            acc_ref[...] = jnp.zeros_like(acc_ref)

        acc_ref[...] += lax.dot_general(a_ref[...].astype(BF16), b_ref[...].astype(BF16), dims,
                                        preferred_element_type=F32)

        @pl.when(k == nk - 1)
        def _():
            r = acc_ref[...]
            if has_add:
                r = r + add_ref[...]
            o_ref[...] = r.astype(o_ref.dtype)

    if mode == "nn":
        a_spec = pl.BlockSpec((tm, tk), lambda i, j, k: (i, k))
        b_spec = pl.BlockSpec((tk, tn), lambda i, j, k: (k, j))
    elif mode == "nt":
        a_spec = pl.BlockSpec((tm, tk), lambda i, j, k: (i, k))
        b_spec = pl.BlockSpec((tn, tk), lambda i, j, k: (j, k))
    else:
        a_spec = pl.BlockSpec((tk, tm), lambda i, j, k: (k, i))
        b_spec = pl.BlockSpec((tk, tn), lambda i, j, k: (k, j))
    o_spec = pl.BlockSpec((tm, tn), lambda i, j, k: (i, j))
    in_specs = [a_spec, b_spec] + ([o_spec] if has_add else [])
    args = (a, b) + ((add,) if has_add else ())
    return pl.pallas_call(
        body, name=name, grid=(M // tm, N // tn, nk), in_specs=in_specs, out_specs=o_spec,
        out_shape=jax.ShapeDtypeStruct((M, N), out_dtype),
        scratch_shapes=[pltpu.VMEM((tm, tn), F32)],
        compiler_params=_params(("parallel", "parallel", "arbitrary")),
    )(*args)


def _rowwise(name, fn, rows, row_ins, full_ins, outs, reds=(), tr=256):
    tr = min(tr, rows)
    assert rows % tr == 0
    n_r, n_f, n_o, n_d = len(row_ins), len(full_ins), len(outs), len(reds)

    def body(*refs):
        ins = [r[...] for r in refs[:n_r + n_f]]
        o_refs = refs[n_r + n_f:n_r + n_f + n_o]
        d_refs = refs[n_r + n_f + n_o:]
        res = fn(*ins)
        for o, v in zip(o_refs, res[:n_o]):
            o[...] = v.astype(o.dtype)
        if n_d:
            @pl.when(pl.program_id(0) == 0)
            def _():
                for d in d_refs:
                    d[...] = jnp.zeros_like(d)

            for d, v in zip(d_refs, res[n_o:]):
                d[...] += v

    in_specs = []
    for (arr, w, cidx, roff) in row_ins:
        in_specs.append(pl.BlockSpec((tr, w), functools.partial(lambda i, c, r: (i + r, c), c=cidx, r=roff)))
    for f in full_ins:
        in_specs.append(pl.BlockSpec(f.shape, lambda i: (0, 0)))
    out_specs = [pl.BlockSpec((tr, w), lambda i: (i, 0)) for (w, _) in outs]
    out_specs += [pl.BlockSpec((r, w), lambda i: (0, 0)) for (r, w) in reds]
    out_shape = [jax.ShapeDtypeStruct((rows, w), dt) for (w, dt) in outs]
    out_shape += [jax.ShapeDtypeStruct((r, w), F32) for (r, w) in reds]
    res = pl.pallas_call(
        body, name=name, grid=(rows // tr,), in_specs=in_specs, out_specs=out_specs, out_shape=out_shape,
        compiler_params=_params(("arbitrary",) if n_d else ("parallel",)),
    )(*[a for (a, _, _, _) in row_ins], *full_ins)
    return res


def _whole(arr, width=None, cidx=0, roff=0):
    return (arr, arr.shape[1] if width is None else width, cidx, roff)


def _colsum(v):
    return jnp.sum(v, axis=0, keepdims=True)


def _sigmoid(z):
    return 1.0 / (1.0 + jnp.exp(-z))


def _rms_fwd(x_in, g, n_true, name):
    def fn(x, g):
        x = x.astype(F32)
        r = lax.rsqrt(jnp.sum(x * x, axis=1, keepdims=True) * (1.0 / n_true) + RMS_EPS)
        return (x * r * g,)

    rows = x_in[0].shape[0]
    return _rowwise(name, fn, rows, [x_in], [g], [(x_in[1], BF16)])[0]


def _rms_bwd(x_in, dh, g, n_true, res, name):
    has_res = res is not None

    def fn(*a):
        if has_res:
            x, dh, rs, g = a
        else:
            x, dh, g = a
        x = x.astype(F32)
        dh = dh.astype(F32)
        r = lax.rsqrt(jnp.sum(x * x, axis=1, keepdims=True) * (1.0 / n_true) + RMS_EPS)
        xh = x * r
        dxh = dh * g
        dx = r * (dxh - xh * (jnp.sum(dxh * xh, axis=1, keepdims=True) * (1.0 / n_true)))
        if has_res:
            dx = dx + rs
            return dx, dx, _colsum(dh * xh)
        return dx, _colsum(dh * xh)

    rows, w = x_in[0].shape[0], x_in[1]
    ins = [x_in, _whole(dh)] + ([_whole(res)] if has_res else [])
    outs = [(w, F32), (w, BF16)] if has_res else [(w, F32)]
    return _rowwise(name, fn, rows, ins, [g], outs, [(1, w)])


ATT_SUB = 2
ATT_SUB_FWD = 2
ATT_HEADS_FWD = 2
ATT_HEADS_BWD = 2
LOG2E = 1.4426950408889634
LN2 = 0.6931471805599453


def _att_tile(S):
    return min(512, S // 2)


def _visible(q_idx, k_idx, group):
    if group == 1:
        return q_idx >= k_idx
    return (q_idx // group) >= (k_idx // group)


def _attn_fwd(q, k, v_t, group, cum2, name):
    H, S, dk = q.shape
    dv = v_t.shape[2]
    t = _att_tile(S)
    ts = t // ATT_SUB_FWD
    n = S // t
    HP = ATT_HEADS_FWD
    bias = cum2 is not None

    def body(*refs):
        if bias:
            q_ref, k_ref, vt_ref, ck_ref, o_ref, lse_ref = refs
        else:
            q_ref, k_ref, vt_ref, o_ref, lse_ref = refs
        i = pl.program_id(1)

        def step(j, carry, masked):
            start = pl.multiple_of(j * t, t)
            subs = []
            for g in range(HP):
                qv = q_ref[g]
                for h in range(ATT_SUB_FWD):
                    rs = pl.multiple_of(start + h * ts, ts)
                    kh = k_ref[g, pl.ds(rs, ts), :]
                    s = lax.dot_general(kh, qv, (((1,), (1,)), ((), ())), preferred_element_type=F32)
                    if bias:
                        s = s - ck_ref[g, pl.ds(rs, ts), :]
                    if masked:
                        kr = lax.broadcasted_iota(jnp.int32, (ts, t), 0) + h * ts
                        qc = lax.broadcasted_iota(jnp.int32, (ts, t), 1)
                        s = jnp.where(_visible(qc, kr, group), s, NEG_BIG)
                    subs.append(s)
            out = []
            for g in range(HP):
                m, l, acc = carry[g]
                for h in range(ATT_SUB_FWD):
                    s = subs[g * ATT_SUB_FWD + h]
                    m_new = jnp.maximum(m, jnp.max(s, axis=0, keepdims=True))
                    p = jnp.exp2(s - m_new)
                    a = jnp.exp2(m - m_new)
                    l = a * l + jnp.sum(p, axis=0, keepdims=True)
                    vh = vt_ref[g, j, :, h * ts:(h + 1) * ts]
                    acc = a * acc + jnp.dot(vh, p.astype(BF16), preferred_element_type=F32)
                    m = m_new
                out.append((m, l, acc))
            return tuple(out)

        init = (jnp.full((1, t), NEG_BIG, F32), jnp.zeros((1, t), F32), jnp.zeros((dv, t), F32))
        carry = lax.fori_loop(0, i, lambda j, cr: step(j, cr, False), (init,) * HP)
        carry = step(i, carry, True)
        for g in range(HP):
            m, l, acc = carry[g]
            o_ref[g] = (acc / l).astype(o_ref.dtype)
            lse_ref[g, 0] = m + jnp.log(l) * LOG2E

    in_specs = [pl.BlockSpec((HP, t, dk), lambda h, i: (h, i, 0)),
                pl.BlockSpec((HP, S, dk), lambda h, i: (h, 0, 0)),
                pl.BlockSpec((HP, n, dv, t), lambda h, i: (h, 0, 0, 0))]
    args = [q, k, v_t]
    if bias:
        in_specs.append(pl.BlockSpec((HP, S, 1), lambda h, i: (h, 0, 0)))
        args.append(cum2.reshape(H, S, 1))
    return pl.pallas_call(
        body, name=name, grid=(H // HP, n), in_specs=in_specs,
        out_specs=[pl.BlockSpec((HP, dv, t), lambda h, i: (h, 0, i)),
                   pl.BlockSpec((HP, 1, 1, t), lambda h, i: (h, i, 0, 0))],
        out_shape=[jax.ShapeDtypeStruct((H, dv, S), BF16), jax.ShapeDtypeStruct((H, n, 1, t), F32)],
        compiler_params=_params(("parallel", "arbitrary")),
    )(*args)


def _attn_bwd_q(q, q_t, k, v, do, do_t, lse, delta, scale, group, cum, name):
    H, S, dk = q.shape
    dv = v.shape[-1]
    t = _att_tile(S)
    ts = t // ATT_SUB
    n = S // t
    HP = ATT_HEADS_BWD
    bias = cum is not None

    def body(*refs):
        if bias:
            q_ref, qt_ref, k_ref, v_ref, do_ref, dot_ref, lse_ref, dl_ref, ck_ref, dq_ref, dkt_ref, dvt_ref = refs
        else:
            q_ref, qt_ref, k_ref, v_ref, do_ref, dot_ref, lse_ref, dl_ref, dq_ref, dkt_ref, dvt_ref = refs
        i = pl.program_id(1)

        @pl.when(i == 0)
        def _():
            dkt_ref[...] = jnp.zeros_like(dkt_ref)
            dvt_ref[...] = jnp.zeros_like(dvt_ref)

        def step(j, carry, masked):
            start = pl.multiple_of(j * t, t)
            subs = []
            for g in range(HP):
                qi = q_ref[g]
                doi = do_ref[g]
                for h in range(ATT_SUB):
                    rs = pl.multiple_of(start + h * ts, ts)
                    kh = k_ref[g, pl.ds(rs, ts), :]
                    vh = v_ref[g, pl.ds(rs, ts), :]
                    s = lax.dot_general(qi, kh, (((1,), (1,)), ((), ())), preferred_element_type=F32)
                    if bias:
                        s = s - ck_ref[g, pl.ds(j, 1), h * ts:(h + 1) * ts]
                    if masked:
                        qr = lax.broadcasted_iota(jnp.int32, (t, ts), 0)
                        kc = lax.broadcasted_iota(jnp.int32, (t, ts), 1) + h * ts
                        s = jnp.where(_visible(qr, kc, group), s, NEG_BIG)
                    dp = lax.dot_general(doi, vh, (((1,), (1,)), ((), ())), preferred_element_type=F32)
                    subs.append((kh, s, dp))
            out = []
            for g in range(HP):
                dq_acc = carry[g]
                for h in range(ATT_SUB):
                    kh, s, dp = subs[g * ATT_SUB + h]
                    lanes = slice(h * ts, (h + 1) * ts)
                    p = jnp.exp2(s - lse_ref[g])
                    ds_b = (p * (dp - dl_ref[g])).astype(BF16)
                    dvt_ref[g, j, :, lanes] += jnp.dot(dot_ref[g, 0], p.astype(BF16), preferred_element_type=F32)
                    dkt_ref[g, j, :, lanes] += jnp.dot(qt_ref[g, 0], ds_b, preferred_element_type=F32)
                    dq_acc = dq_acc + jnp.dot(ds_b, kh, preferred_element_type=F32)
                out.append(dq_acc)
            return tuple(out)

        carry = lax.fori_loop(0, i, lambda j, cr: step(j, cr, False), (jnp.zeros((t, dk), F32),) * HP)
        carry = step(i, carry, True)
        for g in range(HP):
            dq_ref[g] = carry[g] * scale

    row = lambda d: pl.BlockSpec((HP, t, d), lambda h, i: (h, i, 0))
    tile_t = lambda d: pl.BlockSpec((HP, 1, d, t), lambda h, i: (h, i, 0, 0))
    whole = lambda d: pl.BlockSpec((HP, S, d), lambda h, i: (h, 0, 0))
    tiles_out = lambda d: pl.BlockSpec((HP, n, d, t), lambda h, i: (h, 0, 0, 0))
    in_specs = [row(dk), tile_t(dk), whole(dk), whole(dv), row(dv), tile_t(dv), row(1), row(1)]
    args = [q, q_t, k, v, do, do_t, lse.reshape(H, S, 1), delta.reshape(H, S, 1)]
    out_specs = [row(dk), tiles_out(dk), tiles_out(dv)]
    out_shape = [jax.ShapeDtypeStruct((H, S, dk), F32), jax.ShapeDtypeStruct((H, n, dk, t), F32),
                 jax.ShapeDtypeStruct((H, n, dv, t), F32)]
    if bias:
        in_specs.append(pl.BlockSpec((HP, n, t), lambda h, i: (h, 0, 0)))
        args.append(cum.reshape(H, n, t))
    return pl.pallas_call(
        body, name=name, grid=(H // HP, n), in_specs=in_specs, out_specs=out_specs, out_shape=out_shape,
        compiler_params=_params(("parallel", "arbitrary")),
    )(*args)


def _attn_delta(o, do, name):
    H, S, dv = o.shape

    def fn(o, do):
        return (jnp.sum(o.astype(F32) * do.astype(F32), axis=1, keepdims=True),)

    d = _rowwise(name, fn, H * S, [_whole(o.reshape(H * S, dv)), _whole(do.reshape(H * S, dv))], [], [(1, F32)],
                 tr=1024)[0]
    return d.reshape(H, S)


def _fox_gate_fwd(f_t, b, name):
    Hh, S = f_t.shape
    nb = S // LANES

    def body(f_ref, b_ref, cum_ref):
        r = lax.broadcasted_iota(jnp.int32, (LANES, LANES), 0)
        c = lax.broadcasted_iota(jnp.int32, (LANES, LANES), 1)
        upper = (r <= c).astype(F32)
        carry = jnp.zeros((Hh, 1), F32)
        for blk in range(nb):
            z = f_ref[:, blk * LANES:(blk + 1) * LANES] + b_ref[...]
            logf = jnp.minimum(z, 0.0) - jnp.log(1.0 + jnp.exp(-jnp.abs(z)))
            cs = jnp.dot(logf, upper, preferred_element_type=F32, precision=lax.Precision.HIGHEST) + carry
            cum_ref[:, blk * LANES:(blk + 1) * LANES] = cs * LOG2E
            carry = cs[:, LANES - 1:LANES]

    return pl.pallas_call(body, name=name, out_shape=jax.ShapeDtypeStruct((Hh, S), F32),
                          compiler_params=pltpu.CompilerParams(vmem_limit_bytes=VMEM_LIMIT_BYTES))(f_t, b)


def _fox_gate_bwd(f_t, b, dcum_k, dcum_q, name):
    Hh, S = f_t.shape
    nb = S // LANES

    def body(f_ref, b_ref, dck_ref, dcq_ref, dz_ref, db_ref):
        r = lax.broadcasted_iota(jnp.int32, (LANES, LANES), 0)
        c = lax.broadcasted_iota(jnp.int32, (LANES, LANES), 1)
        lower = (r >= c).astype(F32)
        carry = jnp.zeros((Hh, 1), F32)
        db = jnp.zeros((Hh, 1), F32)
        for blk in range(nb - 1, -1, -1):
            sl = slice(blk * LANES, (blk + 1) * LANES)
            rc = jnp.dot(dck_ref[:, sl] + dcq_ref[:, sl], lower, preferred_element_type=F32,
                         precision=lax.Precision.HIGHEST) + carry
            carry = rc[:, 0:1]
            z = f_ref[:, sl] + b_ref[...]
            dz = rc * (1.0 - _sigmoid(z))
            dz_ref[:, sl] = dz
            db = db + jnp.sum(dz, axis=1, keepdims=True)
        db_ref[...] = db

    return pl.pallas_call(
        body, name=name,
        out_shape=[jax.ShapeDtypeStruct((Hh, S), F32), jax.ShapeDtypeStruct((Hh, 1), F32)],
        compiler_params=pltpu.CompilerParams(vmem_limit_bytes=VMEM_LIMIT_BYTES))(f_t, b, dcum_k, dcum_q)


def _conv_tile(S):
    return min(512, S // 2)


def _glu(cin):
    a = cin[:, :CONV_CHANNELS].astype(F32)
    b = cin[:, CONV_CHANNELS:].astype(F32)
    return a * _sigmoid(b)


def _conv_taps(ext_ref, w_ref, ts, first):
    acc = jnp.zeros((ts, CONV_CHANNELS), F32)
    for j in range(CONV_WIDTH):
        acc = acc + w_ref[j:j + 1, :] * ext_ref[first + j:first + j + ts, :]
    return acc


def _fill_u0_ext(ext_ref, cin_ref, halo_ref, i):
    ext_ref[0:CONV_HALO, :] = jnp.where(i > 0, _glu(halo_ref[...]), 0.0)
    ext_ref[CONV_HALO:, :] = _glu(cin_ref[...])


def _conv_specs(ts, cidx):
    per = ts // CONV_HALO
    wide = 2 * CONV_CHANNELS
    return [pl.BlockSpec((ts, wide), lambda i: (i, cidx)),
            pl.BlockSpec((CONV_HALO, wide), lambda i: (jnp.maximum(i * per - 1, 0), cidx))]


def _conv_fwd(proj, w, bias, ln_g, ln_b, name):
    S = proj.shape[0]
    ts = _conv_tile(S)
    C = CONV_CHANNELS

    def body(cin_ref, halo_ref, w_ref, b_ref, g_ref, bb_ref, o_ref, ext_ref):
        _fill_u0_ext(ext_ref, cin_ref, halo_ref, pl.program_id(0))
        u1 = _conv_taps(ext_ref, w_ref, ts, CONV_HALO - (CONV_WIDTH - 1)) + b_ref[...]
        mu = jnp.mean(u1, axis=1, keepdims=True)
        xc = u1 - mu
        rstd = lax.rsqrt(jnp.mean(xc * xc, axis=1, keepdims=True) + LN_EPS)
        u2 = xc * rstd * g_ref[...] + bb_ref[...]
        o_ref[...] = (u2 * _sigmoid(u2)).astype(o_ref.dtype)

    vec = pl.BlockSpec((1, C), lambda i: (0, 0))
    return pl.pallas_call(
        body, name=name, grid=(S // ts,),
        in_specs=_conv_specs(ts, COL_CONV // (2 * C)) + [pl.BlockSpec((32, C), lambda i: (0, 0)), vec, vec, vec],
        out_specs=pl.BlockSpec((ts, C), lambda i: (i, 0)),
        out_shape=jax.ShapeDtypeStruct((S, C), BF16),
        scratch_shapes=[pltpu.VMEM((ts + CONV_HALO, C), F32)],
        compiler_params=_params(("parallel",)),
    )(proj, proj, w, bias, ln_g, ln_b)


def _conv_bwd_a(proj, du3, w, bias, ln_g, ln_b, name):
    S = proj.shape[0]
    ts = _conv_tile(S)
    C = CONV_CHANNELS
    first = CONV_HALO - (CONV_WIDTH - 1)

    def body(cin_ref, halo_ref, du3_ref, w_ref, b_ref, g_ref, bb_ref, du1_ref, dw_ref, dbias_ref, dg_ref, dbb_ref,
             ext_ref):
        i = pl.program_id(0)

        @pl.when(i == 0)
        def _():
            dw_ref[...] = jnp.zeros_like(dw_ref)
            dbias_ref[...] = jnp.zeros_like(dbias_ref)
            dg_ref[...] = jnp.zeros_like(dg_ref)
            dbb_ref[...] = jnp.zeros_like(dbb_ref)

        _fill_u0_ext(ext_ref, cin_ref, halo_ref, i)
        u1 = _conv_taps(ext_ref, w_ref, ts, first) + b_ref[...]
        mu = jnp.mean(u1, axis=1, keepdims=True)
        xc = u1 - mu
        rstd = lax.rsqrt(jnp.mean(xc * xc, axis=1, keepdims=True) + LN_EPS)
        xh = xc * rstd
        u2 = xh * g_ref[...] + bb_ref[...]
        sg = _sigmoid(u2)
        du2 = du3_ref[...].astype(F32) * (sg * (1.0 + u2 * (1.0 - sg)))
        dg_ref[...] += _colsum(du2 * xh)
        dbb_ref[...] += _colsum(du2)
        dxh = du2 * g_ref[...]
        du1 = rstd * (dxh - jnp.mean(dxh, axis=1, keepdims=True) - xh * jnp.mean(dxh * xh, axis=1, keepdims=True))
        du1_ref[...] = du1
        dbias_ref[...] += _colsum(du1)
        for j in range(CONV_WIDTH):
            dw_ref[j:j + 1, :] += _colsum(du1 * ext_ref[first + j:first + j + ts, :])

    vec = pl.BlockSpec((1, C), lambda i: (0, 0))
    taps = pl.BlockSpec((32, C), lambda i: (0, 0))
    return pl.pallas_call(
        body, name=name, grid=(S // ts,),
        in_specs=_conv_specs(ts, COL_CONV // (2 * C)) + [pl.BlockSpec((ts, C), lambda i: (i, 0)), taps, vec, vec, vec],
        out_specs=[pl.BlockSpec((ts, C), lambda i: (i, 0)), taps, vec, vec, vec],
        out_shape=[jax.ShapeDtypeStruct((S, C), F32), jax.ShapeDtypeStruct((32, C), F32)]
        + [jax.ShapeDtypeStruct((1, C), F32)] * 3,
        scratch_shapes=[pltpu.VMEM((ts + CONV_HALO, C), F32)],
        compiler_params=_params(("arbitrary",)),
    )(proj, proj, du3, w, bias, ln_g, ln_b)


def _conv_bwd_b(proj, du1, w, name):
    S = proj.shape[0]
    ts = _conv_tile(S)
    C = CONV_CHANNELS
    per = ts // CONV_HALO
    nblk = S // ts
    last_halo = S // CONV_HALO - 1

    def body(cin_ref, du1_ref, nxt_ref, w_ref, o_ref, ext_ref):
        i = pl.program_id(0)
        ext_ref[0:ts, :] = du1_ref[...]
        ext_ref[ts:, :] = jnp.where(i < nblk - 1, nxt_ref[...], 0.0)
        du0 = jnp.zeros((ts, C), F32)
        for j in range(CONV_WIDTH):
            off = CONV_WIDTH - 1 - j
            du0 = du0 + w_ref[j:j + 1, :] * ext_ref[off:off + ts, :]
        a = cin_ref[:, :C].astype(F32)
        sg = _sigmoid(cin_ref[:, C:].astype(F32))
        o_ref[:, :C] = (du0 * sg).astype(o_ref.dtype)
        o_ref[:, C:] = (du0 * a * sg * (1.0 - sg)).astype(o_ref.dtype)

    return pl.pallas_call(
        body, name=name, grid=(nblk,),
        in_specs=[pl.BlockSpec((ts, 2 * C), lambda i: (i, COL_CONV // (2 * C))),
                  pl.BlockSpec((ts, C), lambda i: (i, 0)),
                  pl.BlockSpec((CONV_HALO, C), lambda i: (jnp.minimum((i + 1) * per, last_halo), 0)),
                  pl.BlockSpec((32, C), lambda i: (0, 0))],
        out_specs=pl.BlockSpec((ts, 2 * C), lambda i: (i, 0)),
        out_shape=jax.ShapeDtypeStruct((S, 2 * C), BF16),
        scratch_shapes=[pltpu.VMEM((ts + CONV_HALO, C), F32)],
        compiler_params=_params(("parallel",)),
    )(proj, du1, du1, w)


def _mesh_pos():
    return lax.axis_index("x"), lax.axis_index("y"), lax.axis_index("c")


def _exchange(x, gather, name):
    R = x.shape[-2]

    def body(x_ref, out_ref, send_sems, recv_sems, local_sem):
        mx, my, mc = _mesh_pos()
        me = 4 * mx + 2 * my + mc

        def src(dst_dev):
            return x_ref if gather else x_ref.at[dst_dev]

        local = pltpu.make_async_copy(src(me), out_ref.at[me], local_sem)
        local.start()
        copies = []
        for k in range(1, N_DEV):
            px, py, pc = mx ^ (k >> 2), my ^ ((k >> 1) & 1), mc ^ (k & 1)
            peer = 4 * px + 2 * py + pc
            cp = pltpu.make_async_remote_copy(
                src_ref=src(peer), dst_ref=out_ref.at[me], send_sem=send_sems.at[k - 1], recv_sem=recv_sems.at[k - 1],
                device_id=(px, py, pc), device_id_type=pl.DeviceIdType.MESH)
            cp.start()
            copies.append(cp)
        for cp in copies:
            cp.wait_recv()
        for cp in copies:
            cp.wait_send()
        local.wait()

    return pl.pallas_call(
        body, name=name,
        in_specs=[pl.BlockSpec(memory_space=pl.ANY)], out_specs=pl.BlockSpec(memory_space=pl.ANY),
        out_shape=jax.ShapeDtypeStruct((N_DEV, R, LANES), x.dtype),
        scratch_shapes=[pltpu.SemaphoreType.DMA((N_DEV - 1,)), pltpu.SemaphoreType.DMA((N_DEV - 1,)),
                        pltpu.SemaphoreType.DMA(())],
    )(x)


def _gather_two_level(x, name):
    R = x.shape[0]

    def body(x_ref, out_ref, send_sems, recv_sems, local_sem):
        mx, my, mc = _mesh_pos()
        me, sibling = (mx, my, mc), (mx, my, 1 - mc)
        chips = [(1 - mx, my), (mx, 1 - my), (1 - mx, 1 - my)]

        def slot(px, py, pc):
            return out_ref.at[4 * px + 2 * py + pc]

        def copy(k, block, to, src=None):
            return pltpu.make_async_remote_copy(
                src_ref=slot(*block) if src is None else src, dst_ref=slot(*block), send_sem=send_sems.at[k],
                recv_sem=recv_sems.at[k], device_id=to, device_id_type=pl.DeviceIdType.MESH)

        mine = pltpu.make_async_copy(x_ref, slot(*me), local_sem)
        mine.start()
        first = [copy(0, me, sibling, src=x_ref)]
        first += [copy(1 + j, me, (*chip, mc), src=x_ref) for j, chip in enumerate(chips)]
        for cp in first:
            cp.start()
        passed = [copy(4 + j, (*chip, mc), sibling) for j, chip in enumerate(chips)]
        for j, chip in enumerate(chips):
            copy(1 + j, (*chip, mc), me).wait_recv()
            passed[j].start()
        copy(0, sibling, me).wait_recv()
        for j, chip in enumerate(chips):
            copy(4 + j, (*chip, 1 - mc), me).wait_recv()
        for cp in first + passed:
            cp.wait_send()
        mine.wait()

    return pl.pallas_call(
        body, name=name,
        in_specs=[pl.BlockSpec(memory_space=pl.ANY)], out_specs=pl.BlockSpec(memory_space=pl.ANY),
        out_shape=jax.ShapeDtypeStruct((N_DEV, R, LANES), x.dtype),
        scratch_shapes=[pltpu.SemaphoreType.DMA((N_DEV - 1,)), pltpu.SemaphoreType.DMA((N_DEV - 1,)),
                        pltpu.SemaphoreType.DMA(())],
    )(x)


SEGMENT_ROWS = 16


def _seg_rows(shape):
    n = int(np.prod(shape))
    return -(-n // (SEGMENT_ROWS * LANES)) * SEGMENT_ROWS


def _flat_total_rows(shapes, multiple):
    rows = sum(_seg_rows(s) for s in shapes)
    return -(-rows // multiple) * multiple


def _to_rows(a, lead=()):
    shape = a.shape[len(lead):]
    n, rows = int(np.prod(shape)), _seg_rows(shape)
    if n == rows * LANES:
        return a.reshape(lead + (rows, LANES))
    flat = a.reshape(lead + (n,))
    flat = jnp.pad(flat, [(0, 0)] * len(lead) + [(0, rows * LANES - n)])
    return flat.reshape(lead + (rows, LANES))


def _from_rows(seg, shape, lead=()):
    n, rows = int(np.prod(shape)), _seg_rows(shape)
    if n == rows * LANES:
        return seg.reshape(lead + tuple(shape))
    return seg.reshape(lead + (rows * LANES,))[..., :n].reshape(lead + tuple(shape))


def _pack_flat(arrs, multiple, lead=()):
    parts = [_to_rows(a, lead) for a in arrs]
    rows = sum(p.shape[-2] for p in parts)
    total = -(-rows // multiple) * multiple
    if total != rows:
        parts.append(jnp.zeros(lead + (total - rows, LANES), parts[0].dtype))
    return jnp.concatenate(parts, axis=len(lead))


def _unpack_flat(flat, shapes, lead=()):
    out, r0 = [], 0
    for s in shapes:
        rows = _seg_rows(s)
        out.append(_from_rows(flat[..., r0:r0 + rows, :], s, lead))
        r0 += rows
    return out


def _adamw_sum(name, g_slabs, w, m, v, tr):
    R = w.shape[0]
    c1 = 1.0 - ADAM_B1 ** ADAM_STEP
    c2 = 1.0 - ADAM_B2 ** ADAM_STEP

    def fn(*a):
        g = a[0].astype(F32)
        for d in range(1, N_DEV):
            g = g + a[d].astype(F32)
        w, m, v = a[N_DEV:]
        m_new = ADAM_B1 * m + (1.0 - ADAM_B1) * g
        v_new = ADAM_B2 * v + (1.0 - ADAM_B2) * (g * g)
        delta = -ADAM_LR * ((m_new / c1) / (jnp.sqrt(v_new / c2) + ADAM_EPS) + ADAM_WD * w)
        return g, delta, m_new, v_new

    tr = min(tr, R)
    ins = [(g_slabs, LANES, 0, d * (R // tr)) for d in range(N_DEV)] + [_whole(w), _whole(m), _whole(v)]
    return _rowwise(name, fn, R, ins, [], [(LANES, F32)] * 4, tr=tr)


def _full_weights(gathered, local_shapes):
    segs = _unpack_flat(gathered, local_shapes, lead=(N_DEV,))
    out = {}
    for (name, kind), shp, seg in zip(SHARDED, local_shapes, segs):
        L, a, b = shp
        if kind == 'col':
            out[name] = seg.transpose(1, 2, 0, 3).reshape(L, a, N_DEV * b)
        else:
            out[name] = seg.transpose(1, 0, 2, 3).reshape(L, N_DEV * a, b)
    return out


def _pack_grads(grads, local_shapes):
    parts = []
    for (name, kind), shp in zip(SHARDED, local_shapes):
        L, a, b = shp
        g = grads[name].astype(BF16)
        if kind == 'col':
            parts.append(g.reshape(L, a, N_DEV, b).transpose(2, 0, 1, 3))
        else:
            parts.append(g.reshape(L, N_DEV, a, b).transpose(1, 0, 2, 3))
    return _pack_flat(parts, FLAT_ROW_MULTIPLE, lead=(N_DEV,))


def _rearrange_w_in(w):
    z = lambda n: jnp.zeros((w.shape[0], n), w.dtype)
    return jnp.concatenate([
        w[:, O_GATE:O_END], w[:, O_CONV:O_GATE], w[:, O_QB:O_KB], w[:, O_KB:O_VB], w[:, O_VB:O_F],
        w[:, O_CQ:O_CKV], z(CQ_PAD - Q_LORA), w[:, O_CKV:O_KR], w[:, O_KR:O_QB], w[:, O_F:O_CONV],
        z(LANES - MLA_ROPE - FOX_HEADS), z(IN_COLS - COL_SMALL - LANES)], axis=1)


def _restore_w_in(g):
    return jnp.concatenate([
        g[:, COL_CQ:COL_CQ + Q_LORA], g[:, COL_CKV:COL_CKV + KV_LORA], g[:, COL_SMALL:COL_SMALL + MLA_ROPE],
        g[:, COL_QB:COL_KB], g[:, COL_KB:COL_VB], g[:, COL_VB:COL_CQ],
        g[:, COL_SMALL + MLA_ROPE:COL_SMALL + MLA_ROPE + FOX_HEADS], g[:, COL_CONV:COL_QB], g[:, COL_GATE:COL_CONV]],
        axis=1)


def _rearrange_w_uq(w):
    w3 = w.reshape(Q_LORA, MLA_HEADS, MLA_NOPE + MLA_ROPE)
    cols = jnp.concatenate([w3[:, :, :MLA_NOPE].reshape(Q_LORA, -1),
                            w3[:, :, MLA_NOPE:MLA_NOPE + ROPE_HALF].reshape(Q_LORA, -1),
                            w3[:, :, MLA_NOPE + ROPE_HALF:].reshape(Q_LORA, -1)], axis=1)
    return jnp.pad(cols, ((0, CQ_PAD - Q_LORA), (0, 0)))


def _restore_w_uq(g):
    g = g[:Q_LORA]
    n = MLA_HEADS * MLA_NOPE
    h = MLA_HEADS * ROPE_HALF
    parts = [g[:, :n].reshape(Q_LORA, MLA_HEADS, MLA_NOPE), g[:, n:n + h].reshape(Q_LORA, MLA_HEADS, ROPE_HALF),
             g[:, n + h:].reshape(Q_LORA, MLA_HEADS, ROPE_HALF)]
    return jnp.concatenate(parts, axis=2).reshape(Q_LORA, -1)


def _rearrange_w_ukv(w):
    w3 = w.reshape(KV_LORA, MLA_HEADS, MLA_NOPE + MLA_V)
    return jnp.concatenate([w3[:, :, :MLA_NOPE].reshape(KV_LORA, -1), w3[:, :, MLA_NOPE:].reshape(KV_LORA, -1)], axis=1)


def _restore_w_ukv(g):
    n = MLA_HEADS * MLA_NOPE
    parts = [g[:, :n].reshape(KV_LORA, MLA_HEADS, MLA_NOPE), g[:, n:].reshape(KV_LORA, MLA_HEADS, MLA_V)]
    return jnp.concatenate(parts, axis=2).reshape(KV_LORA, -1)


def _heads(a, H):
    S = a.shape[0]
    return a.reshape(S, H, -1).transpose(1, 0, 2)


def _tiles_t(a, H):
    S = a.shape[0]
    t = _att_tile(S)
    return a.reshape(S // t, t, H, -1).transpose(2, 0, 3, 1)


def _untiles_t(a):
    H, n, d, t = a.shape
    return a.transpose(1, 3, 0, 2).reshape(n * t, H * d)


def _unheads(a):
    H, S, d = a.shape
    return a.transpose(1, 0, 2).reshape(S, H * d)


def _rope_q(x_src, cos, sin, name):
    def fn(x1, x2, c, s):
        return x1 * c - x2 * s, x2 * c + x1 * s

    S = x_src.shape[0]
    return _rowwise(name, fn, S, [(x_src, LANES, 4, 0), (x_src, LANES, 5, 0), _whole(cos), _whole(sin)], [],
                    [(LANES, F32), (LANES, F32)], tr=512)


def _rope_k(x_in, cos_k, sin_k, fold_heads, name):
    def fn(x, c, s):
        if fold_heads:
            x = x[:, :LANES] + x[:, LANES:]
            x = x + pltpu.roll(x, 64, 1)
            x = x + pltpu.roll(x, 32, 1)
        lane = lax.broadcasted_iota(jnp.int32, x.shape, 1)
        partner = jnp.where(lane < ROPE_HALF, pltpu.roll(x, LANES - ROPE_HALF, 1), pltpu.roll(x, ROPE_HALF, 1))
        return (x * c + partner * s,)

    S = x_in[0].shape[0]
    return _rowwise(name, fn, S, [x_in, _whole(cos_k), _whole(sin_k)], [], [(LANES, F32)], tr=512)[0]


def _layer_forward(x, W, T, l):
    S = x.shape[0]
    nm = lambda s: f"{s}_l{l}"
    h1 = _rms_fwd(_whole(x), W['norm_mix_g'], D_MODEL, nm("rms_mix"))
    proj = _mm(h1, W['w_in'], name=nm("mm_in"))
    small = proj[:, COL_SMALL:COL_SMALL + LANES]

    cqn = _rms_fwd((proj, CQ_PAD, COL_CQ // CQ_PAD, 0), W['q_norm_g'], Q_LORA, nm("rms_q"))
    ckvn = _rms_fwd((proj, KV_LORA, COL_CKV // KV_LORA, 0), W['kv_norm_g'], KV_LORA, nm("rms_kv"))
    qa = _mm(cqn, W['w_uq'], name=nm("mm_uq"))
    kv = _mm(ckvn, W['w_ukv'], out_dtype=BF16, name=nm("mm_ukv"))
    q_r1, q_r2 = _rope_q(qa, T['cos_q'], T['sin_q'], nm("rope_q"))
    k_rope = _rope_k(_whole(small), T['cos_k'], T['sin_k'], False, nm("rope_k"))[:, :MLA_ROPE]
    n_nope = MLA_HEADS * MLA_NOPE
    mla_scale = (MLA_NOPE + MLA_ROPE) ** -0.5
    q_mla_s = (jnp.concatenate([qa[:, :n_nope].reshape(S, MLA_HEADS, MLA_NOPE), q_r1.reshape(S, MLA_HEADS, ROPE_HALF),
                                q_r2.reshape(S, MLA_HEADS, ROPE_HALF)], axis=2) * (mla_scale * LOG2E)).astype(BF16)
    q_mla = q_mla_s.transpose(1, 0, 2)
    q_mla_t = _tiles_t(q_mla_s.reshape(S, -1), MLA_HEADS)
    k_mla = jnp.concatenate([kv[:, :n_nope].reshape(S, MLA_HEADS, MLA_NOPE),
                             jnp.broadcast_to(k_rope.astype(BF16)[:, None, :], (S, MLA_HEADS, MLA_ROPE))],
                            axis=2).transpose(1, 0, 2)
    v_mla = _heads(kv[:, n_nope:], MLA_HEADS)
    o_mla_t, lse_a = _attn_fwd(q_mla, k_mla, _tiles_t(kv[:, n_nope:], MLA_HEADS), CHUNK, None, nm("mla_fwd"))
    o_mla = o_mla_t.transpose(0, 2, 1)
    oa_cat = o_mla_t.transpose(2, 0, 1).reshape(S, MLA_HEADS * MLA_V)
    o_a = _mm(oa_cat, W['w_bo_a'], out_dtype=BF16, name=nm("mm_bo_a"))

    f_t = small[:, MLA_ROPE:MLA_ROPE + FOX_HEADS].T
    cum = _fox_gate_fwd(f_t, W['b_forget'], nm("fox_gate"))
    fox_scale = FOX_HEAD_DIM ** -0.5
    qb = (proj[:, COL_QB:COL_KB] * (fox_scale * LOG2E)).astype(BF16).reshape(S, FOX_HEADS, FOX_HEAD_DIM)
    kb = proj[:, COL_KB:COL_VB].astype(BF16).reshape(S, FOX_HEADS, FOX_HEAD_DIM)
    one = jnp.ones((S, FOX_HEADS, 1), BF16)
    zero = lambda w: jnp.zeros((S, FOX_HEADS, w), BF16)
    extra = FOX_FEATURES - FOX_HEAD_DIM
    q_fox = jnp.concatenate([qb, zero(extra)], axis=2).transpose(1, 0, 2)
    k_fox = jnp.concatenate([kb, one, zero(extra - 1)], axis=2).transpose(1, 0, 2)
    q_fox_t = _tiles_t(jnp.concatenate([qb, one, zero(extra - 1)], axis=2).reshape(S, -1), FOX_HEADS)
    vb = proj[:, COL_VB:COL_CQ].astype(BF16)
    v_fox = _heads(vb, FOX_HEADS)
    o_fox_t, lse_b = _attn_fwd(q_fox, k_fox, _tiles_t(vb, FOX_HEADS), 1, cum, nm("fox_fwd"))
    o_fox = o_fox_t.transpose(0, 2, 1)
    ob_cat = o_fox_t.transpose(2, 0, 1).reshape(S, FOX_HEADS * FOX_HEAD_DIM)
    o_b = _mm(ob_cat, W['w_bo_b'], out_dtype=BF16, name=nm("mm_bo_b"))

    u3 = _conv_fwd(proj, W['dw_kernel'], W['dw_bias'], W['conv_ln_g'], W['conv_ln_b'], nm("conv_fwd"))
    o_c = _mm(u3, W['w_bo_c'], out_dtype=BF16, name=nm("mm_bo_c"))

    def gate_fn(la, lb, lc, oa, ob, oc, bg):
        ga = _sigmoid(la + bg[:, :D_MODEL])
        gb = _sigmoid(lb + bg[:, D_MODEL:2 * D_MODEL])
        gc = _sigmoid(lc + bg[:, 2 * D_MODEL:])
        return (ga * oa.astype(F32) + gb * ob.astype(F32) + gc * oc.astype(F32),)

    logit_ins = [(proj, D_MODEL, COL_GATE // D_MODEL + b, 0) for b in range(3)]
    y = _rowwise(nm("gate_fwd"), gate_fn, S, logit_ins + [_whole(o_a), _whole(o_b), _whole(o_c)], [W['b_gate']],
                 [(D_MODEL, BF16)])[0]
    x2 = _mm(y, W['w_out'], add=x, name=nm("mm_out"))

    h2 = _rms_fwd(_whole(x2), W['norm_ffn_g'], D_MODEL, nm("rms_ffn"))
    gu = _mm(h2, W['w_gu'], out_dtype=BF16, name=nm("mm_gu"))

    def swiglu_fn(gt, up):
        gt = gt.astype(F32)
        return (gt * _sigmoid(gt) * up.astype(F32),)

    ff = _rowwise(nm("swiglu_fwd"), swiglu_fn, S, [(gu, FFN_HIDDEN, 0, 0), (gu, FFN_HIDDEN, 1, 0)], [],
                  [(FFN_HIDDEN, BF16)])[0]
    x3 = _mm(ff, W['w_ffn_down'], add=x2, name=nm("mm_down"))

    saved = dict(x=x, h1=h1, proj=proj, small=small, cqn=cqn, ckvn=ckvn, q_mla=q_mla, k_mla=k_mla, v_mla=v_mla,
                 q_mla_t=q_mla_t, q_fox_t=q_fox_t,
                 o_mla=o_mla, lse_a=lse_a, oa_cat=oa_cat, o_a=o_a, f_t=f_t, cum=cum, q_fox=q_fox, k_fox=k_fox,
                 v_fox=v_fox, o_fox=o_fox, lse_b=lse_b, ob_cat=ob_cat, o_b=o_b, u3=u3, o_c=o_c, y=y, x2=x2, h2=h2,
                 gu=gu, ff=ff)
    return x3, saved


def _layer_backward(dx3, dx3_b, sv, W, T, l):
    S = dx3.shape[0]
    nm = lambda s: f"{s}_l{l}"
    G = {}

    G['w_ffn_down'] = _mm(sv['ff'], dx3_b, mode="tn", name=nm("mm_down_dw"))
    dff = _mm(dx3_b, W['w_ffn_down'], mode="nt", out_dtype=BF16, name=nm("mm_down_dx"))

    def swiglu_bwd_fn(gt, up, d):
        gt, up, d = gt.astype(F32), up.astype(F32), d.astype(F32)
        sg = _sigmoid(gt)
        return (jnp.concatenate([d * up * (sg * (1.0 + gt * (1.0 - sg))), d * (gt * sg)], axis=1),)

    dgu = _rowwise(nm("swiglu_bwd"), swiglu_bwd_fn, S,
                   [(sv['gu'], FFN_HIDDEN, 0, 0), (sv['gu'], FFN_HIDDEN, 1, 0), _whole(dff)], [],
                   [(2 * FFN_HIDDEN, BF16)])[0]
    G['w_gu'] = _mm(sv['h2'], dgu, mode="tn", name=nm("mm_gu_dw"))
    dh2 = _mm(dgu, W['w_gu'], mode="nt", out_dtype=BF16, name=nm("mm_gu_dx"))
    dx2, dx2_b, G['norm_ffn_g'] = _rms_bwd(_whole(sv['x2']), dh2, W['norm_ffn_g'], D_MODEL, dx3, nm("rms_ffn_bwd"))

    G['w_out'] = _mm(sv['y'], dx2_b, mode="tn", name=nm("mm_out_dw"))
    dy = _mm(dx2_b, W['w_out'], mode="nt", out_dtype=BF16, name=nm("mm_out_dx"))

    def gate_bwd_fn(la, lb, lc, oa, ob, oc, dy, bg):
        dy = dy.astype(F32)
        outs, dls = [], []
        for k, (lg, o) in enumerate(((la, oa), (lb, ob), (lc, oc))):
            g = _sigmoid(lg + bg[:, k * D_MODEL:(k + 1) * D_MODEL])
            outs.append(dy * g)
            dls.append(dy * o.astype(F32) * g * (1.0 - g))
        dl = jnp.concatenate(dls, axis=1)
        return (*outs, dl, _colsum(dl))

    proj = sv['proj']
    logit_ins = [(proj, D_MODEL, COL_GATE // D_MODEL + b, 0) for b in range(3)]
    do_a, do_b, do_c, dlogit, G['b_gate'] = _rowwise(
        nm("gate_bwd"), gate_bwd_fn, S, logit_ins + [_whole(sv['o_a']), _whole(sv['o_b']), _whole(sv['o_c']), _whole(dy)],
        [W['b_gate']], [(D_MODEL, BF16)] * 3 + [(3 * D_MODEL, BF16)], [(1, 3 * D_MODEL)], tr=128)

    G['w_bo_c'] = _mm(sv['u3'], do_c, mode="tn", name=nm("mm_bo_c_dw"))
    du3 = _mm(do_c, W['w_bo_c'], mode="nt", out_dtype=BF16, name=nm("mm_bo_c_dx"))
    du1, G['dw_kernel'], G['dw_bias'], G['conv_ln_g'], G['conv_ln_b'] = _conv_bwd_a(
        proj, du3, W['dw_kernel'], W['dw_bias'], W['conv_ln_g'], W['conv_ln_b'], nm("conv_bwd_a"))
    dconv = _conv_bwd_b(proj, du1, W['dw_kernel'], nm("conv_bwd_b"))

    G['w_bo_b'] = _mm(sv['ob_cat'], do_b, mode="tn", name=nm("mm_bo_b_dw"))
    dob_cat = _mm(do_b, W['w_bo_b'], mode="nt", out_dtype=BF16, name=nm("mm_bo_b_dx"))
    dob = _heads(dob_cat, FOX_HEADS)
    delta_b = _attn_delta(sv['o_fox'], dob, nm("fox_delta"))
    fox_scale = FOX_HEAD_DIM ** -0.5
    dq_fx, dk_fxt, dv_ft = _attn_bwd_q(
        sv['q_fox'], sv['q_fox_t'], sv['k_fox'], sv['v_fox'], dob, _tiles_t(dob_cat, FOX_HEADS),
        sv['lse_b'].reshape(FOX_HEADS, S), delta_b, fox_scale, 1, sv['cum'], nm("fox_bwd"))
    dq_f = dq_fx[:, :, :FOX_HEAD_DIM]
    dk_ft = dk_fxt[:, :, :FOX_HEAD_DIM, :]
    dcum_q = dq_fx[:, :, FOX_HEAD_DIM] * (1.0 / fox_scale)
    dcum_k = -dk_fxt[:, :, FOX_HEAD_DIM, :].reshape(FOX_HEADS, S)
    dz, G['b_forget'] = _fox_gate_bwd(sv['f_t'], W['b_forget'], dcum_k, dcum_q, nm("fox_gate_bwd"))

    G['w_bo_a'] = _mm(sv['oa_cat'], do_a, mode="tn", name=nm("mm_bo_a_dw"))
    doa_cat = _mm(do_a, W['w_bo_a'], mode="nt", out_dtype=BF16, name=nm("mm_bo_a_dx"))
    doa = _heads(doa_cat, MLA_HEADS)
    delta_a = _attn_delta(sv['o_mla'], doa, nm("mla_delta"))
    dq_m, dk_mt, dv_mt = _attn_bwd_q(sv['q_mla'], sv['q_mla_t'], sv['k_mla'], sv['v_mla'], doa,
                                     _tiles_t(doa_cat, MLA_HEADS), sv['lse_a'].reshape(MLA_HEADS, S), delta_a,
                                     (MLA_NOPE + MLA_ROPE) ** -0.5, CHUNK, None, nm("mla_bwd"))
    dq_s = dq_m.transpose(1, 0, 2)
    dqr = jnp.concatenate([dq_s[:, :, MLA_NOPE:MLA_NOPE + ROPE_HALF].reshape(S, -1),
                           dq_s[:, :, MLA_NOPE + ROPE_HALF:].reshape(S, -1)], axis=1)

    def rope_q_bwd_fn(d1, d2, c, s):
        return d1 * c + d2 * s, d2 * c - d1 * s

    dq_r1, dq_r2 = _rowwise(nm("rope_q_bwd"), rope_q_bwd_fn, S,
                            [(dqr, LANES, 0, 0), (dqr, LANES, 1, 0), _whole(T['cos_q']), _whole(T['sin_q'])], [],
                            [(LANES, BF16), (LANES, BF16)], tr=512)
    dqa = jnp.concatenate([dq_s[:, :, :MLA_NOPE].reshape(S, -1).astype(BF16), dq_r1, dq_r2], axis=1)
    G['w_uq'] = _mm(sv['cqn'], dqa, mode="tn", name=nm("mm_uq_dw"))
    dcqn = _mm(dqa, W['w_uq'], mode="nt", name=nm("mm_uq_dx"))
    dcq, G['q_norm_g'] = _rms_bwd((proj, CQ_PAD, COL_CQ // CQ_PAD, 0), dcqn, W['q_norm_g'], Q_LORA, None,
                                  nm("rms_q_bwd"))
    dk_s = (_untiles_t(dk_mt) * LN2).reshape(S, MLA_HEADS, MLA_NOPE + MLA_ROPE)
    dkv = jnp.concatenate([dk_s[:, :, :MLA_NOPE].reshape(S, -1), _untiles_t(dv_mt)], axis=1).astype(BF16)
    G['w_ukv'] = _mm(sv['ckvn'], dkv, mode="tn", name=nm("mm_ukv_dw"))
    dckvn = _mm(dkv, W['w_ukv'], mode="nt", name=nm("mm_ukv_dx"))
    dckv, G['kv_norm_g'] = _rms_bwd((proj, KV_LORA, COL_CKV // KV_LORA, 0), dckvn, W['kv_norm_g'], KV_LORA, None,
                                    nm("rms_kv_bwd"))
    dk_rope_heads = dk_s[:, :, MLA_NOPE:].reshape(S, MLA_HEADS * MLA_ROPE)
    dkr = _rope_k(_whole(dk_rope_heads), T['cos_k'], T['sin_k_neg'], True, nm("rope_k_bwd"))

    dsmall = jnp.concatenate([dkr[:, :MLA_ROPE], dz.T, jnp.zeros((S, LANES - MLA_ROPE - FOX_HEADS), F32)], axis=1)
    dproj = jnp.concatenate([
        dlogit, dconv, _unheads(dq_f).astype(BF16), (_untiles_t(dk_ft) * LN2).astype(BF16),
        _untiles_t(dv_ft).astype(BF16),
        dcq.astype(BF16), dckv.astype(BF16), dsmall.astype(BF16),
        jnp.zeros((S, IN_COLS - COL_SMALL - LANES), BF16)], axis=1)
    G['w_in'] = _mm(sv['h1'], dproj, mode="tn", name=nm("mm_in_dw"))
    dh1 = _mm(dproj, W['w_in'], mode="nt", out_dtype=BF16, name=nm("mm_in_dx"))
    dx, dx_b, G['norm_mix_g'] = _rms_bwd(_whole(sv['x']), dh1, W['norm_mix_g'], D_MODEL, dx2, nm("rms_mix_bwd"))
    return dx, dx_b, G


def _loss_head(x, target, g, name):
    def fn(x, t, g):
        r = lax.rsqrt(jnp.mean(x * x, axis=1, keepdims=True) + RMS_EPS)
        xh = x * r
        e = xh * g - t
        part = 0.5 * jnp.sum(jnp.mean(e * e, axis=1, keepdims=True), axis=0, keepdims=True)
        dy = e * (1.0 / D_MODEL)
        dxh = dy * g
        dx = r * (dxh - xh * jnp.mean(dxh * xh, axis=1, keepdims=True))
        return dx, dx, jnp.broadcast_to(part, (1, LANES)), _colsum(dy * xh)

    S = x.shape[0]
    dx, dx_b, part, dg = _rowwise(name, fn, S, [_whole(x), _whole(target)], [g], [(D_MODEL, F32), (D_MODEL, BF16)],
                                  [(1, LANES), (1, D_MODEL)])
    return part[0, 0], dx, dx_b, dg


def kernel(x, positions, norm_mix_g, w_in, b_gate, q_norm_g, w_uq, kv_norm_g, w_ukv, b_forget, dw_kernel, dw_bias, conv_ln_g, conv_ln_b, w_bo_a, w_bo_b, w_bo_c, w_out, norm_ffn_g, w_ffn_gate, w_ffn_up, w_ffn_down, final_norm_g, loss_target, m_norm_mix_g, m_w_in, m_b_gate, m_q_norm_g, m_w_uq, m_kv_norm_g, m_w_ukv, m_b_forget, m_dw_kernel, m_dw_bias, m_conv_ln_g, m_conv_ln_b, m_w_bo_a, m_w_bo_b, m_w_bo_c, m_w_out, m_norm_ffn_g, m_w_ffn_gate, m_w_ffn_up, m_w_ffn_down, m_final_norm_g, v_norm_mix_g, v_w_in, v_b_gate, v_q_norm_g, v_w_uq, v_kv_norm_g, v_w_ukv, v_b_forget, v_dw_kernel, v_dw_bias, v_conv_ln_g, v_conv_ln_b, v_w_bo_a, v_w_bo_b, v_w_bo_c, v_w_out, v_norm_ffn_g, v_w_ffn_gate, v_w_ffn_up, v_w_ffn_down, v_final_norm_g):
    local = dict(norm_mix_g=norm_mix_g, w_in=w_in, b_gate=b_gate, q_norm_g=q_norm_g, w_uq=w_uq, kv_norm_g=kv_norm_g,
                 w_ukv=w_ukv, b_forget=b_forget, dw_kernel=dw_kernel, dw_bias=dw_bias, conv_ln_g=conv_ln_g,
                 conv_ln_b=conv_ln_b, w_bo_a=w_bo_a, w_bo_b=w_bo_b, w_bo_c=w_bo_c, w_out=w_out, norm_ffn_g=norm_ffn_g,
                 w_ffn_gate=w_ffn_gate, w_ffn_up=w_ffn_up, w_ffn_down=w_ffn_down, final_norm_g=final_norm_g)
    mom_m = dict(norm_mix_g=m_norm_mix_g, w_in=m_w_in, b_gate=m_b_gate, q_norm_g=m_q_norm_g, w_uq=m_w_uq,
                 kv_norm_g=m_kv_norm_g, w_ukv=m_w_ukv, b_forget=m_b_forget, dw_kernel=m_dw_kernel, dw_bias=m_dw_bias,
                 conv_ln_g=m_conv_ln_g, conv_ln_b=m_conv_ln_b, w_bo_a=m_w_bo_a, w_bo_b=m_w_bo_b, w_bo_c=m_w_bo_c,
                 w_out=m_w_out, norm_ffn_g=m_norm_ffn_g, w_ffn_gate=m_w_ffn_gate, w_ffn_up=m_w_ffn_up,
                 w_ffn_down=m_w_ffn_down, final_norm_g=m_final_norm_g)
    mom_v = dict(norm_mix_g=v_norm_mix_g, w_in=v_w_in, b_gate=v_b_gate, q_norm_g=v_q_norm_g, w_uq=v_w_uq,
                 kv_norm_g=v_kv_norm_g, w_ukv=v_w_ukv, b_forget=v_b_forget, dw_kernel=v_dw_kernel, dw_bias=v_dw_bias,
                 conv_ln_g=v_conv_ln_g, conv_ln_b=v_conv_ln_b, w_bo_a=v_w_bo_a, w_bo_b=v_w_bo_b, w_bo_c=v_w_bo_c,
                 w_out=v_w_out, norm_ffn_g=v_norm_ffn_g, w_ffn_gate=v_w_ffn_gate, w_ffn_up=v_w_ffn_up,
                 w_ffn_down=v_w_ffn_down, final_norm_g=v_final_norm_g)
    S = x.shape[1]
    xs = x[0]
    sh_names = [n for n, _ in SHARDED]
    sh_shapes = [local[n].shape for n in sh_names]
    rep_shapes = [local[n].shape for n in REPLICATED]

    w_flat = _pack_flat([local[n] for n in sh_names], FLAT_ROW_MULTIPLE)
    gathered = _gather_two_level(w_flat.astype(BF16), "gather_weights")
    full = _full_weights(gathered, sh_shapes)

    def layer_weights(l):
        W = {n: full[n][l] for n in ('w_bo_a', 'w_bo_b', 'w_bo_c', 'w_out', 'w_ffn_down')}
        W['w_in'] = _rearrange_w_in(full['w_in'][l])
        W['w_uq'] = _rearrange_w_uq(full['w_uq'][l])
        W['w_ukv'] = _rearrange_w_ukv(full['w_ukv'][l])
        W['w_gu'] = jnp.concatenate([full['w_ffn_gate'][l], full['w_ffn_up'][l]], axis=1)
        W['dw_kernel'] = jnp.pad(full['dw_kernel'][l].astype(F32), ((0, 32 - CONV_WIDTH), (0, 0)))
        for n in ('norm_mix_g', 'b_gate', 'kv_norm_g', 'dw_bias', 'conv_ln_g', 'conv_ln_b', 'norm_ffn_g'):
            W[n] = local[n][l][None, :]
        W['q_norm_g'] = jnp.pad(local['q_norm_g'][l], (0, CQ_PAD - Q_LORA))[None, :]
        W['b_forget'] = local['b_forget'][l][:, None]
        return W

    inv_freq = 1.0 / (ROPE_THETA ** (jnp.arange(0, MLA_ROPE, 2, dtype=F32) / MLA_ROPE))
    ang = positions[0].astype(F32)[:, None] * inv_freq
    cos, sin = jnp.cos(ang), jnp.sin(ang)
    zpad = jnp.zeros((S, LANES - MLA_ROPE), F32)
    T = dict(cos_q=jnp.tile(cos, (1, MLA_HEADS)), sin_q=jnp.tile(sin, (1, MLA_HEADS)),
             cos_k=jnp.concatenate([cos, cos, zpad], axis=1), sin_k=jnp.concatenate([-sin, sin, zpad], axis=1),
             sin_k_neg=jnp.concatenate([sin, -sin, zpad], axis=1))

    Ws, saved = [], []
    h = xs
    for l in range(DEPTH):
        W = layer_weights(l)
        h, sv = _layer_forward(h, W, T, l)
        Ws.append(W)
        saved.append(sv)
    loss_part, dh, dh_b, dg_final = _loss_head(h, loss_target[0], local['final_norm_g'][None, :], "loss_head")
    loss = lax.psum(loss_part, ("x", "y", "c"))
    layer_grads = [None] * DEPTH
    for l in range(DEPTH - 1, -1, -1):
        dh, dh_b, layer_grads[l] = _layer_backward(dh, dh_b, saved[l], Ws[l], T, l)
    grad_x = dh[None]

    grads_full = {}
    grads_full['w_in'] = jnp.stack([_restore_w_in(g['w_in']) for g in layer_grads])
    grads_full['w_uq'] = jnp.stack([_restore_w_uq(g['w_uq']) for g in layer_grads])
    grads_full['w_ukv'] = jnp.stack([_restore_w_ukv(g['w_ukv']) for g in layer_grads])
    grads_full['dw_kernel'] = jnp.stack([g['dw_kernel'][:CONV_WIDTH] for g in layer_grads])
    for n in ('w_bo_a', 'w_bo_b', 'w_bo_c', 'w_out', 'w_ffn_down'):
        grads_full[n] = jnp.stack([g[n] for g in layer_grads])
    grads_full['w_ffn_gate'] = jnp.stack([g['w_gu'][:, :FFN_HIDDEN] for g in layer_grads])
    grads_full['w_ffn_up'] = jnp.stack([g['w_gu'][:, FFN_HIDDEN:] for g in layer_grads])
    packed = _pack_grads(grads_full, sh_shapes)
    received = _exchange(packed, False, "scatter_grads")
    R = w_flat.shape[0]
    m_flat = _pack_flat([mom_m[n] for n in sh_names], FLAT_ROW_MULTIPLE)
    v_flat = _pack_flat([mom_v[n] for n in sh_names], FLAT_ROW_MULTIPLE)
    g_sh, d_sh, nm_sh, nv_sh = _adamw_sum("adamw_sharded", received.reshape(N_DEV * R, LANES), w_flat, m_flat, v_flat,
                                          512)

    rep_grads = {
        'norm_mix_g': jnp.concatenate([g['norm_mix_g'] for g in layer_grads]),
        'b_gate': jnp.concatenate([g['b_gate'] for g in layer_grads]),
        'q_norm_g': jnp.concatenate([g['q_norm_g'][:, :Q_LORA] for g in layer_grads]),
        'kv_norm_g': jnp.concatenate([g['kv_norm_g'] for g in layer_grads]),
        'b_forget': jnp.concatenate([g['b_forget'].T for g in layer_grads]),
        'dw_bias': jnp.concatenate([g['dw_bias'] for g in layer_grads]),
        'conv_ln_g': jnp.concatenate([g['conv_ln_g'] for g in layer_grads]),
        'conv_ln_b': jnp.concatenate([g['conv_ln_b'] for g in layer_grads]),
        'norm_ffn_g': jnp.concatenate([g['norm_ffn_g'] for g in layer_grads]),
        'final_norm_g': dg_final[0],
    }
    REP_ROWS = 256
    rg_flat = _pack_flat([rep_grads[n] for n in REPLICATED], REP_ROWS)
    rg_all = _exchange(rg_flat, True, "gather_replicated_grads")
    Rr = rg_flat.shape[0]
    rw = _pack_flat([local[n] for n in REPLICATED], REP_ROWS)
    rm = _pack_flat([mom_m[n] for n in REPLICATED], REP_ROWS)
    rv = _pack_flat([mom_v[n] for n in REPLICATED], REP_ROWS)
    g_rp, d_rp, nm_rp, nv_rp = _adamw_sum("adamw_replicated", rg_all.reshape(N_DEV * Rr, LANES), rw, rm, rv, Rr)

    def by_name(flat_sh, flat_rp):
        d = dict(zip(sh_names, _unpack_flat(flat_sh, sh_shapes)))
        d.update(zip(REPLICATED, _unpack_flat(flat_rp, rep_shapes)))
        return [d[n] for n in WEIGHT_NAMES]

    return (loss, grad_x, *by_name(g_sh, g_rp), *by_name(d_sh, d_rp), *by_name(nm_sh, nm_rp), *by_name(nv_sh, nv_rp))
```

```python
import functools

import numpy as np
import jax
import jax.numpy as jnp
from jax import lax
from jax.experimental import pallas as pl
from jax.experimental.pallas import tpu as pltpu

F32 = jnp.float32
BF16 = jnp.bfloat16

D_MODEL = 1024
DEPTH = 4
CHUNK = 64
MLA_HEADS, MLA_NOPE, MLA_ROPE, MLA_V = 8, 64, 32, 64
Q_LORA, KV_LORA = 384, 256
ROPE_THETA = 10000.0
FOX_HEADS, FOX_HEAD_DIM = 8, 64
FOX_FEATURES = 80
CONV_CHANNELS, CONV_WIDTH = 512, 31
FFN_HIDDEN = 2816
RMS_EPS = 1e-6
LN_EPS = 1e-5
ADAM_LR, ADAM_B1, ADAM_B2, ADAM_EPS, ADAM_WD, ADAM_STEP = 0.001, 0.9, 0.999, 1e-08, 0.01, 10

N_DEV = 8
LANES = 128
VMEM_LIMIT_BYTES = 56 * 1024 * 1024
NEG_BIG = -1e30
ROPE_HALF = MLA_ROPE // 2
CONV_HALO = 32

COL_GATE = 0
COL_CONV = 3072
COL_QB = 4096
COL_KB = 4608
COL_VB = 5120
COL_CQ = 5632
COL_CKV = 6144
COL_SMALL = 6400
IN_COLS = 6656
CQ_PAD = 512
O_CQ, O_CKV, O_KR, O_QB, O_KB, O_VB, O_F, O_CONV, O_GATE, O_END = 0, 384, 640, 672, 1184, 1696, 2208, 2216, 3240, 6312

WEIGHT_NAMES = ['norm_mix_g', 'w_in', 'b_gate', 'q_norm_g', 'w_uq', 'kv_norm_g', 'w_ukv', 'b_forget', 'dw_kernel',
                'dw_bias', 'conv_ln_g', 'conv_ln_b', 'w_bo_a', 'w_bo_b', 'w_bo_c', 'w_out', 'norm_ffn_g',
                'w_ffn_gate', 'w_ffn_up', 'w_ffn_down', 'final_norm_g']
SHARDED = [('w_in', 'col'), ('w_uq', 'col'), ('w_ukv', 'col'), ('dw_kernel', 'col'), ('w_bo_a', 'col'),
           ('w_bo_b', 'col'), ('w_bo_c', 'col'), ('w_out', 'row'), ('w_ffn_gate', 'col'), ('w_ffn_up', 'col'),
           ('w_ffn_down', 'row')]
REPLICATED = ['norm_mix_g', 'b_gate', 'q_norm_g', 'kv_norm_g', 'b_forget', 'dw_bias', 'conv_ln_g', 'conv_ln_b',
              'norm_ffn_g', 'final_norm_g']
FLAT_ROW_MULTIPLE = 512


def _pick(n, cands):
    for c in cands:
        if n % c == 0:
            return c
    raise ValueError(f"no tile for {n}")


def _params(sem):
    return pltpu.CompilerParams(dimension_semantics=sem, vmem_limit_bytes=VMEM_LIMIT_BYTES)


MM_ACC_BYTES = 8 * 1024 * 1024
MM_FULL_K = 2816
_LANE_TILES = (2048, 1664, 1536, 1408, 1024, 768, 512, 384, 256, 128)


def _mm_tiles(mode, M, N, K):
    if mode == "tn":
        tm = _pick(M, tuple(c for c in _LANE_TILES if c <= 1408))
        tn = _pick(N, tuple(c for c in _LANE_TILES if tm * c * 4 <= MM_ACC_BYTES))
        tk = _pick(K, (1024, 512, 256, 128))
    else:
        tm = _pick(M, (1024, 512, 256, 128))
        tn = _pick(N, (512, 384, 256, 128))
        tk = K if K <= MM_FULL_K else _pick(K, _LANE_TILES)
    return tm, tn, tk


def _mm(a, b, *, mode="nn", out_dtype=F32, add=None, name):
    if mode == "nn":
        (M, K), N = a.shape, b.shape[1]
    elif mode == "nt":
        (M, K), N = a.shape, b.shape[0]
    else:
        (K, M), N = a.shape, b.shape[1]
    tm, tn, tk = _mm_tiles(mode, M, N, K)
    nk = K // tk
    dims = {"nn": (((1,), (0,)), ((), ())), "nt": (((1,), (1,)), ((), ())), "tn": (((0,), (0,)), ((), ()))}[mode]
    has_add = add is not None

    def body(*refs):
        if has_add:
            a_ref, b_ref, add_ref, o_ref, acc_ref = refs
        else:
            a_ref, b_ref, o_ref, acc_ref = refs
        k = pl.program_id(2)

        @pl.when(k == 0)
        def _():
            acc_ref[...] = jnp.zeros_like(acc_ref)

        acc_ref[...] += lax.dot_general(a_ref[...].astype(BF16), b_ref[...].astype(BF16), dims,
                                        preferred_element_type=F32)

        @pl.when(k == nk - 1)
        def _():
            r = acc_ref[...]
            if has_add:
                r = r + add_ref[...]
            o_ref[...] = r.astype(o_ref.dtype)

    if mode == "nn":
        a_spec = pl.BlockSpec((tm, tk), lambda i, j, k: (i, k))
        b_spec = pl.BlockSpec((tk, tn), lambda i, j, k: (k, j))
    elif mode == "nt":
        a_spec = pl.BlockSpec((tm, tk), lambda i, j, k: (i, k))
        b_spec = pl.BlockSpec((tn, tk), lambda i, j, k: (j, k))
    else:
        a_spec = pl.BlockSpec((tk, tm), lambda i, j, k: (k, i))
        b_spec = pl.BlockSpec((tk, tn), lambda i, j, k: (k, j))
    o_spec = pl.BlockSpec((tm, tn), lambda i, j, k: (i, j))
    in_specs = [a_spec, b_spec] + ([o_spec] if has_add else [])
    args = (a, b) + ((add,) if has_add else ())
    return pl.pallas_call(
        body, name=name, grid=(M // tm, N // tn, nk), in_specs=in_specs, out_specs=o_spec,
        out_shape=jax.ShapeDtypeStruct((M, N), out_dtype),
        scratch_shapes=[pltpu.VMEM((tm, tn), F32)],
        compiler_params=_params(("parallel", "parallel", "arbitrary")),
    )(*args)


def _rowwise(name, fn, rows, row_ins, full_ins, outs, reds=(), tr=256):
    tr = min(tr, rows)
    assert rows % tr == 0
    n_r, n_f, n_o, n_d = len(row_ins), len(full_ins), len(outs), len(reds)

    def body(*refs):
        ins = [r[...] for r in refs[:n_r + n_f]]
        o_refs = refs[n_r + n_f:n_r + n_f + n_o]
        d_refs = refs[n_r + n_f + n_o:]
        res = fn(*ins)
        for o, v in zip(o_refs, res[:n_o]):
            o[...] = v.astype(o.dtype)
        if n_d:
            @pl.when(pl.program_id(0) == 0)
            def _():
                for d in d_refs:
                    d[...] = jnp.zeros_like(d)

            for d, v in zip(d_refs, res[n_o:]):
                d[...] += v

    in_specs = []
    for (arr, w, cidx, roff) in row_ins:
        in_specs.append(pl.BlockSpec((tr, w), functools.partial(lambda i, c, r: (i + r, c), c=cidx, r=roff)))
    for f in full_ins:
        in_specs.append(pl.BlockSpec(f.shape, lambda i: (0, 0)))
    out_specs = [pl.BlockSpec((tr, w), lambda i: (i, 0)) for (w, _) in outs]
    out_specs += [pl.BlockSpec((r, w), lambda i: (0, 0)) for (r, w) in reds]
    out_shape = [jax.ShapeDtypeStruct((rows, w), dt) for (w, dt) in outs]
    out_shape += [jax.ShapeDtypeStruct((r, w), F32) for (r, w) in reds]
    res = pl.pallas_call(
        body, name=name, grid=(rows // tr,), in_specs=in_specs, out_specs=out_specs, out_shape=out_shape,
        compiler_params=_params(("arbitrary",) if n_d else ("parallel",)),
    )(*[a for (a, _, _, _) in row_ins], *full_ins)
    return res


def _whole(arr, width=None, cidx=0, roff=0):
    return (arr, arr.shape[1] if width is None else width, cidx, roff)


def _colsum(v):
    return jnp.sum(v, axis=0, keepdims=True)


def _sigmoid(z):
    return 1.0 / (1.0 + jnp.exp(-z))


def _rms_fwd(x_in, g, n_true, name):
    def fn(x, g):
        x = x.astype(F32)
        r = lax.rsqrt(jnp.sum(x * x, axis=1, keepdims=True) * (1.0 / n_true) + RMS_EPS)
        return (x * r * g,)

    rows = x_in[0].shape[0]
    return _rowwise(name, fn, rows, [x_in], [g], [(x_in[1], BF16)])[0]


def _rms_bwd(x_in, dh, g, n_true, res, name):
    has_res = res is not None

    def fn(*a):
        if has_res:
            x, dh, rs, g = a
        else:
            x, dh, g = a
        x = x.astype(F32)
        dh = dh.astype(F32)
        r = lax.rsqrt(jnp.sum(x * x, axis=1, keepdims=True) * (1.0 / n_true) + RMS_EPS)
        xh = x * r
        dxh = dh * g
        dx = r * (dxh - xh * (jnp.sum(dxh * xh, axis=1, keepdims=True) * (1.0 / n_true)))
        if has_res:
            dx = dx + rs
            return dx, dx, _colsum(dh * xh)
        return dx, _colsum(dh * xh)

    rows, w = x_in[0].shape[0], x_in[1]
    ins = [x_in, _whole(dh)] + ([_whole(res)] if has_res else [])
    outs = [(w, F32), (w, BF16)] if has_res else [(w, F32)]
    return _rowwise(name, fn, rows, ins, [g], outs, [(1, w)])


ATT_SUB = 2
ATT_SUB_FWD = 2
ATT_HEADS_FWD = 2
ATT_HEADS_BWD = 2
LOG2E = 1.4426950408889634
LN2 = 0.6931471805599453


def _att_tile(S):
    return min(512, S // 2)


def _visible(q_idx, k_idx, group):
    if group == 1:
        return q_idx >= k_idx
    return (q_idx // group) >= (k_idx // group)


def _attn_fwd(q, k, v_t, group, cum2, name):
    H, S, dk = q.shape
    dv = v_t.shape[2]
    t = _att_tile(S)
    ts = t // ATT_SUB_FWD
    n = S // t
    HP = ATT_HEADS_FWD
    bias = cum2 is not None

    def body(*refs):
        if bias:
            q_ref, k_ref, vt_ref, ck_ref, o_ref, lse_ref = refs
        else:
            q_ref, k_ref, vt_ref, o_ref, lse_ref = refs
        i = pl.program_id(1)

        def step(j, carry, masked):
            start = pl.multiple_of(j * t, t)
            subs = []
            for g in range(HP):
                qv = q_ref[g]
                for h in range(ATT_SUB_FWD):
                    rs = pl.multiple_of(start + h * ts, ts)
                    kh = k_ref[g, pl.ds(rs, ts), :]
                    s = lax.dot_general(kh, qv, (((1,), (1,)), ((), ())), preferred_element_type=F32)
                    if bias:
                        s = s - ck_ref[g, pl.ds(rs, ts), :]
                    if masked:
                        kr = lax.broadcasted_iota(jnp.int32, (ts, t), 0) + h * ts
                        qc = lax.broadcasted_iota(jnp.int32, (ts, t), 1)
                        s = jnp.where(_visible(qc, kr, group), s, NEG_BIG)
                    subs.append(s)
            out = []
            for g in range(HP):
                m, l, acc = carry[g]
                for h in range(ATT_SUB_FWD):
                    s = subs[g * ATT_SUB_FWD + h]
                    m_new = jnp.maximum(m, jnp.max(s, axis=0, keepdims=True))
                    p = jnp.exp2(s - m_new)
                    a = jnp.exp2(m - m_new)
                    l = a * l + jnp.sum(p, axis=0, keepdims=True)
                    vh = vt_ref[g, j, :, h * ts:(h + 1) * ts]
                    acc = a * acc + jnp.dot(vh, p.astype(BF16), preferred_element_type=F32)
                    m = m_new
                out.append((m, l, acc))
            return tuple(out)

        init = (jnp.full((1, t), NEG_BIG, F32), jnp.zeros((1, t), F32), jnp.zeros((dv, t), F32))
        carry = lax.fori_loop(0, i, lambda j, cr: step(j, cr, False), (init,) * HP)
        carry = step(i, carry, True)
        for g in range(HP):
            m, l, acc = carry[g]
            o_ref[g] = (acc / l).astype(o_ref.dtype)
            lse_ref[g, 0] = m + jnp.log(l) * LOG2E

    in_specs = [pl.BlockSpec((HP, t, dk), lambda h, i: (h, i, 0)),
                pl.BlockSpec((HP, S, dk), lambda h, i: (h, 0, 0)),
                pl.BlockSpec((HP, n, dv, t), lambda h, i: (h, 0, 0, 0))]
    args = [q, k, v_t]
    if bias:
        in_specs.append(pl.BlockSpec((HP, S, 1), lambda h, i: (h, 0, 0)))
        args.append(cum2.reshape(H, S, 1))
    return pl.pallas_call(
        body, name=name, grid=(H // HP, n), in_specs=in_specs,
        out_specs=[pl.BlockSpec((HP, dv, t), lambda h, i: (h, 0, i)),
                   pl.BlockSpec((HP, 1, 1, t), lambda h, i: (h, i, 0, 0))],
        out_shape=[jax.ShapeDtypeStruct((H, dv, S), BF16), jax.ShapeDtypeStruct((H, n, 1, t), F32)],
        compiler_params=_params(("parallel", "arbitrary")),
    )(*args)


def _attn_bwd_q(q, q_t, k, v, do, do_t, lse, delta, scale, group, cum, name):
    H, S, dk = q.shape
    dv = v.shape[-1]
    t = _att_tile(S)
    ts = t // ATT_SUB
    n = S // t
    HP = ATT_HEADS_BWD
    bias = cum is not None

    def body(*refs):
        if bias:
            q_ref, qt_ref, k_ref, v_ref, do_ref, dot_ref, lse_ref, dl_ref, ck_ref, dq_ref, dkt_ref, dvt_ref = refs
        else:
            q_ref, qt_ref, k_ref, v_ref, do_ref, dot_ref, lse_ref, dl_ref, dq_ref, dkt_ref, dvt_ref = refs
        i = pl.program_id(1)

        @pl.when(i == 0)
        def _():
            dkt_ref[...] = jnp.zeros_like(dkt_ref)
            dvt_ref[...] = jnp.zeros_like(dvt_ref)

        def step(j, carry, masked):
            start = pl.multiple_of(j * t, t)
            subs = []
            for g in range(HP):
                qi = q_ref[g]
                doi = do_ref[g]
                for h in range(ATT_SUB):
                    rs = pl.multiple_of(start + h * ts, ts)
                    kh = k_ref[g, pl.ds(rs, ts), :]
                    vh = v_ref[g, pl.ds(rs, ts), :]
                    s = lax.dot_general(qi, kh, (((1,), (1,)), ((), ())), preferred_element_type=F32)
                    if bias:
                        s = s - ck_ref[g, pl.ds(j, 1), h * ts:(h + 1) * ts]
                    if masked:
                        qr = lax.broadcasted_iota(jnp.int32, (t, ts), 0)
                        kc = lax.broadcasted_iota(jnp.int32, (t, ts), 1) + h * ts
                        s = jnp.where(_visible(qr, kc, group), s, NEG_BIG)
                    dp = lax.dot_general(doi, vh, (((1,), (1,)), ((), ())), preferred_element_type=F32)
                    subs.append((kh, s, dp))
            out = []
            for g in range(HP):
                dq_acc = carry[g]
                for h in range(ATT_SUB):
                    kh, s, dp = subs[g * ATT_SUB + h]
                    lanes = slice(h * ts, (h + 1) * ts)
                    p = jnp.exp2(s - lse_ref[g])
                    ds_b = (p * (dp - dl_ref[g])).astype(BF16)
                    dvt_ref[g, j, :, lanes] += jnp.dot(dot_ref[g, 0], p.astype(BF16), preferred_element_type=F32)
                    dkt_ref[g, j, :, lanes] += jnp.dot(qt_ref[g, 0], ds_b, preferred_element_type=F32)
                    dq_acc = dq_acc + jnp.dot(ds_b, kh, preferred_element_type=F32)
                out.append(dq_acc)
            return tuple(out)

        carry = lax.fori_loop(0, i, lambda j, cr: step(j, cr, False), (jnp.zeros((t, dk), F32),) * HP)
        carry = step(i, carry, True)
        for g in range(HP):
            dq_ref[g] = carry[g] * scale

    row = lambda d: pl.BlockSpec((HP, t, d), lambda h, i: (h, i, 0))
    tile_t = lambda d: pl.BlockSpec((HP, 1, d, t), lambda h, i: (h, i, 0, 0))
    whole = lambda d: pl.BlockSpec((HP, S, d), lambda h, i: (h, 0, 0))
    tiles_out = lambda d: pl.BlockSpec((HP, n, d, t), lambda h, i: (h, 0, 0, 0))
    in_specs = [row(dk), tile_t(dk), whole(dk), whole(dv), row(dv), tile_t(dv), row(1), row(1)]
    args = [q, q_t, k, v, do, do_t, lse.reshape(H, S, 1), delta.reshape(H, S, 1)]
    out_specs = [row(dk), tiles_out(dk), tiles_out(dv)]
    out_shape = [jax.ShapeDtypeStruct((H, S, dk), F32), jax.ShapeDtypeStruct((H, n, dk, t), F32),
                 jax.ShapeDtypeStruct((H, n, dv, t), F32)]
    if bias:
        in_specs.append(pl.BlockSpec((HP, n, t), lambda h, i: (h, 0, 0)))
        args.append(cum.reshape(H, n, t))
    return pl.pallas_call(
        body, name=name, grid=(H // HP, n), in_specs=in_specs, out_specs=out_specs, out_shape=out_shape,
        compiler_params=_params(("parallel", "arbitrary")),
    )(*args)


def _attn_delta(o, do, name):
    H, S, dv = o.shape

    def fn(o, do):
        return (jnp.sum(o.astype(F32) * do.astype(F32), axis=1, keepdims=True),)

    d = _rowwise(name, fn, H * S, [_whole(o.reshape(H * S, dv)), _whole(do.reshape(H * S, dv))], [], [(1, F32)],
                 tr=1024)[0]
    return d.reshape(H, S)


def _fox_gate_fwd(f_t, b, name):
    Hh, S = f_t.shape
    nb = S // LANES

    def body(f_ref, b_ref, cum_ref):
        r = lax.broadcasted_iota(jnp.int32, (LANES, LANES), 0)
        c = lax.broadcasted_iota(jnp.int32, (LANES, LANES), 1)
        upper = (r <= c).astype(F32)
        carry = jnp.zeros((Hh, 1), F32)
        for blk in range(nb):
            z = f_ref[:, blk * LANES:(blk + 1) * LANES] + b_ref[...]
            logf = jnp.minimum(z, 0.0) - jnp.log(1.0 + jnp.exp(-jnp.abs(z)))
            cs = jnp.dot(logf, upper, preferred_element_type=F32, precision=lax.Precision.HIGHEST) + carry
            cum_ref[:, blk * LANES:(blk + 1) * LANES] = cs * LOG2E
            carry = cs[:, LANES - 1:LANES]

    return pl.pallas_call(body, name=name, out_shape=jax.ShapeDtypeStruct((Hh, S), F32),
                          compiler_params=pltpu.CompilerParams(vmem_limit_bytes=VMEM_LIMIT_BYTES))(f_t, b)


def _fox_gate_bwd(f_t, b, dcum_k, dcum_q, name):
    Hh, S = f_t.shape
    nb = S // LANES

    def body(f_ref, b_ref, dck_ref, dcq_ref, dz_ref, db_ref):
        r = lax.broadcasted_iota(jnp.int32, (LANES, LANES), 0)
        c = lax.broadcasted_iota(jnp.int32, (LANES, LANES), 1)
        lower = (r >= c).astype(F32)
        carry = jnp.zeros((Hh, 1), F32)
        db = jnp.zeros((Hh, 1), F32)
        for blk in range(nb - 1, -1, -1):
            sl = slice(blk * LANES, (blk + 1) * LANES)
            rc = jnp.dot(dck_ref[:, sl] + dcq_ref[:, sl], lower, preferred_element_type=F32,
                         precision=lax.Precision.HIGHEST) + carry
            carry = rc[:, 0:1]
            z = f_ref[:, sl] + b_ref[...]
            dz = rc * (1.0 - _sigmoid(z))
            dz_ref[:, sl] = dz
            db = db + jnp.sum(dz, axis=1, keepdims=True)
        db_ref[...] = db

    return pl.pallas_call(
        body, name=name,
        out_shape=[jax.ShapeDtypeStruct((Hh, S), F32), jax.ShapeDtypeStruct((Hh, 1), F32)],
        compiler_params=pltpu.CompilerParams(vmem_limit_bytes=VMEM_LIMIT_BYTES))(f_t, b, dcum_k, dcum_q)


def _conv_tile(S):
    return min(512, S // 2)


def _glu(cin):
    a = cin[:, :CONV_CHANNELS].astype(F32)
    b = cin[:, CONV_CHANNELS:].astype(F32)
    return a * _sigmoid(b)


def _conv_taps(ext_ref, w_ref, ts, first):
    acc = jnp.zeros((ts, CONV_CHANNELS), F32)
    for j in range(CONV_WIDTH):
        acc = acc + w_ref[j:j + 1, :] * ext_ref[first + j:first + j + ts, :]
    return acc


def _fill_u0_ext(ext_ref, cin_ref, halo_ref, i):
    ext_ref[0:CONV_HALO, :] = jnp.where(i > 0, _glu(halo_ref[...]), 0.0)
    ext_ref[CONV_HALO:, :] = _glu(cin_ref[...])


def _conv_specs(ts, cidx):
    per = ts // CONV_HALO
    wide = 2 * CONV_CHANNELS
    return [pl.BlockSpec((ts, wide), lambda i: (i, cidx)),
            pl.BlockSpec((CONV_HALO, wide), lambda i: (jnp.maximum(i * per - 1, 0), cidx))]


def _conv_fwd(proj, w, bias, ln_g, ln_b, name):
    S = proj.shape[0]
    ts = _conv_tile(S)
    C = CONV_CHANNELS

    def body(cin_ref, halo_ref, w_ref, b_ref, g_ref, bb_ref, o_ref, ext_ref):
        _fill_u0_ext(ext_ref, cin_ref, halo_ref, pl.program_id(0))
        u1 = _conv_taps(ext_ref, w_ref, ts, CONV_HALO - (CONV_WIDTH - 1)) + b_ref[...]
        mu = jnp.mean(u1, axis=1, keepdims=True)
        xc = u1 - mu
        rstd = lax.rsqrt(jnp.mean(xc * xc, axis=1, keepdims=True) + LN_EPS)
        u2 = xc * rstd * g_ref[...] + bb_ref[...]
        o_ref[...] = (u2 * _sigmoid(u2)).astype(o_ref.dtype)

    vec = pl.BlockSpec((1, C), lambda i: (0, 0))
    return pl.pallas_call(
        body, name=name, grid=(S // ts,),
        in_specs=_conv_specs(ts, COL_CONV // (2 * C)) + [pl.BlockSpec((32, C), lambda i: (0, 0)), vec, vec, vec],
        out_specs=pl.BlockSpec((ts, C), lambda i: (i, 0)),
        out_shape=jax.ShapeDtypeStruct((S, C), BF16),
        scratch_shapes=[pltpu.VMEM((ts + CONV_HALO, C), F32)],
        compiler_params=_params(("parallel",)),
    )(proj, proj, w, bias, ln_g, ln_b)


def _conv_bwd_a(proj, du3, w, bias, ln_g, ln_b, name):
    S = proj.shape[0]
    ts = _conv_tile(S)
    C = CONV_CHANNELS
    first = CONV_HALO - (CONV_WIDTH - 1)

    def body(cin_ref, halo_ref, du3_ref, w_ref, b_ref, g_ref, bb_ref, du1_ref, dw_ref, dbias_ref, dg_ref, dbb_ref,
             ext_ref):
        i = pl.program_id(0)

        @pl.when(i == 0)
        def _():
            dw_ref[...] = jnp.zeros_like(dw_ref)
            dbias_ref[...] = jnp.zeros_like(dbias_ref)
            dg_ref[...] = jnp.zeros_like(dg_ref)
            dbb_ref[...] = jnp.zeros_like(dbb_ref)

        _fill_u0_ext(ext_ref, cin_ref, halo_ref, i)
        u1 = _conv_taps(ext_ref, w_ref, ts, first) + b_ref[...]
        mu = jnp.mean(u1, axis=1, keepdims=True)
        xc = u1 - mu
        rstd = lax.rsqrt(jnp.mean(xc * xc, axis=1, keepdims=True) + LN_EPS)
        xh = xc * rstd
        u2 = xh * g_ref[...] + bb_ref[...]
        sg = _sigmoid(u2)
        du2 = du3_ref[...].astype(F32) * (sg * (1.0 + u2 * (1.0 - sg)))
        dg_ref[...] += _colsum(du2 * xh)
        dbb_ref[...] += _colsum(du2)
        dxh = du2 * g_ref[...]
        du1 = rstd * (dxh - jnp.mean(dxh, axis=1, keepdims=True) - xh * jnp.mean(dxh * xh, axis=1, keepdims=True))
        du1_ref[...] = du1
        dbias_ref[...] += _colsum(du1)
        for j in range(CONV_WIDTH):
            dw_ref[j:j + 1, :] += _colsum(du1 * ext_ref[first + j:first + j + ts, :])

    vec = pl.BlockSpec((1, C), lambda i: (0, 0))
    taps = pl.BlockSpec((32, C), lambda i: (0, 0))
    return pl.pallas_call(
        body, name=name, grid=(S // ts,),
        in_specs=_conv_specs(ts, COL_CONV // (2 * C)) + [pl.BlockSpec((ts, C), lambda i: (i, 0)), taps, vec, vec, vec],
        out_specs=[pl.BlockSpec((ts, C), lambda i: (i, 0)), taps, vec, vec, vec],
        out_shape=[jax.ShapeDtypeStruct((S, C), F32), jax.ShapeDtypeStruct((32, C), F32)]
        + [jax.ShapeDtypeStruct((1, C), F32)] * 3,
        scratch_shapes=[pltpu.VMEM((ts + CONV_HALO, C), F32)],
        compiler_params=_params(("arbitrary",)),
    )(proj, proj, du3, w, bias, ln_g, ln_b)


def _conv_bwd_b(proj, du1, w, name):
    S = proj.shape[0]
    ts = _conv_tile(S)
    C = CONV_CHANNELS
    per = ts // CONV_HALO
    nblk = S // ts
    last_halo = S // CONV_HALO - 1

    def body(cin_ref, du1_ref, nxt_ref, w_ref, o_ref, ext_ref):
        i = pl.program_id(0)
        ext_ref[0:ts, :] = du1_ref[...]
        ext_ref[ts:, :] = jnp.where(i < nblk - 1, nxt_ref[...], 0.0)
        du0 = jnp.zeros((ts, C), F32)
        for j in range(CONV_WIDTH):
            off = CONV_WIDTH - 1 - j
            du0 = du0 + w_ref[j:j + 1, :] * ext_ref[off:off + ts, :]
        a = cin_ref[:, :C].astype(F32)
        sg = _sigmoid(cin_ref[:, C:].astype(F32))
        o_ref[:, :C] = (du0 * sg).astype(o_ref.dtype)
        o_ref[:, C:] = (du0 * a * sg * (1.0 - sg)).astype(o_ref.dtype)

    return pl.pallas_call(
        body, name=name, grid=(nblk,),
        in_specs=[pl.BlockSpec((ts, 2 * C), lambda i: (i, COL_CONV // (2 * C))),
                  pl.BlockSpec((ts, C), lambda i: (i, 0)),
                  pl.BlockSpec((CONV_HALO, C), lambda i: (jnp.minimum((i + 1) * per, last_halo), 0)),
                  pl.BlockSpec((32, C), lambda i: (0, 0))],
        out_specs=pl.BlockSpec((ts, 2 * C), lambda i: (i, 0)),
        out_shape=jax.ShapeDtypeStruct((S, 2 * C), BF16),
        scratch_shapes=[pltpu.VMEM((ts + CONV_HALO, C), F32)],
        compiler_params=_params(("parallel",)),
    )(proj, du1, du1, w)


def _mesh_pos():
    return lax.axis_index("x"), lax.axis_index("y"), lax.axis_index("c")


def _exchange(x, gather, name):
    R = x.shape[-2]

    def body(x_ref, out_ref, send_sems, recv_sems, local_sem):
        mx, my, mc = _mesh_pos()
        me = 4 * mx + 2 * my + mc

        def src(dst_dev):
            return x_ref if gather else x_ref.at[dst_dev]

        local = pltpu.make_async_copy(src(me), out_ref.at[me], local_sem)
        local.start()
        copies = []
        for k in range(1, N_DEV):
            px, py, pc = mx ^ (k >> 2), my ^ ((k >> 1) & 1), mc ^ (k & 1)
            peer = 4 * px + 2 * py + pc
            cp = pltpu.make_async_remote_copy(
                src_ref=src(peer), dst_ref=out_ref.at[me], send_sem=send_sems.at[k - 1], recv_sem=recv_sems.at[k - 1],
                device_id=(px, py, pc), device_id_type=pl.DeviceIdType.MESH)
            cp.start()
            copies.append(cp)
        for cp in copies:
            cp.wait_recv()
        for cp in copies:
            cp.wait_send()
        local.wait()

    return pl.pallas_call(
        body, name=name,
        in_specs=[pl.BlockSpec(memory_space=pl.ANY)], out_specs=pl.BlockSpec(memory_space=pl.ANY),
        out_shape=jax.ShapeDtypeStruct((N_DEV, R, LANES), x.dtype),
        scratch_shapes=[pltpu.SemaphoreType.DMA((N_DEV - 1,)), pltpu.SemaphoreType.DMA((N_DEV - 1,)),
                        pltpu.SemaphoreType.DMA(())],
    )(x)


def _gather_two_level(x, name):
    R = x.shape[0]

    def body(x_ref, out_ref, send_sems, recv_sems, local_sem):
        mx, my, mc = _mesh_pos()
        me, sibling = (mx, my, mc), (mx, my, 1 - mc)
        chips = [(1 - mx, my), (mx, 1 - my), (1 - mx, 1 - my)]

        def slot(px, py, pc):
            return out_ref.at[4 * px + 2 * py + pc]

        def copy(k, block, to, src=None):
            return pltpu.make_async_remote_copy(
                src_ref=slot(*block) if src is None else src, dst_ref=slot(*block), send_sem=send_sems.at[k],
                recv_sem=recv_sems.at[k], device_id=to, device_id_type=pl.DeviceIdType.MESH)

        mine = pltpu.make_async_copy(x_ref, slot(*me), local_sem)
        mine.start()
        first = [copy(0, me, sibling, src=x_ref)]
        first += [copy(1 + j, me, (*chip, mc), src=x_ref) for j, chip in enumerate(chips)]
        for cp in first:
            cp.start()
        passed = [copy(4 + j, (*chip, mc), sibling) for j, chip in enumerate(chips)]
        for j, chip in enumerate(chips):
            copy(1 + j, (*chip, mc), me).wait_recv()
            passed[j].start()
        copy(0, sibling, me).wait_recv()
        for j, chip in enumerate(chips):
            copy(4 + j, (*chip, 1 - mc), me).wait_recv()
        for cp in first + passed:
            cp.wait_send()
        mine.wait()

    return pl.pallas_call(
        body, name=name,
        in_specs=[pl.BlockSpec(memory_space=pl.ANY)], out_specs=pl.BlockSpec(memory_space=pl.ANY),
        out_shape=jax.ShapeDtypeStruct((N_DEV, R, LANES), x.dtype),
        scratch_shapes=[pltpu.SemaphoreType.DMA((N_DEV - 1,)), pltpu.SemaphoreType.DMA((N_DEV - 1,)),
                        pltpu.SemaphoreType.DMA(())],
    )(x)


SEGMENT_ROWS = 16


def _seg_rows(shape):
    n = int(np.prod(shape))
    return -(-n // (SEGMENT_ROWS * LANES)) * SEGMENT_ROWS


def _flat_total_rows(shapes, multiple):
    rows = sum(_seg_rows(s) for s in shapes)
    return -(-rows // multiple) * multiple


def _to_rows(a, lead=()):
    shape = a.shape[len(lead):]
    n, rows = int(np.prod(shape)), _seg_rows(shape)
    if n == rows * LANES:
        return a.reshape(lead + (rows, LANES))
    flat = a.reshape(lead + (n,))
    flat = jnp.pad(flat, [(0, 0)] * len(lead) + [(0, rows * LANES - n)])
    return flat.reshape(lead + (rows, LANES))


def _from_rows(seg, shape, lead=()):
    n, rows = int(np.prod(shape)), _seg_rows(shape)
    if n == rows * LANES:
        return seg.reshape(lead + tuple(shape))
    return seg.reshape(lead + (rows * LANES,))[..., :n].reshape(lead + tuple(shape))


def _pack_flat(arrs, multiple, lead=()):
    parts = [_to_rows(a, lead) for a in arrs]
    rows = sum(p.shape[-2] for p in parts)
    total = -(-rows // multiple) * multiple
    if total != rows:
        parts.append(jnp.zeros(lead + (total - rows, LANES), parts[0].dtype))
    return jnp.concatenate(parts, axis=len(lead))


def _unpack_flat(flat, shapes, lead=()):
    out, r0 = [], 0
    for s in shapes:
        rows = _seg_rows(s)
        out.append(_from_rows(flat[..., r0:r0 + rows, :], s, lead))
        r0 += rows
    return out


def _adamw_sum(name, g_slabs, w, m, v, tr):
    R = w.shape[0]
    c1 = 1.0 - ADAM_B1 ** ADAM_STEP
    c2 = 1.0 - ADAM_B2 ** ADAM_STEP

    def fn(*a):
        g = a[0].astype(F32)
        for d in range(1, N_DEV):
            g = g + a[d].astype(F32)
        w, m, v = a[N_DEV:]
        m_new = ADAM_B1 * m + (1.0 - ADAM_B1) * g
        v_new = ADAM_B2 * v + (1.0 - ADAM_B2) * (g * g)
        delta = -ADAM_LR * ((m_new / c1) / (jnp.sqrt(v_new / c2) + ADAM_EPS) + ADAM_WD * w)
        return g, delta, m_new, v_new

    tr = min(tr, R)
    ins = [(g_slabs, LANES, 0, d * (R // tr)) for d in range(N_DEV)] + [_whole(w), _whole(m), _whole(v)]
    return _rowwise(name, fn, R, ins, [], [(LANES, F32)] * 4, tr=tr)


def _full_weights(gathered, local_shapes):
    segs = _unpack_flat(gathered, local_shapes, lead=(N_DEV,))
    out = {}
    for (name, kind), shp, seg in zip(SHARDED, local_shapes, segs):
        L, a, b = shp
        if kind == 'col':
            out[name] = seg.transpose(1, 2, 0, 3).reshape(L, a, N_DEV * b)
        else:
            out[name] = seg.transpose(1, 0, 2, 3).reshape(L, N_DEV * a, b)
    return out


def _pack_grads(grads, local_shapes):
    parts = []
    for (name, kind), shp in zip(SHARDED, local_shapes):
        L, a, b = shp
        g = grads[name].astype(BF16)
        if kind == 'col':
            parts.append(g.reshape(L, a, N_DEV, b).transpose(2, 0, 1, 3))
        else:
            parts.append(g.reshape(L, N_DEV, a, b).transpose(1, 0, 2, 3))
    return _pack_flat(parts, FLAT_ROW_MULTIPLE, lead=(N_DEV,))


def _rearrange_w_in(w):
    z = lambda n: jnp.zeros((w.shape[0], n), w.dtype)
    return jnp.concatenate([
        w[:, O_GATE:O_END], w[:, O_CONV:O_GATE], w[:, O_QB:O_KB], w[:, O_KB:O_VB], w[:, O_VB:O_F],
        w[:, O_CQ:O_CKV], z(CQ_PAD - Q_LORA), w[:, O_CKV:O_KR], w[:, O_KR:O_QB], w[:, O_F:O_CONV],
        z(LANES - MLA_ROPE - FOX_HEADS), z(IN_COLS - COL_SMALL - LANES)], axis=1)


def _restore_w_in(g):
    return jnp.concatenate([
        g[:, COL_CQ:COL_CQ + Q_LORA], g[:, COL_CKV:COL_CKV + KV_LORA], g[:, COL_SMALL:COL_SMALL + MLA_ROPE],
        g[:, COL_QB:COL_KB], g[:, COL_KB:COL_VB], g[:, COL_VB:COL_CQ],
        g[:, COL_SMALL + MLA_ROPE:COL_SMALL + MLA_ROPE + FOX_HEADS], g[:, COL_CONV:COL_QB], g[:, COL_GATE:COL_CONV]],
        axis=1)


def _rearrange_w_uq(w):
    w3 = w.reshape(Q_LORA, MLA_HEADS, MLA_NOPE + MLA_ROPE)
    cols = jnp.concatenate([w3[:, :, :MLA_NOPE].reshape(Q_LORA, -1),
                            w3[:, :, MLA_NOPE:MLA_NOPE + ROPE_HALF].reshape(Q_LORA, -1),
                            w3[:, :, MLA_NOPE + ROPE_HALF:].reshape(Q_LORA, -1)], axis=1)
    return jnp.pad(cols, ((0, CQ_PAD - Q_LORA), (0, 0)))


def _restore_w_uq(g):
    g = g[:Q_LORA]
    n = MLA_HEADS * MLA_NOPE
    h = MLA_HEADS * ROPE_HALF
    parts = [g[:, :n].reshape(Q_LORA, MLA_HEADS, MLA_NOPE), g[:, n:n + h].reshape(Q_LORA, MLA_HEADS, ROPE_HALF),
             g[:, n + h:].reshape(Q_LORA, MLA_HEADS, ROPE_HALF)]
    return jnp.concatenate(parts, axis=2).reshape(Q_LORA, -1)


def _rearrange_w_ukv(w):
    w3 = w.reshape(KV_LORA, MLA_HEADS, MLA_NOPE + MLA_V)
    return jnp.concatenate([w3[:, :, :MLA_NOPE].reshape(KV_LORA, -1), w3[:, :, MLA_NOPE:].reshape(KV_LORA, -1)], axis=1)


def _restore_w_ukv(g):
    n = MLA_HEADS * MLA_NOPE
    parts = [g[:, :n].reshape(KV_LORA, MLA_HEADS, MLA_NOPE), g[:, n:].reshape(KV_LORA, MLA_HEADS, MLA_V)]
    return jnp.concatenate(parts, axis=2).reshape(KV_LORA, -1)


def _heads(a, H):
    S = a.shape[0]
    return a.reshape(S, H, -1).transpose(1, 0, 2)


def _tiles_t(a, H):
    S = a.shape[0]
    t = _att_tile(S)
    return a.reshape(S // t, t, H, -1).transpose(2, 0, 3, 1)


def _untiles_t(a):
    H, n, d, t = a.shape
    return a.transpose(1, 3, 0, 2).reshape(n * t, H * d)


def _unheads(a):
    H, S, d = a.shape
    return a.transpose(1, 0, 2).reshape(S, H * d)


def _rope_q(x_src, cos, sin, name):
    def fn(x1, x2, c, s):
        return x1 * c - x2 * s, x2 * c + x1 * s

    S = x_src.shape[0]
    return _rowwise(name, fn, S, [(x_src, LANES, 4, 0), (x_src, LANES, 5, 0), _whole(cos), _whole(sin)], [],
                    [(LANES, F32), (LANES, F32)], tr=512)


def _rope_k(x_in, cos_k, sin_k, fold_heads, name):
    def fn(x, c, s):
        if fold_heads:
            x = x[:, :LANES] + x[:, LANES:]
            x = x + pltpu.roll(x, 64, 1)
            x = x + pltpu.roll(x, 32, 1)
        lane = lax.broadcasted_iota(jnp.int32, x.shape, 1)
        partner = jnp.where(lane < ROPE_HALF, pltpu.roll(x, LANES - ROPE_HALF, 1), pltpu.roll(x, ROPE_HALF, 1))
        return (x * c + partner * s,)

    S = x_in[0].shape[0]
    return _rowwise(name, fn, S, [x_in, _whole(cos_k), _whole(sin_k)], [], [(LANES, F32)], tr=512)[0]


def _layer_forward(x, W, T, l):
    S = x.shape[0]
    nm = lambda s: f"{s}_l{l}"
    h1 = _rms_fwd(_whole(x), W['norm_mix_g'], D_MODEL, nm("rms_mix"))
    proj = _mm(h1, W['w_in'], name=nm("mm_in"))
    small = proj[:, COL_SMALL:COL_SMALL + LANES]

    cqn = _rms_fwd((proj, CQ_PAD, COL_CQ // CQ_PAD, 0), W['q_norm_g'], Q_LORA, nm("rms_q"))
    ckvn = _rms_fwd((proj, KV_LORA, COL_CKV // KV_LORA, 0), W['kv_norm_g'], KV_LORA, nm("rms_kv"))
    qa = _mm(cqn, W['w_uq'], name=nm("mm_uq"))
    kv = _mm(ckvn, W['w_ukv'], out_dtype=BF16, name=nm("mm_ukv"))
    q_r1, q_r2 = _rope_q(qa, T['cos_q'], T['sin_q'], nm("rope_q"))
    k_rope = _rope_k(_whole(small), T['cos_k'], T['sin_k'], False, nm("rope_k"))[:, :MLA_ROPE]
    n_nope = MLA_HEADS * MLA_NOPE
    mla_scale = (MLA_NOPE + MLA_ROPE) ** -0.5
    q_mla_s = (jnp.concatenate([qa[:, :n_nope].reshape(S, MLA_HEADS, MLA_NOPE), q_r1.reshape(S, MLA_HEADS, ROPE_HALF),
                                q_r2.reshape(S, MLA_HEADS, ROPE_HALF)], axis=2) * (mla_scale * LOG2E)).astype(BF16)
    q_mla = q_mla_s.transpose(1, 0, 2)
    q_mla_t = _tiles_t(q_mla_s.reshape(S, -1), MLA_HEADS)
    k_mla = jnp.concatenate([kv[:, :n_nope].reshape(S, MLA_HEADS, MLA_NOPE),
                             jnp.broadcast_to(k_rope.astype(BF16)[:, None, :], (S, MLA_HEADS, MLA_ROPE))],
                            axis=2).transpose(1, 0, 2)
    v_mla = _heads(kv[:, n_nope:], MLA_HEADS)
    o_mla_t, lse_a = _attn_fwd(q_mla, k_mla, _tiles_t(kv[:, n_nope:], MLA_HEADS), CHUNK, None, nm("mla_fwd"))
    o_mla = o_mla_t.transpose(0, 2, 1)
    oa_cat = o_mla_t.transpose(2, 0, 1).reshape(S, MLA_HEADS * MLA_V)
    o_a = _mm(oa_cat, W['w_bo_a'], out_dtype=BF16, name=nm("mm_bo_a"))

    f_t = small[:, MLA_ROPE:MLA_ROPE + FOX_HEADS].T
    cum = _fox_gate_fwd(f_t, W['b_forget'], nm("fox_gate"))
    fox_scale = FOX_HEAD_DIM ** -0.5
    qb = (proj[:, COL_QB:COL_KB] * (fox_scale * LOG2E)).astype(BF16).reshape(S, FOX_HEADS, FOX_HEAD_DIM)
    kb = proj[:, COL_KB:COL_VB].astype(BF16).reshape(S, FOX_HEADS, FOX_HEAD_DIM)
    one = jnp.ones((S, FOX_HEADS, 1), BF16)
    zero = lambda w: jnp.zeros((S, FOX_HEADS, w), BF16)
    extra = FOX_FEATURES - FOX_HEAD_DIM
    q_fox = jnp.concatenate([qb, zero(extra)], axis=2).transpose(1, 0, 2)
    k_fox = jnp.concatenate([kb, one, zero(extra - 1)], axis=2).transpose(1, 0, 2)
    q_fox_t = _tiles_t(jnp.concatenate([qb, one, zero(extra - 1)], axis=2).reshape(S, -1), FOX_HEADS)
    vb = proj[:, COL_VB:COL_CQ].astype(BF16)
    v_fox = _heads(vb, FOX_HEADS)
    o_fox_t, lse_b = _attn_fwd(q_fox, k_fox, _tiles_t(vb, FOX_HEADS), 1, cum, nm("fox_fwd"))
    o_fox = o_fox_t.transpose(0, 2, 1)
    ob_cat = o_fox_t.transpose(2, 0, 1).reshape(S, FOX_HEADS * FOX_HEAD_DIM)
    o_b = _mm(ob_cat, W['w_bo_b'], out_dtype=BF16, name=nm("mm_bo_b"))

    u3 = _conv_fwd(proj, W['dw_kernel'], W['dw_bias'], W['conv_ln_g'], W['conv_ln_b'], nm("conv_fwd"))
    o_c = _mm(u3, W['w_bo_c'], out_dtype=BF16, name=nm("mm_bo_c"))

    def gate_fn(la, lb, lc, oa, ob, oc, bg):
        ga = _sigmoid(la + bg[:, :D_MODEL])
        gb = _sigmoid(lb + bg[:, D_MODEL:2 * D_MODEL])
        gc = _sigmoid(lc + bg[:, 2 * D_MODEL:])
        return (ga * oa.astype(F32) + gb * ob.astype(F32) + gc * oc.astype(F32),)

    logit_ins = [(proj, D_MODEL, COL_GATE // D_MODEL + b, 0) for b in range(3)]
    y = _rowwise(nm("gate_fwd"), gate_fn, S, logit_ins + [_whole(o_a), _whole(o_b), _whole(o_c)], [W['b_gate']],
                 [(D_MODEL, BF16)])[0]
    x2 = _mm(y, W['w_out'], add=x, name=nm("mm_out"))

    h2 = _rms_fwd(_whole(x2), W['norm_ffn_g'], D_MODEL, nm("rms_ffn"))
    gu = _mm(h2, W['w_gu'], out_dtype=BF16, name=nm("mm_gu"))

    def swiglu_fn(gt, up):
        gt = gt.astype(F32)
        return (gt * _sigmoid(gt) * up.astype(F32),)

    ff = _rowwise(nm("swiglu_fwd"), swiglu_fn, S, [(gu, FFN_HIDDEN, 0, 0), (gu, FFN_HIDDEN, 1, 0)], [],
                  [(FFN_HIDDEN, BF16)])[0]
    x3 = _mm(ff, W['w_ffn_down'], add=x2, name=nm("mm_down"))

    saved = dict(x=x, h1=h1, proj=proj, small=small, cqn=cqn, ckvn=ckvn, q_mla=q_mla, k_mla=k_mla, v_mla=v_mla,
                 q_mla_t=q_mla_t, q_fox_t=q_fox_t,
                 o_mla=o_mla, lse_a=lse_a, oa_cat=oa_cat, o_a=o_a, f_t=f_t, cum=cum, q_fox=q_fox, k_fox=k_fox,
                 v_fox=v_fox, o_fox=o_fox, lse_b=lse_b, ob_cat=ob_cat, o_b=o_b, u3=u3, o_c=o_c, y=y, x2=x2, h2=h2,
                 gu=gu, ff=ff)
    return x3, saved


def _layer_backward(dx3, dx3_b, sv, W, T, l):
    S = dx3.shape[0]
    nm = lambda s: f"{s}_l{l}"
    G = {}

    G['w_ffn_down'] = _mm(sv['ff'], dx3_b, mode="tn", name=nm("mm_down_dw"))
    dff = _mm(dx3_b, W['w_ffn_down'], mode="nt", out_dtype=BF16, name=nm("mm_down_dx"))

    def swiglu_bwd_fn(gt, up, d):
        gt, up, d = gt.astype(F32), up.astype(F32), d.astype(F32)
        sg = _sigmoid(gt)
        return (jnp.concatenate([d * up * (sg * (1.0 + gt * (1.0 - sg))), d * (gt * sg)], axis=1),)

    dgu = _rowwise(nm("swiglu_bwd"), swiglu_bwd_fn, S,
                   [(sv['gu'], FFN_HIDDEN, 0, 0), (sv['gu'], FFN_HIDDEN, 1, 0), _whole(dff)], [],
                   [(2 * FFN_HIDDEN, BF16)])[0]
    G['w_gu'] = _mm(sv['h2'], dgu, mode="tn", name=nm("mm_gu_dw"))
    dh2 = _mm(dgu, W['w_gu'], mode="nt", out_dtype=BF16, name=nm("mm_gu_dx"))
    dx2, dx2_b, G['norm_ffn_g'] = _rms_bwd(_whole(sv['x2']), dh2, W['norm_ffn_g'], D_MODEL, dx3, nm("rms_ffn_bwd"))

    G['w_out'] = _mm(sv['y'], dx2_b, mode="tn", name=nm("mm_out_dw"))
    dy = _mm(dx2_b, W['w_out'], mode="nt", out_dtype=BF16, name=nm("mm_out_dx"))

    def gate_bwd_fn(la, lb, lc, oa, ob, oc, dy, bg):
        dy = dy.astype(F32)
        outs, dls = [], []
        for k, (lg, o) in enumerate(((la, oa), (lb, ob), (lc, oc))):
            g = _sigmoid(lg + bg[:, k * D_MODEL:(k + 1) * D_MODEL])
            outs.append(dy * g)
            dls.append(dy * o.astype(F32) * g * (1.0 - g))
        dl = jnp.concatenate(dls, axis=1)
        return (*outs, dl, _colsum(dl))

    proj = sv['proj']
    logit_ins = [(proj, D_MODEL, COL_GATE // D_MODEL + b, 0) for b in range(3)]
    do_a, do_b, do_c, dlogit, G['b_gate'] = _rowwise(
        nm("gate_bwd"), gate_bwd_fn, S, logit_ins + [_whole(sv['o_a']), _whole(sv['o_b']), _whole(sv['o_c']), _whole(dy)],
        [W['b_gate']], [(D_MODEL, BF16)] * 3 + [(3 * D_MODEL, BF16)], [(1, 3 * D_MODEL)], tr=128)

    G['w_bo_c'] = _mm(sv['u3'], do_c, mode="tn", name=nm("mm_bo_c_dw"))
    du3 = _mm(do_c, W['w_bo_c'], mode="nt", out_dtype=BF16, name=nm("mm_bo_c_dx"))
    du1, G['dw_kernel'], G['dw_bias'], G['conv_ln_g'], G['conv_ln_b'] = _conv_bwd_a(
        proj, du3, W['dw_kernel'], W['dw_bias'], W['conv_ln_g'], W['conv_ln_b'], nm("conv_bwd_a"))
    dconv = _conv_bwd_b(proj, du1, W['dw_kernel'], nm("conv_bwd_b"))

    G['w_bo_b'] = _mm(sv['ob_cat'], do_b, mode="tn", name=nm("mm_bo_b_dw"))
    dob_cat = _mm(do_b, W['w_bo_b'], mode="nt", out_dtype=BF16, name=nm("mm_bo_b_dx"))
    dob = _heads(dob_cat, FOX_HEADS)
    delta_b = _attn_delta(sv['o_fox'], dob, nm("fox_delta"))
    fox_scale = FOX_HEAD_DIM ** -0.5
    dq_fx, dk_fxt, dv_ft = _attn_bwd_q(
        sv['q_fox'], sv['q_fox_t'], sv['k_fox'], sv['v_fox'], dob, _tiles_t(dob_cat, FOX_HEADS),
        sv['lse_b'].reshape(FOX_HEADS, S), delta_b, fox_scale, 1, sv['cum'], nm("fox_bwd"))
    dq_f = dq_fx[:, :, :FOX_HEAD_DIM]
    dk_ft = dk_fxt[:, :, :FOX_HEAD_DIM, :]
    dcum_q = dq_fx[:, :, FOX_HEAD_DIM] * (1.0 / fox_scale)
    dcum_k = -dk_fxt[:, :, FOX_HEAD_DIM, :].reshape(FOX_HEADS, S)
    dz, G['b_forget'] = _fox_gate_bwd(sv['f_t'], W['b_forget'], dcum_k, dcum_q, nm("fox_gate_bwd"))

    G['w_bo_a'] = _mm(sv['oa_cat'], do_a, mode="tn", name=nm("mm_bo_a_dw"))
    doa_cat = _mm(do_a, W['w_bo_a'], mode="nt", out_dtype=BF16, name=nm("mm_bo_a_dx"))
    doa = _heads(doa_cat, MLA_HEADS)
    delta_a = _attn_delta(sv['o_mla'], doa, nm("mla_delta"))
    dq_m, dk_mt, dv_mt = _attn_bwd_q(sv['q_mla'], sv['q_mla_t'], sv['k_mla'], sv['v_mla'], doa,
                                     _tiles_t(doa_cat, MLA_HEADS), sv['lse_a'].reshape(MLA_HEADS, S), delta_a,
                                     (MLA_NOPE + MLA_ROPE) ** -0.5, CHUNK, None, nm("mla_bwd"))
    dq_s = dq_m.transpose(1, 0, 2)
    dqr = jnp.concatenate([dq_s[:, :, MLA_NOPE:MLA_NOPE + ROPE_HALF].reshape(S, -1),
                           dq_s[:, :, MLA_NOPE + ROPE_HALF:].reshape(S, -1)], axis=1)

    def rope_q_bwd_fn(d1, d2, c, s):
        return d1 * c + d2 * s, d2 * c - d1 * s

    dq_r1, dq_r2 = _rowwise(nm("rope_q_bwd"), rope_q_bwd_fn, S,
                            [(dqr, LANES, 0, 0), (dqr, LANES, 1, 0), _whole(T['cos_q']), _whole(T['sin_q'])], [],
                            [(LANES, BF16), (LANES, BF16)], tr=512)
    dqa = jnp.concatenate([dq_s[:, :, :MLA_NOPE].reshape(S, -1).astype(BF16), dq_r1, dq_r2], axis=1)
    G['w_uq'] = _mm(sv['cqn'], dqa, mode="tn", name=nm("mm_uq_dw"))
    dcqn = _mm(dqa, W['w_uq'], mode="nt", name=nm("mm_uq_dx"))
    dcq, G['q_norm_g'] = _rms_bwd((proj, CQ_PAD, COL_CQ // CQ_PAD, 0), dcqn, W['q_norm_g'], Q_LORA, None,
                                  nm("rms_q_bwd"))
    dk_s = (_untiles_t(dk_mt) * LN2).reshape(S, MLA_HEADS, MLA_NOPE + MLA_ROPE)
    dkv = jnp.concatenate([dk_s[:, :, :MLA_NOPE].reshape(S, -1), _untiles_t(dv_mt)], axis=1).astype(BF16)
    G['w_ukv'] = _mm(sv['ckvn'], dkv, mode="tn", name=nm("mm_ukv_dw"))
    dckvn = _mm(dkv, W['w_ukv'], mode="nt", name=nm("mm_ukv_dx"))
    dckv, G['kv_norm_g'] = _rms_bwd((proj, KV_LORA, COL_CKV // KV_LORA, 0), dckvn, W['kv_norm_g'], KV_LORA, None,
                                    nm("rms_kv_bwd"))
    dk_rope_heads = dk_s[:, :, MLA_NOPE:].reshape(S, MLA_HEADS * MLA_ROPE)
    dkr = _rope_k(_whole(dk_rope_heads), T['cos_k'], T['sin_k_neg'], True, nm("rope_k_bwd"))

    dsmall = jnp.concatenate([dkr[:, :MLA_ROPE], dz.T, jnp.zeros((S, LANES - MLA_ROPE - FOX_HEADS), F32)], axis=1)
    drest = jnp.concatenate([
        _unheads(dq_f).astype(BF16), (_untiles_t(dk_ft) * LN2).astype(BF16), _untiles_t(dv_ft).astype(BF16),
        dcq.astype(BF16), dckv.astype(BF16), dsmall.astype(BF16),
        jnp.zeros((S, IN_COLS - COL_SMALL - LANES), BF16)], axis=1)
    w_gate, w_conv, w_rest = W['w_in'][:, :COL_CONV], W['w_in'][:, COL_CONV:COL_QB], W['w_in'][:, COL_QB:]
    G['w_in'] = jnp.concatenate([_mm(sv['h1'], dlogit, mode="tn", name=nm("mm_in_dw_gate")),
                                 _mm(sv['h1'], dconv, mode="tn", name=nm("mm_in_dw_conv")),
                                 _mm(sv['h1'], drest, mode="tn", name=nm("mm_in_dw_rest"))], axis=1)
    dh1 = _mm(dlogit, w_gate, mode="nt", name=nm("mm_in_dx_gate"))
    dh1 = _mm(dconv, w_conv, mode="nt", add=dh1, name=nm("mm_in_dx_conv"))
    dh1 = _mm(drest, w_rest, mode="nt", add=dh1, out_dtype=BF16, name=nm("mm_in_dx_rest"))
    dx, dx_b, G['norm_mix_g'] = _rms_bwd(_whole(sv['x']), dh1, W['norm_mix_g'], D_MODEL, dx2, nm("rms_mix_bwd"))
    return dx, dx_b, G


def _loss_head(x, target, g, name):
    def fn(x, t, g):
        r = lax.rsqrt(jnp.mean(x * x, axis=1, keepdims=True) + RMS_EPS)
        xh = x * r
        e = xh * g - t
        part = 0.5 * jnp.sum(jnp.mean(e * e, axis=1, keepdims=True), axis=0, keepdims=True)
        dy = e * (1.0 / D_MODEL)
        dxh = dy * g
        dx = r * (dxh - xh * jnp.mean(dxh * xh, axis=1, keepdims=True))
        return dx, dx, jnp.broadcast_to(part, (1, LANES)), _colsum(dy * xh)

    S = x.shape[0]
    dx, dx_b, part, dg = _rowwise(name, fn, S, [_whole(x), _whole(target)], [g], [(D_MODEL, F32), (D_MODEL, BF16)],
                                  [(1, LANES), (1, D_MODEL)])
    return part[0, 0], dx, dx_b, dg


def kernel(x, positions, norm_mix_g, w_in, b_gate, q_norm_g, w_uq, kv_norm_g, w_ukv, b_forget, dw_kernel, dw_bias, conv_ln_g, conv_ln_b, w_bo_a, w_bo_b, w_bo_c, w_out, norm_ffn_g, w_ffn_gate, w_ffn_up, w_ffn_down, final_norm_g, loss_target, m_norm_mix_g, m_w_in, m_b_gate, m_q_norm_g, m_w_uq, m_kv_norm_g, m_w_ukv, m_b_forget, m_dw_kernel, m_dw_bias, m_conv_ln_g, m_conv_ln_b, m_w_bo_a, m_w_bo_b, m_w_bo_c, m_w_out, m_norm_ffn_g, m_w_ffn_gate, m_w_ffn_up, m_w_ffn_down, m_final_norm_g, v_norm_mix_g, v_w_in, v_b_gate, v_q_norm_g, v_w_uq, v_kv_norm_g, v_w_ukv, v_b_forget, v_dw_kernel, v_dw_bias, v_conv_ln_g, v_conv_ln_b, v_w_bo_a, v_w_bo_b, v_w_bo_c, v_w_out, v_norm_ffn_g, v_w_ffn_gate, v_w_ffn_up, v_w_ffn_down, v_final_norm_g):
    local = dict(norm_mix_g=norm_mix_g, w_in=w_in, b_gate=b_gate, q_norm_g=q_norm_g, w_uq=w_uq, kv_norm_g=kv_norm_g,
                 w_ukv=w_ukv, b_forget=b_forget, dw_kernel=dw_kernel, dw_bias=dw_bias, conv_ln_g=conv_ln_g,
                 conv_ln_b=conv_ln_b, w_bo_a=w_bo_a, w_bo_b=w_bo_b, w_bo_c=w_bo_c, w_out=w_out, norm_ffn_g=norm_ffn_g,
                 w_ffn_gate=w_ffn_gate, w_ffn_up=w_ffn_up, w_ffn_down=w_ffn_down, final_norm_g=final_norm_g)
    mom_m = dict(norm_mix_g=m_norm_mix_g, w_in=m_w_in, b_gate=m_b_gate, q_norm_g=m_q_norm_g, w_uq=m_w_uq,
                 kv_norm_g=m_kv_norm_g, w_ukv=m_w_ukv, b_forget=m_b_forget, dw_kernel=m_dw_kernel, dw_bias=m_dw_bias,
                 conv_ln_g=m_conv_ln_g, conv_ln_b=m_conv_ln_b, w_bo_a=m_w_bo_a, w_bo_b=m_w_bo_b, w_bo_c=m_w_bo_c,
                 w_out=m_w_out, norm_ffn_g=m_norm_ffn_g, w_ffn_gate=m_w_ffn_gate, w_ffn_up=m_w_ffn_up,
                 w_ffn_down=m_w_ffn_down, final_norm_g=m_final_norm_g)
    mom_v = dict(norm_mix_g=v_norm_mix_g, w_in=v_w_in, b_gate=v_b_gate, q_norm_g=v_q_norm_g, w_uq=v_w_uq,
                 kv_norm_g=v_kv_norm_g, w_ukv=v_w_ukv, b_forget=v_b_forget, dw_kernel=v_dw_kernel, dw_bias=v_dw_bias,
                 conv_ln_g=v_conv_ln_g, conv_ln_b=v_conv_ln_b, w_bo_a=v_w_bo_a, w_bo_b=v_w_bo_b, w_bo_c=v_w_bo_c,
                 w_out=v_w_out, norm_ffn_g=v_norm_ffn_g, w_ffn_gate=v_w_ffn_gate, w_ffn_up=v_w_ffn_up,
                 w_ffn_down=v_w_ffn_down, final_norm_g=v_final_norm_g)
    S = x.shape[1]
    xs = x[0]
    sh_names = [n for n, _ in SHARDED]
    sh_shapes = [local[n].shape for n in sh_names]
    rep_shapes = [local[n].shape for n in REPLICATED]

    w_flat = _pack_flat([local[n] for n in sh_names], FLAT_ROW_MULTIPLE)
    gathered = _gather_two_level(w_flat.astype(BF16), "gather_weights")
    full = _full_weights(gathered, sh_shapes)

    def layer_weights(l):
        W = {n: full[n][l] for n in ('w_bo_a', 'w_bo_b', 'w_bo_c', 'w_out', 'w_ffn_down')}
        W['w_in'] = _rearrange_w_in(full['w_in'][l])
        W['w_uq'] = _rearrange_w_uq(full['w_uq'][l])
        W['w_ukv'] = _rearrange_w_ukv(full['w_ukv'][l])
        W['w_gu'] = jnp.concatenate([full['w_ffn_gate'][l], full['w_ffn_up'][l]], axis=1)
        W['dw_kernel'] = jnp.pad(full['dw_kernel'][l].astype(F32), ((0, 32 - CONV_WIDTH), (0, 0)))
        for n in ('norm_mix_g', 'b_gate', 'kv_norm_g', 'dw_bias', 'conv_ln_g', 'conv_ln_b', 'norm_ffn_g'):
            W[n] = local[n][l][None, :]
        W['q_norm_g'] = jnp.pad(local['q_norm_g'][l], (0, CQ_PAD - Q_LORA))[None, :]
        W['b_forget'] = local['b_forget'][l][:, None]
        return W

    inv_freq = 1.0 / (ROPE_THETA ** (jnp.arange(0, MLA_ROPE, 2, dtype=F32) / MLA_ROPE))
    ang = positions[0].astype(F32)[:, None] * inv_freq
    cos, sin = jnp.cos(ang), jnp.sin(ang)
    zpad = jnp.zeros((S, LANES - MLA_ROPE), F32)
    T = dict(cos_q=jnp.tile(cos, (1, MLA_HEADS)), sin_q=jnp.tile(sin, (1, MLA_HEADS)),
             cos_k=jnp.concatenate([cos, cos, zpad], axis=1), sin_k=jnp.concatenate([-sin, sin, zpad], axis=1),
             sin_k_neg=jnp.concatenate([sin, -sin, zpad], axis=1))

    Ws, saved = [], []
    h = xs
    for l in range(DEPTH):
        W = layer_weights(l)
        h, sv = _layer_forward(h, W, T, l)
        Ws.append(W)
        saved.append(sv)
    loss_part, dh, dh_b, dg_final = _loss_head(h, loss_target[0], local['final_norm_g'][None, :], "loss_head")
    loss = lax.psum(loss_part, ("x", "y", "c"))
    layer_grads = [None] * DEPTH
    for l in range(DEPTH - 1, -1, -1):
        dh, dh_b, layer_grads[l] = _layer_backward(dh, dh_b, saved[l], Ws[l], T, l)
    grad_x = dh[None]

    grads_full = {}
    grads_full['w_in'] = jnp.stack([_restore_w_in(g['w_in']) for g in layer_grads])
    grads_full['w_uq'] = jnp.stack([_restore_w_uq(g['w_uq']) for g in layer_grads])
    grads_full['w_ukv'] = jnp.stack([_restore_w_ukv(g['w_ukv']) for g in layer_grads])
    grads_full['dw_kernel'] = jnp.stack([g['dw_kernel'][:CONV_WIDTH] for g in layer_grads])
    for n in ('w_bo_a', 'w_bo_b', 'w_bo_c', 'w_out', 'w_ffn_down'):
        grads_full[n] = jnp.stack([g[n] for g in layer_grads])
    grads_full['w_ffn_gate'] = jnp.stack([g['w_gu'][:, :FFN_HIDDEN] for g in layer_grads])
    grads_full['w_ffn_up'] = jnp.stack([g['w_gu'][:, FFN_HIDDEN:] for g in layer_grads])
    packed = _pack_grads(grads_full, sh_shapes)
    received = _exchange(packed, False, "scatter_grads")
    R = w_flat.shape[0]
    m_flat = _pack_flat([mom_m[n] for n in sh_names], FLAT_ROW_MULTIPLE)
    v_flat = _pack_flat([mom_v[n] for n in sh_names], FLAT_ROW_MULTIPLE)
    g_sh, d_sh, nm_sh, nv_sh = _adamw_sum("adamw_sharded", received.reshape(N_DEV * R, LANES), w_flat, m_flat, v_flat,
                                          512)

    rep_grads = {
        'norm_mix_g': jnp.concatenate([g['norm_mix_g'] for g in layer_grads]),
        'b_gate': jnp.concatenate([g['b_gate'] for g in layer_grads]),
        'q_norm_g': jnp.concatenate([g['q_norm_g'][:, :Q_LORA] for g in layer_grads]),
        'kv_norm_g': jnp.concatenate([g['kv_norm_g'] for g in layer_grads]),
        'b_forget': jnp.concatenate([g['b_forget'].T for g in layer_grads]),
        'dw_bias': jnp.concatenate([g['dw_bias'] for g in layer_grads]),
        'conv_ln_g': jnp.concatenate([g['conv_ln_g'] for g in layer_grads]),
        'conv_ln_b': jnp.concatenate([g['conv_ln_b'] for g in layer_grads]),
        'norm_ffn_g': jnp.concatenate([g['norm_ffn_g'] for g in layer_grads]),
        'final_norm_g': dg_final[0],
    }
    REP_ROWS = 256
    rg_flat = _pack_flat([rep_grads[n] for n in REPLICATED], REP_ROWS)
    rg_all = _exchange(rg_flat, True, "gather_replicated_grads")
    Rr = rg_flat.shape[0]
    rw = _pack_flat([local[n] for n in REPLICATED], REP_ROWS)
    rm = _pack_flat([mom_m[n] for n in REPLICATED], REP_ROWS)
    rv = _pack_flat([mom_v[n] for n in REPLICATED], REP_ROWS)
    g_rp, d_rp, nm_rp, nv_rp = _adamw_sum("adamw_replicated", rg_all.reshape(N_DEV * Rr, LANES), rw, rm, rv, Rr)

    def by_name(flat_sh, flat_rp):
        d = dict(zip(sh_names, _unpack_flat(flat_sh, sh_shapes)))
        d.update(zip(REPLICATED, _unpack_flat(flat_rp, rep_shapes)))
        return [d[n] for n in WEIGHT_NAMES]

    return (loss, grad_x, *by_name(g_sh, g_rp), *by_name(d_sh, d_rp), *by_name(nm_sh, nm_rp), *by_name(nv_sh, nv_rp))
```
